```python
import jax, jax.numpy as jnp
from jax import lax
import numpy as np

D_MODEL = 2048
BATCH = 1
SEQ = 8192
DEPTH = 1
DEC_BATCH = 32
DEC_SEQ = 8
PAST_LEN = 8192
PAGE_SIZE = 128

HEAD_DIM = 64
NSA_HEADS = 16
NSA_KV_HEADS = 4
NSA_GROUP = NSA_HEADS // NSA_KV_HEADS
NSA_WIDTH = NSA_HEADS * HEAD_DIM
NSA_BRANCHES = 3
CMP_LEN = 32
CMP_STRIDE = 16
CMP_HIDDEN = 2 * HEAD_DIM
SEL_BLOCK = 64
N_SELECT = 16
N_LOCAL = 2
WINDOW = 512
Q_BLOCK = 128
GLA_HEADS = 4
GLA_DK = 128
GLA_DV = 256
GLA_KEY_WIDTH = GLA_HEADS * GLA_DK
GLA_WIDTH = GLA_HEADS * GLA_DV
GLA_RANK = 16
GLA_TAU = 16.0
GLA_CHUNK = 64
ROPE_THETA = 10000.0
EPS = 1e-6
NEG = -1e30
BIG = 1e30
TINY = 1e-30

IN_SPLIT = (NSA_WIDTH,
            NSA_BRANCHES * 2 * NSA_KV_HEADS * HEAD_DIM,
            NSA_HEADS * NSA_BRANCHES,
            NSA_WIDTH,
            GLA_KEY_WIDTH, GLA_KEY_WIDTH, GLA_WIDTH,
            GLA_RANK,
            GLA_WIDTH,
            D_MODEL, D_MODEL)
IN_WIDTH = sum(IN_SPLIT)

kernel_name = 'nsa_gla_gated_hybrid_step'


def rms_norm(x, w):
    xf = x.astype(jnp.float32)
    y = xf * lax.rsqrt(jnp.mean(xf * xf, axis=-1, keepdims=True) + EPS)
    return (y * w.astype(jnp.float32)).astype(x.dtype)


def rope(x, pos):
    half = x.shape[-1] // 2
    inv = ROPE_THETA ** (-jnp.arange(half, dtype=jnp.float32) / half)
    ang = pos.astype(jnp.float32)[:, None] * inv[None, :]
    cos = jnp.cos(ang)[:, None, :]
    sin = jnp.sin(ang)[:, None, :]
    xf = x.astype(jnp.float32)
    x1, x2 = xf[..., :half], xf[..., half:]
    return jnp.concatenate([x1 * cos - x2 * sin, x2 * cos + x1 * sin], axis=-1).astype(x.dtype)


def masked_softmax(s, mask):
    s = jnp.where(mask, s.astype(jnp.float32), NEG)
    p = jnp.exp(s - jnp.max(s, axis=-1, keepdims=True)) * mask
    return p / jnp.maximum(jnp.sum(p, axis=-1, keepdims=True), TINY)


def compress_blocks(kv, cmp_pos, cmp_w1, cmp_b1, cmp_w2, cmp_b2):
    b, t = kv.shape[:2]
    r = CMP_LEN // CMP_STRIDE
    n_chunk = -(-t // CMP_STRIDE)
    kv = jnp.pad(kv, ((0, 0), (0, n_chunk * CMP_STRIDE - t), (0, 0), (0, 0), (0, 0)))
    ch = kv.reshape(b, n_chunk, CMP_STRIDE, 2, NSA_KV_HEADS, HEAD_DIM)
    w1 = cmp_w1.reshape(2, r, CMP_STRIDE, HEAD_DIM, CMP_HIDDEN)
    part = jnp.einsum('bnpxhd,xjpde->jbnxhe', ch, w1)
    nc = n_chunk - r + 1
    hid = part[0][:, :nc]
    for j in range(1, r):
        hid = hid + part[j][:, j:j + nc]
    pos_bias = jnp.einsum('xld,xlde->xe', cmp_pos, cmp_w1) + cmp_b1
    hid = jax.nn.silu(hid + pos_bias[:, None, :])
    out = jnp.einsum('bnxhe,xed->bnxhd', hid, cmp_w2) + cmp_b2[:, None, :]
    c_end = jnp.arange(nc, dtype=jnp.int32) * CMP_STRIDE + (CMP_LEN - 1)
    return out, c_end


def cmp_to_sel(nc, nb):
    cs = jnp.arange(nc)[:, None] * CMP_STRIDE
    js = jnp.arange(nb)[None, :] * SEL_BLOCK
    return ((cs < js + SEL_BLOCK) & (cs + CMP_LEN > js)).astype(jnp.float32)


def to_sel_blocks(kv):
    b, t = kv.shape[:2]
    nb = -(-t // SEL_BLOCK)
    kv = jnp.pad(kv, ((0, 0), (0, nb * SEL_BLOCK - t), (0, 0), (0, 0), (0, 0)))
    return kv.reshape(b, nb, SEL_BLOCK, 2, NSA_KV_HEADS, HEAD_DIM)


def nsa_attend(q, q_pos, kv_c, c_end, kv_s, kv_w, w_pos):
    b, nq = q.shape[:2]
    nc, nb = kv_c.shape[1], kv_s.shape[1]
    s_c = jnp.einsum('bqhgd,bchd->bhgqc', q, kv_c[:, :, 0])
    p_c = masked_softmax(s_c, c_end[None, :] <= q_pos[:, None])
    o_c = jnp.einsum('bhgqc,bchd->bqhgd', p_c.astype(q.dtype), kv_c[:, :, 1])
    imp = jnp.einsum('bhgqc,cn->bhqn', p_c, cmp_to_sel(nc, nb))
    blk = jnp.arange(nb, dtype=jnp.int32)[None, :]
    cur = (q_pos // SEL_BLOCK)[:, None]
    forced = (blk == 0) | ((blk <= cur) & (blk > cur - N_LOCAL))
    imp = jnp.where(forced, BIG, imp)
    imp = jnp.where(blk > cur, -BIG, imp)
    top_v, top_i = lax.top_k(imp, min(N_SELECT, nb))
    n_sel = top_i.shape[-1]
    sel_ok = top_v > -0.5 * BIG
    kv_sh = jnp.moveaxis(kv_s, 4, 1)
    g = jax.vmap(jax.vmap(lambda a, i: a[i]))(kv_sh, top_i)
    key_pos = top_i[..., None] * SEL_BLOCK + jnp.arange(SEL_BLOCK, dtype=jnp.int32)
    m_s = sel_ok[..., None] & (key_pos <= q_pos[:, None, None])
    s_s = jnp.einsum('bqhgd,bhqnsd->bhgqns', q, g[..., 0, :])
    nk = n_sel * SEL_BLOCK
    p_s = masked_softmax(s_s.reshape(b, NSA_KV_HEADS, NSA_GROUP, nq, nk),
                         m_s.reshape(b, NSA_KV_HEADS, nq, nk)[:, :, None])
    o_s = jnp.einsum('bhgqk,bhqkd->bqhgd', p_s.astype(q.dtype),
                     g[..., 1, :].reshape(b, NSA_KV_HEADS, nq, nk, HEAD_DIM))
    s_w = jnp.einsum('bqhgd,bkhd->bhgqk', q, kv_w[:, :, 0])
    m_w = ((w_pos[None, :] <= q_pos[:, None]) & (w_pos[None, :] > q_pos[:, None] - WINDOW)
           & (w_pos[None, :] >= 0))
    p_w = masked_softmax(s_w, m_w)
    o_w = jnp.einsum('bhgqk,bkhd->bqhgd', p_w.astype(q.dtype), kv_w[:, :, 1])
    return o_c, o_s, o_w


def nsa_prompt(q, kvc, kvs, kvw, cmp_pos, cmp_w1, cmp_b1, cmp_w2, cmp_b2):
    b, t = q.shape[:2]
    kv_c, c_end = compress_blocks(kvc, cmp_pos, cmp_w1, cmp_b1, cmp_w2, cmp_b2)
    kv_s = to_sel_blocks(kvs)
    kvw_pad = jnp.pad(kvw, ((0, 0), (WINDOW, 0), (0, 0), (0, 0), (0, 0)))

    def one_block(i):
        s = i * Q_BLOCK
        qb = lax.dynamic_slice_in_dim(q, s, Q_BLOCK, axis=1)
        qp = s + jnp.arange(Q_BLOCK, dtype=jnp.int32)
        kwb = lax.dynamic_slice_in_dim(kvw_pad, s, WINDOW + Q_BLOCK, axis=1)
        wp = s - WINDOW + jnp.arange(WINDOW + Q_BLOCK, dtype=jnp.int32)
        return nsa_attend(qb, qp, kv_c, c_end, kv_s, kwb, wp)

    o_c, o_s, o_w = lax.map(one_block, jnp.arange(t // Q_BLOCK, dtype=jnp.int32))
    shp = (b, t, NSA_KV_HEADS, NSA_GROUP, HEAD_DIM)
    return (jnp.moveaxis(o_c, 0, 1).reshape(shp), jnp.moveaxis(o_s, 0, 1).reshape(shp),
            jnp.moveaxis(o_w, 0, 1).reshape(shp))


def gather_pages(pool, page_table):
    rows = pool[page_table]
    return rows.reshape(page_table.shape[0], page_table.shape[1] * pool.shape[1], *pool.shape[2:])


def nsa_sample(q, kvc, kvs, kvw, cache_c, cache_s, cache_w, page_table,
               cmp_pos, cmp_w1, cmp_b1, cmp_w2, cmp_b2):
    nq = q.shape[1]
    past = page_table.shape[1] * cache_c.shape[1]
    full_c = jnp.concatenate([gather_pages(cache_c, page_table), kvc], axis=1)
    full_s = jnp.concatenate([gather_pages(cache_s, page_table), kvs], axis=1)
    kv_c, c_end = compress_blocks(full_c, cmp_pos, cmp_w1, cmp_b1, cmp_w2, cmp_b2)
    kv_s = to_sel_blocks(full_s)
    wb = cache_w.shape[1]
    kv_w = jnp.concatenate([cache_w, kvw], axis=1)
    w_pos = past - wb + jnp.arange(wb + nq, dtype=jnp.int32)
    q_pos = past + jnp.arange(nq, dtype=jnp.int32)
    return nsa_attend(q, q_pos, kv_c, c_end, kv_s, kv_w, w_pos)


def gla_scan(q, k, v, log_a, s0):
    b, t, h, _ = q.shape
    c = min(GLA_CHUNK, t)
    n = -(-t // c)
    pad = n * c - t

    def prep(a):
        a = jnp.pad(a.astype(jnp.float32), ((0, 0), (0, pad), (0, 0), (0, 0)))
        return jnp.moveaxis(a.reshape(b, n, c, h, a.shape[-1]), 1, 0)

    causal = jnp.tril(jnp.ones((c, c), bool))[None, :, :, None, None]

    def step(S, inp):
        qc, kc, vc, lc = inp
        cum = jnp.cumsum(lc, axis=1)
        o_inter = jnp.einsum('bthk,bhkv->bthv', qc * jnp.exp(cum), S)
        decay = jnp.exp(jnp.where(causal, cum[:, :, None] - cum[:, None, :], NEG))
        att = jnp.einsum('bthk,bshk,btshk->bhts', qc, kc, decay)
        o_intra = jnp.einsum('bhts,bshv->bthv', att, vc)
        c_last = cum[:, -1]
        S = (jnp.exp(c_last)[..., None] * S
             + jnp.einsum('bshk,bshv->bhkv', kc * jnp.exp(c_last[:, None] - cum), vc))
        return S, o_inter + o_intra

    S, o = lax.scan(step, s0.astype(jnp.float32), (prep(q), prep(k), prep(v), prep(log_a)))
    o = jnp.moveaxis(o, 0, 1).reshape(b, n * c, h, GLA_DV)[:, :t]
    return o, S


def sublayer(x, c, pos, s0, nsa_fn, norm_w, w_ada, b_ada, w_in, w_a2, b_a, gla_norm_w,
             w_o_nsa, w_o_gla, w_out):
    b, t, _ = x.shape
    mod = jnp.einsum('bd,de->be', c, w_ada) + b_ada
    shift, scale, gate = jnp.split(mod, 3, axis=-1)
    h = rms_norm(x, norm_w) * (1.0 + scale[:, None, :]) + shift[:, None, :]
    cuts = np.cumsum(IN_SPLIT)[:-1].tolist()
    (q_n, kv_n, g_n, z_n, q_g, k_g, v_g, a_g, z_g, m_n, m_g) = jnp.split(
        jnp.einsum('btd,de->bte', h, w_in), cuts, axis=-1)
    q = rope(q_n.reshape(b, t, NSA_HEADS, HEAD_DIM), pos) * (HEAD_DIM ** -0.5)
    q = q.reshape(b, t, NSA_KV_HEADS, NSA_GROUP, HEAD_DIM)
    kv = kv_n.reshape(b, t, NSA_BRANCHES, 2, NSA_KV_HEADS, HEAD_DIM)
    k = rope(kv[:, :, :, 0].reshape(b, t, NSA_BRANCHES * NSA_KV_HEADS, HEAD_DIM), pos)
    k = k.reshape(b, t, NSA_BRANCHES, NSA_KV_HEADS, HEAD_DIM)
    kv = jnp.stack([k, kv[:, :, :, 1]], axis=3)
    kvc, kvs, kvw = kv[:, :, 0], kv[:, :, 1], kv[:, :, 2]
    o_c, o_s, o_w = nsa_fn(q, kvc, kvs, kvw)
    gb = jax.nn.sigmoid(g_n).reshape(b, t, NSA_KV_HEADS, NSA_GROUP, NSA_BRANCHES)
    o_nsa = (gb[..., 0:1] * o_c + gb[..., 1:2] * o_s + gb[..., 2:3] * o_w).reshape(b, t, NSA_WIDTH)
    o_nsa = o_nsa * jax.nn.silu(z_n)
    qg = q_g.reshape(b, t, GLA_HEADS, GLA_DK) * (GLA_DK ** -0.5)
    kg = k_g.reshape(b, t, GLA_HEADS, GLA_DK)
    vg = v_g.reshape(b, t, GLA_HEADS, GLA_DV)
    log_a = jax.nn.log_sigmoid((jnp.einsum('btr,re->bte', a_g, w_a2) + b_a).astype(jnp.float32)) / GLA_TAU
    o_g, s_new = gla_scan(qg, kg, vg, log_a.reshape(b, t, GLA_HEADS, GLA_DK), s0)
    o_gla = rms_norm(o_g, gla_norm_w).astype(x.dtype).reshape(b, t, GLA_WIDTH) * jax.nn.silu(z_g)
    merged = (jax.nn.sigmoid(m_n) * jnp.einsum('bte,ed->btd', o_nsa, w_o_nsa)
              + jax.nn.sigmoid(m_g) * jnp.einsum('bte,ed->btd', o_gla, w_o_gla))
    y = x + gate[:, None, :] * jnp.einsum('btd,de->bte', merged, w_out)
    return y, kvc, kvs, kvw, s_new


def setup_inputs(seed: int = 0) -> dict:
    key = jax.random.key(seed)
    ks = jax.random.split(key, 25)
    n_pages = PAST_LEN // PAGE_SIZE
    n_pool = (DEC_BATCH * n_pages * 5) // 4
    win_buf = min(WINDOW, PAST_LEN)
    kv_row = (2, NSA_KV_HEADS, HEAD_DIM)

    def nrm(k, shape, scale):
        return scale * jax.random.normal(k, shape, jnp.float32)

    page_table = jax.random.permutation(ks[6], n_pool)[: DEC_BATCH * n_pages]
    page_table = page_table.reshape(DEC_BATCH, n_pages).astype(jnp.int32)
    return {
        'x_prompt': nrm(ks[0], (BATCH, SEQ, D_MODEL), 1.0),
        'x_sample': nrm(ks[1], (DEC_BATCH, DEC_SEQ, D_MODEL), 1.0),
        'cache_kv_cmp': nrm(ks[2], (DEPTH, n_pool, PAGE_SIZE) + kv_row, 1.0),
        'cache_kv_sel': nrm(ks[3], (DEPTH, n_pool, PAGE_SIZE) + kv_row, 1.0),
        'cache_kv_win': nrm(ks[4], (DEPTH, DEC_BATCH, win_buf) + kv_row, 1.0),
        'state_gla': nrm(ks[5], (DEPTH, DEC_BATCH, GLA_HEADS, GLA_DK, GLA_DV), 1.0),
        'page_table': page_table,
        'c_prompt': nrm(ks[7], (BATCH, D_MODEL), 1.0),
        'c_sample': nrm(ks[8], (DEC_BATCH, D_MODEL), 1.0),
        'norm_w': 1.0 + nrm(ks[9], (DEPTH, D_MODEL), 0.02),
        'w_ada': nrm(ks[10], (DEPTH, D_MODEL, 3 * D_MODEL), 0.5 * D_MODEL ** -0.5),
        'b_ada': nrm(ks[11], (DEPTH, 3 * D_MODEL), 0.01),
        'w_in': nrm(ks[12], (DEPTH, D_MODEL, IN_WIDTH), D_MODEL ** -0.5),
        'cmp_pos': nrm(ks[13], (DEPTH, 2, CMP_LEN, HEAD_DIM), 0.1),
        'cmp_w1': nrm(ks[14], (DEPTH, 2, CMP_LEN, HEAD_DIM, CMP_HIDDEN), (CMP_LEN * HEAD_DIM) ** -0.5),
        'cmp_b1': nrm(ks[15], (DEPTH, 2, CMP_HIDDEN), 0.01),
        'cmp_w2': nrm(ks[16], (DEPTH, 2, CMP_HIDDEN, HEAD_DIM), CMP_HIDDEN ** -0.5),
        'cmp_b2': nrm(ks[17], (DEPTH, 2, HEAD_DIM), 0.01),
        'w_a2': nrm(ks[18], (DEPTH, GLA_RANK, GLA_KEY_WIDTH), GLA_RANK ** -0.5),
        'b_a': nrm(ks[19], (DEPTH, GLA_KEY_WIDTH), 0.1),
        'gla_norm_w': 1.0 + nrm(ks[20], (DEPTH, GLA_DV), 0.02),
        'w_o_nsa': nrm(ks[21], (DEPTH, NSA_WIDTH, D_MODEL), NSA_WIDTH ** -0.5),
        'w_o_gla': nrm(ks[22], (DEPTH, GLA_WIDTH, D_MODEL), GLA_WIDTH ** -0.5),
        'w_out': nrm(ks[23], (DEPTH, D_MODEL, D_MODEL), D_MODEL ** -0.5),
        'final_norm_w': 1.0 + nrm(ks[24], (D_MODEL,), 0.02),
    }


def reference(x_prompt, x_sample, cache_kv_cmp, cache_kv_sel, cache_kv_win, state_gla, page_table,
              c_prompt, c_sample, norm_w, w_ada, b_ada, w_in, cmp_pos, cmp_w1, cmp_b1, cmp_w2, cmp_b2,
              w_a2, b_a, gla_norm_w, w_o_nsa, w_o_gla, w_out, final_norm_w):
    t_p = x_prompt.shape[1]
    t_s = x_sample.shape[1]
    past = page_table.shape[1] * cache_kv_cmp.shape[2]
    wb = cache_kv_win.shape[2]
    pos_p = jnp.arange(t_p, dtype=jnp.int32)
    pos_s = past + jnp.arange(t_s, dtype=jnp.int32)
    s0_p = jnp.zeros((x_prompt.shape[0], GLA_HEADS, GLA_DK, GLA_DV), jnp.float32)
    xp, xs = x_prompt, x_sample
    cmp_p, cmp_s, sel_p, sel_s, win_p, win_s, st_p, st_s = [], [], [], [], [], [], [], []
    for l in range(DEPTH):
        cmp_l = (cmp_pos[l], cmp_w1[l], cmp_b1[l], cmp_w2[l], cmp_b2[l])
        shared = (norm_w[l], w_ada[l], b_ada[l], w_in[l], w_a2[l], b_a[l], gla_norm_w[l],
                  w_o_nsa[l], w_o_gla[l], w_out[l])
        nsa_p = lambda q, kc, ks, kw: nsa_prompt(q, kc, ks, kw, *cmp_l)
        xp, kvc_p, kvs_p, kvw_p, sp = sublayer(xp, c_prompt, pos_p, s0_p, nsa_p, *shared)
        nsa_s = lambda q, kc, ks, kw: nsa_sample(q, kc, ks, kw, cache_kv_cmp[l], cache_kv_sel[l],
                                                 cache_kv_win[l], page_table, *cmp_l)
        xs, kvc_s, kvs_s, kvw_s, ss = sublayer(xs, c_sample, pos_s, state_gla[l], nsa_s, *shared)
        cmp_p.append(kvc_p)
        cmp_s.append(kvc_s)
        sel_p.append(kvs_p)
        sel_s.append(kvs_s)
        win_p.append(kvw_p[:, t_p - min(WINDOW, t_p):])
        win_s.append(jnp.concatenate([cache_kv_win[l], kvw_s], axis=1)[:, t_s:])
        st_p.append(sp.astype(state_gla.dtype))
        st_s.append(ss.astype(state_gla.dtype))
    y_prompt = rms_norm(xp, final_norm_w)
    y_sample = rms_norm(xs, final_norm_w)
    kv_cmp_prompt = jnp.stack(cmp_p, axis=0)
    kv_cmp_sample = jnp.stack(cmp_s, axis=0)
    kv_sel_prompt = jnp.stack(sel_p, axis=0)
    kv_sel_sample = jnp.stack(sel_s, axis=0)
    kv_win_prompt = jnp.stack(win_p, axis=0)
    kv_win_sample = jnp.stack(win_s, axis=0)
    gla_state_prompt = jnp.stack(st_p, axis=0)
    gla_state_sample = jnp.stack(st_s, axis=0)
    return (y_prompt, y_sample, kv_cmp_prompt, kv_cmp_sample, kv_sel_prompt, kv_sel_sample,
            kv_win_prompt, kv_win_sample, gla_state_prompt, gla_state_sample)
```

```python
import functools

import jax
import jax.numpy as jnp
import numpy as np
from jax import lax
from jax.experimental import pallas as pl
from jax.experimental.pallas import tpu as pltpu

F32 = jnp.float32
BF16 = jnp.bfloat16

D_MODEL = 2048
HEAD_DIM = 64
NSA_HEADS = 16
NSA_KV_HEADS = 4
NSA_GROUP = 4
NSA_WIDTH = 1024
HALF_ROW = NSA_KV_HEADS * HEAD_DIM
KV_ROW = 2 * HALF_ROW
CMP_LEN = 32
CMP_STRIDE = 16
CMP_HIDDEN = 128
SEL_BLOCK = 64
N_SELECT = 16
N_LOCAL = 2
WINDOW = 512
Q_BLOCK = 128
PAGE = 128
GLA_HEADS = 4
GLA_DK = 128
GLA_DV = 256
GLA_RANK = 16
GLA_TAU = 16.0
GLA_CHUNK = 64
ROPE_THETA = 10000.0
EPS = 1e-6
NEG = -1e30
BIG = 1e30
TINY = 1e-30
REMOVED = -3e38

LANE = 128
VMEM_LIMIT = 48 * 1024 * 1024

C_MN, C_MG, C_ZN, C_VG, C_ZG, C_QG, C_KG, C_AG = 0, 2048, 4096, 5120, 6144, 7168, 7680, 8192
RM_COLS = 8320
R_Q, R_KV, R_GN = 0, 1024, 2560
FM_ROWS = 2688


def _cparams(sem):
    return pltpu.CompilerParams(dimension_semantics=sem, vmem_limit_bytes=VMEM_LIMIT)


def _dot(a, b):
    return jnp.dot(a, b, preferred_element_type=F32)


def _dot_nt(a, b):
    return lax.dot_general(a, b, (((1,), (1,)), ((), ())), preferred_element_type=F32)


def _silu(x):
    return x * jax.nn.sigmoid(x)


def _ada_kernel(c_ref, w_ref, b_ref, o_ref):
    o_ref[...] = _dot(c_ref[...].astype(BF16), w_ref[...].astype(BF16)) + b_ref[...]


def ada_mod(c_rows, w_ada, b_ada):
    rows, tn = c_rows.shape[0], 512
    n = w_ada.shape[1]
    return pl.pallas_call(
        _ada_kernel,
        grid=(n // tn,),
        in_specs=[pl.BlockSpec((rows, D_MODEL), lambda j: (0, 0)),
                  pl.BlockSpec((D_MODEL, tn), lambda j: (0, j)),
                  pl.BlockSpec((1, tn), lambda j: (0, j))],
        out_specs=pl.BlockSpec((rows, tn), lambda j: (0, j)),
        out_shape=jax.ShapeDtypeStruct((rows, n), F32),
        compiler_params=_cparams(("parallel",)),
        name="ada",
    )(c_rows, w_ada, b_ada.reshape(1, n))


def _modulated_norm(x_ref, sc_ref, sh_ref, nw_ref, h_ref):
    x = x_ref[...]
    y = x * lax.rsqrt(jnp.mean(x * x, axis=-1, keepdims=True) + EPS) * nw_ref[...]
    h_ref[...] = (y * (1.0 + sc_ref[...]) + sh_ref[...]).astype(BF16)


def _inproj_rm_kernel(x_ref, sc_ref, sh_ref, nw_ref, w_ref, o_ref, h_ref):
    @pl.when(pl.program_id(1) == 0)
    def _():
        _modulated_norm(x_ref, sc_ref, sh_ref, nw_ref, h_ref)

    o_ref[...] = _dot_nt(h_ref[...], w_ref[...])


def _inproj_fm_kernel(x_ref, sc_ref, sh_ref, nw_ref, w_ref, o_ref, h_ref):
    @pl.when(pl.program_id(1) == 0)
    def _():
        _modulated_norm(x_ref, sc_ref, sh_ref, nw_ref, h_ref)

    o_ref[...] = _dot_nt(w_ref[...], h_ref[...])


def in_proj(x, scale, shift, norm_w, w_t, tm, tn, feature_major):
    rows, n = x.shape[0], w_t.shape[0]
    per_row = scale.shape[0] != 1
    mod_spec = pl.BlockSpec((tm, D_MODEL), lambda i, j: (i, 0)) if per_row else pl.BlockSpec((1, D_MODEL), lambda i, j: (0, 0))
    if feature_major:
        body, out_spec, out_shape = _inproj_fm_kernel, pl.BlockSpec((tn, tm), lambda i, j: (j, i)), (n, rows)
    else:
        body, out_spec, out_shape = _inproj_rm_kernel, pl.BlockSpec((tm, tn), lambda i, j: (i, j)), (rows, n)
    return pl.pallas_call(
        body,
        grid=(rows // tm, n // tn),
        in_specs=[pl.BlockSpec((tm, D_MODEL), lambda i, j: (i, 0)),
                  mod_spec, mod_spec,
                  pl.BlockSpec((1, D_MODEL), lambda i, j: (0, 0)),
                  pl.BlockSpec((tn, D_MODEL), lambda i, j: (j, 0))],
        out_specs=out_spec,
        out_shape=jax.ShapeDtypeStruct(out_shape, F32),
        scratch_shapes=[pltpu.VMEM((tm, D_MODEL), BF16)],
        compiler_params=_cparams(("parallel", "arbitrary")),
        name="inproj_fm" if feature_major else "inproj_rm",
    )(x, scale, shift, norm_w.reshape(1, D_MODEL), w_t)


def _rope_kernel(q_ref, c_ref, s_ref, w_ref, cos_ref, sin_ref, qo_ref, co_ref, so_ref, wo_ref, *tile_refs):
    cos, sin = cos_ref[...], sin_ref[...]
    hh = HEAD_DIM // 2
    tr = cos.shape[1]

    def rot(src, head):
        x1 = src[head * HEAD_DIM:head * HEAD_DIM + hh, :]
        x2 = src[head * HEAD_DIM + hh:(head + 1) * HEAD_DIM, :]
        return x1 * cos - x2 * sin, x2 * cos + x1 * sin

    for head in range(NSA_HEADS):
        o1, o2 = rot(q_ref, head)
        qo_ref[head * HEAD_DIM:head * HEAD_DIM + hh, :] = (o1 * HEAD_DIM ** -0.5).astype(BF16)
        qo_ref[head * HEAD_DIM + hh:(head + 1) * HEAD_DIM, :] = (o2 * HEAD_DIM ** -0.5).astype(BF16)
    for src, dst in ((c_ref, co_ref), (s_ref, so_ref), (w_ref, wo_ref)):
        for head in range(NSA_KV_HEADS):
            o1, o2 = rot(src, head)
            dst[head * HEAD_DIM:head * HEAD_DIM + hh, :] = o1
            dst[head * HEAD_DIM + hh:(head + 1) * HEAD_DIM, :] = o2
        dst[HALF_ROW:, :] = src[HALF_ROW:, :]
    if tile_refs:
        ks_ref, kw_ref, vs_ref, vw_ref = tile_refs
        for dst, k_rows, v_tiles, v_tile in ((so_ref, ks_ref, vs_ref, SEL_TILE), (wo_ref, kw_ref, vw_ref, Q_BLOCK)):
            for pair in range(NSA_KV_HEADS // 2):
                k_pair = dst[pair * LANE:(pair + 1) * LANE, :].T.astype(BF16)
                k_rows[2 * pair] = k_pair[:, :HEAD_DIM]
                k_rows[2 * pair + 1] = k_pair[:, HEAD_DIM:]
            for head in range(NSA_KV_HEADS):
                v = dst[HALF_ROW + head * HEAD_DIM:HALF_ROW + (head + 1) * HEAD_DIM, :].astype(BF16)
                for w in range(tr // v_tile):
                    v_tiles[head, w] = v[:, w * v_tile:(w + 1) * v_tile]


def rope_stage(proj_t, cos_t, sin_t, tr, with_tiles):
    tok = proj_t.shape[1]
    kv_spec = lambda k: pl.BlockSpec((KV_ROW, tr), lambda i, k=k: (R_KV // KV_ROW + k, i))
    out_kv = jax.ShapeDtypeStruct((KV_ROW, tok), F32)
    tab = pl.BlockSpec((HEAD_DIM // 2, tr), lambda i: (0, i))
    out_specs = [pl.BlockSpec((NSA_WIDTH, tr), lambda i: (0, i))] + [pl.BlockSpec((KV_ROW, tr), lambda i: (0, i))] * 3
    out_shape = [jax.ShapeDtypeStruct((NSA_WIDTH, tok), BF16), out_kv, out_kv, out_kv]
    if with_tiles:
        k_rows = jax.ShapeDtypeStruct((NSA_KV_HEADS, tok, HEAD_DIM), BF16)
        k_spec = pl.BlockSpec((NSA_KV_HEADS, tr, HEAD_DIM), lambda i: (0, i, 0))
        v_tiles = lambda tile: jax.ShapeDtypeStruct((NSA_KV_HEADS, tok // tile, HEAD_DIM, tile), BF16)
        v_spec = lambda tile: pl.BlockSpec((NSA_KV_HEADS, tr // tile, HEAD_DIM, tile), lambda i: (0, i, 0, 0))
        out_specs += [k_spec, k_spec, v_spec(SEL_TILE), v_spec(Q_BLOCK)]
        out_shape += [k_rows, k_rows, v_tiles(SEL_TILE), v_tiles(Q_BLOCK)]
    return pl.pallas_call(
        _rope_kernel,
        grid=(tok // tr,),
        in_specs=[pl.BlockSpec((NSA_WIDTH, tr), lambda i: (R_Q // NSA_WIDTH, i)), kv_spec(0), kv_spec(1), kv_spec(2), tab, tab],
        out_specs=out_specs,
        out_shape=out_shape,
        compiler_params=_cparams(("parallel",)),
        name="rope",
    )(proj_t, proj_t, proj_t, proj_t, cos_t, sin_t)


def _rope_tables(pos):
    half = HEAD_DIM // 2
    inv = ROPE_THETA ** (-jnp.arange(half, dtype=F32) / half)
    ang = inv[:, None] * pos.astype(F32)[None, :]
    return jnp.cos(ang), jnp.sin(ang)


def _posbias_kernel(p_ref, w_ref, b_ref, o_ref):
    for x in range(2):
        o_ref[x] = _dot(p_ref[x], w_ref[x]) + b_ref[x]


def pos_bias(cmp_pos, cmp_w1, cmp_b1):
    k = CMP_LEN * HEAD_DIM
    pos = jnp.zeros((2, 8, k), F32).at[:, 0].set(cmp_pos.reshape(2, k))
    out = pl.pallas_call(
        _posbias_kernel,
        out_shape=jax.ShapeDtypeStruct((2, 8, CMP_HIDDEN), F32),
        compiler_params=pltpu.CompilerParams(vmem_limit_bytes=VMEM_LIMIT),
        name="posbias",
    )(pos, cmp_w1.reshape(2, k, CMP_HIDDEN), cmp_b1.reshape(2, 1, CMP_HIDDEN))
    return out[:, 0]


CMP_PAGES = 16
CMP_CHUNKS = CMP_PAGES * PAGE // CMP_STRIDE
CHUNKS_PER_PAGE = PAGE // CMP_STRIDE


def _compress_kernel(pt_ref, *refs):
    pages = refs[:CMP_PAGES]
    perm_ref, w1_ref, pb_ref, w2_ref, w2t_ref, b2_ref, b2c_ref, k_ref, vt_ref, carry_ref = refs[CMP_PAGES:]
    s = pl.program_id(1)

    @pl.when(s == 0)
    def _():
        carry_ref[...] = jnp.zeros_like(carry_ref)

    n = CMP_CHUNKS
    perm = perm_ref[...]
    rows_by_p = [_dot_nt(perm, pg[0].astype(BF16)) for pg in pages]
    row0 = lax.broadcasted_iota(jnp.int32, (n, 1), 0) == 0
    for t in range(KV_ROW // LANE):
        x = t // 2
        sl = slice(t * LANE, (t + 1) * LANE)
        acc = jnp.zeros((n, 4 * CMP_HIDDEN), F32)
        for pp in range(CMP_STRIDE // 2):
            parts = []
            for p in (2 * pp, 2 * pp + 1):
                parts.append(jnp.concatenate(
                    [r[p * CHUNKS_PER_PAGE:(p + 1) * CHUNKS_PER_PAGE, sl] for r in rows_by_p], axis=0))
            lhs = jnp.concatenate(parts, axis=1).astype(BF16)
            acc = acc + _dot(lhs, w1_ref[x, pp])
        hid = []
        for hh in range(2):
            part0 = acc[:, hh * 256:hh * 256 + CMP_HIDDEN]
            part1 = acc[:, hh * 256 + CMP_HIDDEN:(hh + 1) * 256]
            csl = slice((t * 2 + hh) * CMP_HIDDEN, (t * 2 + hh + 1) * CMP_HIDDEN)
            prev = jnp.where(row0, carry_ref[0:1, csl], pltpu.roll(part0, 1, 0))
            carry_ref[0:1, csl] = part0[n - 1:n, :]
            hid.append(_silu(prev + part1 + pb_ref[x:x + 1, :]))
        hid = jnp.concatenate(hid, axis=1).astype(BF16)
        if x == 0:
            k_ref[0, :, sl] = _dot(hid, w2_ref[...]) + b2_ref[...]
        else:
            tv = t - 2
            vt_ref[0, tv * LANE:(tv + 1) * LANE, :] = _dot_nt(w2t_ref[...], hid) + b2c_ref[...]


def compress(pool_t, page_table, perm, w1t, pb, w2k, w2vt, b2k, b2vc):
    b, n_pages = page_table.shape
    steps = n_pages // CMP_PAGES
    n_blk = n_pages * CHUNKS_PER_PAGE
    page_spec = lambda k: pl.BlockSpec((1, KV_ROW, PAGE), lambda bi, si, pt, k=k: (pt[bi, si * CMP_PAGES + k], 0, 0))
    const = lambda a: pl.BlockSpec(a.shape, lambda bi, si, pt: (0,) * a.ndim)
    consts = (perm, w1t, pb, w2k, w2vt, b2k, b2vc)
    grid_spec = pltpu.PrefetchScalarGridSpec(
        num_scalar_prefetch=1,
        grid=(b, steps),
        in_specs=[page_spec(k) for k in range(CMP_PAGES)] + [const(a) for a in consts],
        out_specs=[pl.BlockSpec((1, CMP_CHUNKS, HALF_ROW), lambda bi, si, pt: (bi, si, 0)),
                   pl.BlockSpec((1, HALF_ROW, CMP_CHUNKS), lambda bi, si, pt: (bi, 0, si))],
        scratch_shapes=[pltpu.VMEM((8, 8 * CMP_HIDDEN), F32)],
    )
    return pl.pallas_call(
        _compress_kernel,
        grid_spec=grid_spec,
        out_shape=[jax.ShapeDtypeStruct((b, n_blk, HALF_ROW), F32), jax.ShapeDtypeStruct((b, HALF_ROW, n_blk), F32)],
        compiler_params=_cparams(("parallel", "arbitrary")),
        name="compress",
    )(page_table, *([pool_t] * CMP_PAGES), *consts)


def _compress_weights(cmp_w1, cmp_w2, cmp_b2):
    w1 = cmp_w1.reshape(2, 2, CMP_STRIDE // 2, 2, HEAD_DIM, CMP_HIDDEN)
    w1 = jnp.transpose(w1, (0, 2, 3, 4, 1, 5))
    eye = jnp.eye(2, dtype=F32)
    w1t = jnp.einsum('xqpdje,hk->xqphdkje', w1, eye).reshape(2, CMP_STRIDE // 2, 256, 512).astype(BF16)
    w2bd = jnp.einsum('xed,hk->xhekd', cmp_w2, eye).reshape(2, 256, LANE).astype(BF16)
    b2t = jnp.concatenate([cmp_b2, cmp_b2], axis=1)
    r = np.arange(PAGE)
    perm = np.zeros((PAGE, PAGE), np.float32)
    perm[(r % CMP_STRIDE) * CHUNKS_PER_PAGE + r // CMP_STRIDE, r] = 1.0
    return (jnp.asarray(perm, dtype=BF16), w1t, w2bd[0], jnp.transpose(w2bd[1]), b2t[0:1], b2t[1].reshape(LANE, 1))


def _softmax0(s, mask):
    s = jnp.where(mask, s, NEG)
    m = jnp.max(s, axis=0, keepdims=True)
    p = jnp.where(mask, jnp.exp(s - m), 0.0)
    return p / jnp.maximum(jnp.sum(p, axis=0, keepdims=True), TINY)


def _split_dot(a, x):
    hi = x.astype(BF16)
    lo = (x - hi.astype(F32)).astype(BF16)
    return _dot(a, hi) + _dot(a, lo)


def _split_dot_r(x, a):
    hi = x.astype(BF16)
    lo = (x - hi.astype(F32)).astype(BF16)
    return _dot(hi, a) + _dot(lo, a)


def _top_blocks(imp, blk, cur):
    forced = (blk == 0) | ((blk <= cur) & (blk > cur - N_LOCAL))
    imp = jnp.where(forced, BIG, imp)
    imp = jnp.where(blk > cur, -BIG, imp)
    blk_f = blk.astype(F32)

    def pick(_, carry):
        imp, sel = carry
        mx = jnp.max(imp, axis=0, keepdims=True)
        first = jnp.min(jnp.where(imp == mx, blk_f, 1e9), axis=0, keepdims=True)
        hit = blk_f == first
        return jnp.where(hit, REMOVED, imp), jnp.where(hit, 1.0, sel)

    _, sel = lax.fori_loop(0, N_SELECT, pick, (imp, jnp.zeros_like(imp)))
    return sel


def _flash_update(state, s, mask, v_t):
    m, l, acc = state
    s = jnp.where(mask, s, NEG)
    m_new = jnp.maximum(m, jnp.max(s, axis=0, keepdims=True))
    p = jnp.where(mask, jnp.exp(s - m_new), 0.0)
    alpha = jnp.exp(m - m_new)
    l = alpha * l + jnp.sum(p, axis=0, keepdims=True)
    acc = alpha * acc + _dot(v_t, p.astype(BF16))
    return m_new, l, acc


def _repeat_rows(grp, rows, reps):
    return jnp.concatenate([jnp.broadcast_to(grp[r:r + 1, :], (reps, grp.shape[1])) for r in range(rows)], axis=0)


def _cmp_mask(n_rows, pos_q):
    r = lax.broadcasted_iota(jnp.int32, (n_rows, 1), 0)
    return (r >= 1) & (r * CMP_STRIDE + (CMP_LEN - CMP_STRIDE - 1) <= pos_q)


SEL_TILE = 512
WIN_KEYS = WINDOW + Q_BLOCK


def _nsa_prompt_kernel(qt_ref, kc_ref, vct_ref, ks_ref, vst_ref, kw_ref, vwt_ref, gt_ref, mt_ref, o_ref, sel_ref):
    i = pl.program_id(1)
    cols = NSA_GROUP * Q_BLOCK
    q_blk = qt_ref[...]
    q_t = jnp.concatenate([q_blk[g * HEAD_DIM:(g + 1) * HEAD_DIM, :] for g in range(NSA_GROUP)], axis=1)
    lane = lax.broadcasted_iota(jnp.int32, (1, Q_BLOCK), 1)
    pos_q = i * Q_BLOCK + lane
    tile4 = lambda a: jnp.concatenate([a] * NSA_GROUP, axis=1)

    nc = kc_ref.shape[1]
    s = _dot(kc_ref[0], q_t)
    p_c = _softmax0(s, tile4(_cmp_mask(nc, pos_q)))
    o_c = _dot(vct_ref[0], p_c.astype(BF16))

    pg = p_c[:, 0:Q_BLOCK]
    for g in range(1, NSA_GROUP):
        pg = pg + p_c[:, g * Q_BLOCK:(g + 1) * Q_BLOCK]
    imp = _split_dot(mt_ref[...], pg)
    blk = lax.broadcasted_iota(jnp.int32, (imp.shape[0], 1), 0)
    sel_ref[...] = _top_blocks(imp, blk, pos_q // SEL_BLOCK)

    blocks_per_tile = SEL_TILE // SEL_BLOCK

    def sel_body(j, state):
        s = _dot(ks_ref[0, pl.ds(pl.multiple_of(j * SEL_TILE, SEL_TILE), SEL_TILE), :], q_t)
        grp = sel_ref[pl.ds(pl.multiple_of(j * blocks_per_tile, blocks_per_tile), blocks_per_tile), :]
        key_pos = j * SEL_TILE + lax.broadcasted_iota(jnp.int32, (SEL_TILE, 1), 0)
        mask = (_repeat_rows(grp, blocks_per_tile, SEL_BLOCK) > 0.5) & (key_pos <= pos_q)
        return _flash_update(state, s, tile4(mask), vst_ref[0, j])

    init = (jnp.full((1, cols), NEG, F32), jnp.zeros((1, cols), F32), jnp.zeros((HEAD_DIM, cols), F32))
    n_tiles = (i * Q_BLOCK + Q_BLOCK + SEL_TILE - 1) // SEL_TILE
    _, l_s, acc_s = lax.fori_loop(0, n_tiles, sel_body, init)
    o_s = acc_s / jnp.maximum(l_s, TINY)

    kw = kw_ref[0, pl.ds(pl.multiple_of(i * Q_BLOCK, Q_BLOCK), WIN_KEYS), :]
    s = _dot(kw, q_t)
    w_pos = i * Q_BLOCK - WINDOW + lax.broadcasted_iota(jnp.int32, (WIN_KEYS, 1), 0)
    mask_w = tile4((w_pos <= pos_q) & (w_pos > pos_q - WINDOW) & (w_pos >= 0))
    p_w = _softmax0(s, mask_w).astype(BF16)
    o_w = jnp.zeros((HEAD_DIM, cols), F32)
    for w in range(WIN_KEYS // Q_BLOCK):
        o_w = o_w + _dot(vwt_ref[0, i + w], p_w[w * Q_BLOCK:(w + 1) * Q_BLOCK, :])

    gt = jax.nn.sigmoid(gt_ref[0])
    outs = []
    for g in range(NSA_GROUP):
        sl = slice(g * Q_BLOCK, (g + 1) * Q_BLOCK)
        outs.append(gt[3 * g:3 * g + 1, :] * o_c[:, sl] + gt[3 * g + 1:3 * g + 2, :] * o_s[:, sl]
                    + gt[3 * g + 2:3 * g + 3, :] * o_w[:, sl])
    for pair in range(NSA_GROUP // 2):
        both = jnp.concatenate([outs[2 * pair], outs[2 * pair + 1]], axis=0)
        o_ref[:, pair * LANE:(pair + 1) * LANE] = both.T


def nsa_prompt(qt, kc, vct, ks, vst, kw, vwt, gt, mt, t):
    nq = t // Q_BLOCK
    head = lambda a: pl.BlockSpec((1,) + a.shape[1:], lambda h, i: (h,) + (0,) * (a.ndim - 1))
    return pl.pallas_call(
        _nsa_prompt_kernel,
        grid=(NSA_KV_HEADS, nq),
        in_specs=[pl.BlockSpec((NSA_GROUP * HEAD_DIM, Q_BLOCK), lambda h, i: (h, i)),
                  head(kc), head(vct), head(ks), head(vst), head(kw), head(vwt),
                  pl.BlockSpec((1, 16, Q_BLOCK), lambda h, i: (h, 0, i)),
                  pl.BlockSpec(mt.shape, lambda h, i: (0, 0))],
        out_specs=pl.BlockSpec((Q_BLOCK, NSA_GROUP * HEAD_DIM), lambda h, i: (i, h)),
        out_shape=jax.ShapeDtypeStruct((t, NSA_WIDTH), F32),
        scratch_shapes=[pltpu.VMEM((mt.shape[0], Q_BLOCK), F32)],
        compiler_params=_cparams(("parallel", "arbitrary")),
        name="nsa_p",
    )(qt, kc, vct, ks, vst, kw, vwt, gt, mt)


def _cmp_to_sel_t(n_rows, n_blk, n_blk_pad):
    cs = (np.arange(n_rows)[None, :] - 1) * CMP_STRIDE
    js = np.arange(n_blk_pad)[:, None] * SEL_BLOCK
    m = (cs < js + SEL_BLOCK) & (cs + CMP_LEN > js) & (np.arange(n_rows)[None, :] >= 1) & (np.arange(n_blk_pad)[:, None] < n_blk)
    return jnp.asarray(m.astype(np.float32), dtype=BF16)


S_PAGES = 8
S_COLS = NSA_HEADS * 8


def _nsa_sample_kernel(n_steps, pt_ref, *refs):
    pages = refs[:S_PAGES]
    (qbd_ref, kc_ref, vct_ref, cw_ref, nw_ref, ns_ref, gt_ref, mt_ref, gsum_ref, o_ref,
     sel_ref, m_ref, l_ref, acc_ref, oc_ref, ow_ref) = refs[S_PAGES:]
    s_id = pl.program_id(1)
    past = n_steps * S_PAGES * PAGE
    qbd = qbd_ref[0]
    col = lax.broadcasted_iota(jnp.int32, (1, S_COLS), 1)
    pos_q = past + col % 8
    row = lax.broadcasted_iota(jnp.int32, (PAGE, 1), 0)

    def tile_update(state, tile, mask):
        s = _dot(tile[:HALF_ROW, :].T.astype(BF16), qbd)
        return _flash_update(state, s, mask, tile[HALF_ROW:, :].astype(BF16))

    def fresh():
        return (jnp.full((1, S_COLS), NEG, F32), jnp.zeros((1, S_COLS), F32), jnp.zeros((HALF_ROW, S_COLS), F32))

    @pl.when(s_id == 0)
    def _():
        nc = kc_ref.shape[1]
        s = _dot(kc_ref[0].astype(BF16), qbd)
        p_c = _softmax0(s, _cmp_mask(nc, pos_q))
        oc_ref[...] = _dot(vct_ref[0].astype(BF16), p_c.astype(BF16))
        imp = _split_dot(mt_ref[...], p_c)
        imp = _split_dot_r(imp, gsum_ref[...])
        blk = lax.broadcasted_iota(jnp.int32, (imp.shape[0], 1), 0)
        sel_ref[...] = _top_blocks(imp, blk, pos_q // SEL_BLOCK)

        st = fresh()
        wb = cw_ref.shape[2]
        for w in range(wb // PAGE):
            w_pos = past - wb + w * PAGE + row
            mask = (w_pos <= pos_q) & (w_pos > pos_q - WINDOW) & (w_pos >= 0)
            st = tile_update(st, cw_ref[0, :, w * PAGE:(w + 1) * PAGE], mask)
        w_pos = past + row
        mask = (w_pos <= pos_q) & (w_pos > pos_q - WINDOW)
        _, l_w, acc_w = tile_update(st, nw_ref[0], mask)
        ow_ref[...] = acc_w / jnp.maximum(l_w, TINY)

        nblk0 = past // SEL_BLOCK
        grp = sel_ref[nblk0:nblk0 + 8, :]
        mask = (_repeat_rows(grp, 2, SEL_BLOCK) > 0.5) & (past + row <= pos_q)
        m0, l0, a0 = tile_update(fresh(), ns_ref[0], mask)
        m_ref[...] = jnp.broadcast_to(m0, m_ref.shape)
        l_ref[...] = jnp.broadcast_to(l0, l_ref.shape)
        acc_ref[...] = a0

    blocks_per_step = S_PAGES * PAGE // SEL_BLOCK
    grp = sel_ref[pl.ds(pl.multiple_of(s_id * blocks_per_step, blocks_per_step), blocks_per_step), :]
    st = (m_ref[0:1, :], l_ref[0:1, :], acc_ref[...])
    for k in range(S_PAGES):
        key_pos = (s_id * S_PAGES + k) * PAGE + row
        mask = (_repeat_rows(grp[2 * k:2 * k + 2, :], 2, SEL_BLOCK) > 0.5) & (key_pos <= pos_q)
        st = tile_update(st, pages[k][0], mask)
    m_ref[...] = jnp.broadcast_to(st[0], m_ref.shape)
    l_ref[...] = jnp.broadcast_to(st[1], l_ref.shape)
    acc_ref[...] = st[2]

    @pl.when(s_id == n_steps - 1)
    def _():
        gt = jax.nn.sigmoid(gt_ref[0])
        o_s = st[2] / jnp.maximum(st[1], TINY)
        o_ref[0] = gt[0:1, :] * oc_ref[...] + gt[1:2, :] * o_s + gt[2:3, :] * ow_ref[...]


def nsa_sample(pool_t, page_table, qbd, kc, vct, cache_wt, new_w, new_s, gt, mt, gsum):
    b, n_pages = page_table.shape
    steps = n_pages // S_PAGES
    page_spec = lambda k: pl.BlockSpec((1, KV_ROW, PAGE), lambda bi, si, pt, k=k: (pt[bi, si * S_PAGES + k], 0, 0))
    per_b = lambda a: pl.BlockSpec((1,) + a.shape[1:], lambda bi, si, pt: (bi,) + (0,) * (a.ndim - 1))
    const = lambda a: pl.BlockSpec(a.shape, lambda bi, si, pt: (0,) * a.ndim)
    grid_spec = pltpu.PrefetchScalarGridSpec(
        num_scalar_prefetch=1,
        grid=(b, steps),
        in_specs=[page_spec(k) for k in range(S_PAGES)] + [
            per_b(qbd), per_b(kc), per_b(vct), per_b(cache_wt), per_b(new_w), per_b(new_s), per_b(gt), const(mt), const(gsum)],
        out_specs=pl.BlockSpec((1, HALF_ROW, S_COLS), lambda bi, si, pt: (bi, 0, 0)),
        scratch_shapes=[pltpu.VMEM((mt.shape[0], S_COLS), F32), pltpu.VMEM((8, S_COLS), F32), pltpu.VMEM((8, S_COLS), F32),
                        pltpu.VMEM((HALF_ROW, S_COLS), F32), pltpu.VMEM((HALF_ROW, S_COLS), F32), pltpu.VMEM((HALF_ROW, S_COLS), F32)],
    )
    return pl.pallas_call(
        functools.partial(_nsa_sample_kernel, steps),
        grid_spec=grid_spec,
        out_shape=jax.ShapeDtypeStruct((b, HALF_ROW, S_COLS), F32),
        compiler_params=_cparams(("parallel", "arbitrary")),
        name="nsa_s",
    )(page_table, *([pool_t] * S_PAGES), qbd, kc, vct, cache_wt, new_w, new_s, gt, mt, gsum)


def _gla_kernel(q_ref, k_ref, v_ref, a_ref, z_ref, wa_ref, ba_ref, nw_ref, s0_ref, o_ref, so_ref, s_ref, cum_ref):
    c = q_ref.shape[0]
    ci = pl.program_id(2)

    @pl.when(ci == 0)
    def _():
        s_ref[...] = s0_ref[0, 0]

    pre = _dot(a_ref[...].astype(BF16), wa_ref[...].astype(BF16)) + ba_ref[...]
    log_a = (jnp.minimum(pre, 0.0) - jnp.log1p(jnp.exp(-jnp.abs(pre)))) / GLA_TAU
    t_idx = lax.broadcasted_iota(jnp.int32, (c, 1), 0)
    cum = log_a
    sh = 1
    while sh < c:
        cum = cum + jnp.where(t_idx >= sh, pltpu.roll(cum, sh, 0), 0.0)
        sh *= 2
    cum_ref[...] = cum
    q = q_ref[...] * (GLA_DK ** -0.5)
    k = k_ref[...]
    v = v_ref[...]
    state = s_ref[...]
    o = _dot((q * jnp.exp(cum)).astype(BF16), state.astype(BF16))

    lane = lax.broadcasted_iota(jnp.int32, (1, LANE), 1)

    def col_body(s, att):
        k_row = k_ref[pl.ds(s, 1), :]
        c_row = cum_ref[pl.ds(s, 1), :]
        decay = jnp.exp(jnp.where(t_idx >= s, cum - c_row, NEG))
        column = jnp.sum(q * k_row * decay, axis=-1, keepdims=True)
        return jnp.where(lane == s, column, att)

    att = lax.fori_loop(0, c, col_body, jnp.zeros((c, LANE), F32))
    pad = jnp.zeros((LANE - c, GLA_DV), F32)
    v_pad = jnp.concatenate([v, pad], axis=0).astype(BF16)
    o = o + _dot(att.astype(BF16), v_pad)

    c_last = cum[c - 1:c, :]
    k_dec = jnp.concatenate([k * jnp.exp(c_last - cum), jnp.zeros((LANE - c, GLA_DK), F32)], axis=0)
    eye = lax.broadcasted_iota(jnp.int32, (GLA_DK, GLA_DK), 0) == lax.broadcasted_iota(jnp.int32, (GLA_DK, GLA_DK), 1)
    decay_col = jnp.sum(jnp.where(eye, jnp.exp(c_last), 0.0), axis=1, keepdims=True)
    new_state = decay_col * state + _dot(k_dec.T.astype(BF16), v_pad)
    s_ref[...] = new_state
    so_ref[0, 0] = new_state

    y = o * lax.rsqrt(jnp.mean(o * o, axis=-1, keepdims=True) + EPS) * nw_ref[...]
    o_ref[...] = y * _silu(z_ref[...])


def gla(proj, w_a2p, b_a, gla_norm_w, s0, n_seq, chunk):
    rows = proj.shape[0]
    n_chunk = rows // (n_seq * chunk)
    rowblk = lambda b, c: b * n_chunk + c
    return pl.pallas_call(
        _gla_kernel,
        grid=(n_seq, GLA_HEADS, n_chunk),
        in_specs=[pl.BlockSpec((chunk, GLA_DK), lambda b, h, c: (rowblk(b, c), C_QG // GLA_DK + h)),
                  pl.BlockSpec((chunk, GLA_DK), lambda b, h, c: (rowblk(b, c), C_KG // GLA_DK + h)),
                  pl.BlockSpec((chunk, GLA_DV), lambda b, h, c: (rowblk(b, c), C_VG // GLA_DV + h)),
                  pl.BlockSpec((chunk, LANE), lambda b, h, c: (rowblk(b, c), C_AG // LANE)),
                  pl.BlockSpec((chunk, GLA_DV), lambda b, h, c: (rowblk(b, c), C_ZG // GLA_DV + h)),
                  pl.BlockSpec((LANE, GLA_DK), lambda b, h, c: (0, h)),
                  pl.BlockSpec((1, GLA_DK), lambda b, h, c: (0, h)),
                  pl.BlockSpec((1, GLA_DV), lambda b, h, c: (0, 0)),
                  pl.BlockSpec((1, 1, GLA_DK, GLA_DV), lambda b, h, c: (b, h, 0, 0))],
        out_specs=[pl.BlockSpec((chunk, GLA_DV), lambda b, h, c: (rowblk(b, c), h)),
                   pl.BlockSpec((1, 1, GLA_DK, GLA_DV), lambda b, h, c: (b, h, 0, 0))],
        out_shape=[jax.ShapeDtypeStruct((rows, GLA_HEADS * GLA_DV), F32),
                   jax.ShapeDtypeStruct((n_seq, GLA_HEADS, GLA_DK, GLA_DV), F32)],
        scratch_shapes=[pltpu.VMEM((GLA_DK, GLA_DV), F32), pltpu.VMEM((chunk, GLA_DK), F32)],
        compiler_params=_cparams(("parallel", "parallel", "arbitrary")),
        name="gla",
    )(proj, proj, proj, proj, proj, w_a2p, b_a.reshape(1, -1), gla_norm_w.reshape(1, -1), s0)


def _out_kernel(on_ref, zn_ref, og_ref, mn_ref, mg_ref, x_ref, gate_ref, wn_ref, wg_ref, wo_ref, fw_ref, y_ref):
    o_nsa = (on_ref[...] * _silu(zn_ref[...])).astype(BF16)
    merged = (jax.nn.sigmoid(mn_ref[...]) * _dot(o_nsa, wn_ref[...])
              + jax.nn.sigmoid(mg_ref[...]) * _dot(og_ref[...].astype(BF16), wg_ref[...]))
    y = x_ref[...] + gate_ref[...] * _dot(merged.astype(BF16), wo_ref[...])
    y_ref[...] = y * lax.rsqrt(jnp.mean(y * y, axis=-1, keepdims=True) + EPS) * fw_ref[...]


def out_proj(o_nsa, o_gla, proj, x, gate, w_o_nsa, w_o_gla, w_out, final_norm_w, tm):
    rows = x.shape[0]
    per_row = gate.shape[0] != 1
    gate_spec = pl.BlockSpec((tm, D_MODEL), lambda i: (i, 0)) if per_row else pl.BlockSpec((1, D_MODEL), lambda i: (0, 0))
    resident = lambda a: pl.BlockSpec(a.shape, lambda i: (0, 0), pipeline_mode=pl.Buffered(1))
    return pl.pallas_call(
        _out_kernel,
        grid=(rows // tm,),
        in_specs=[pl.BlockSpec((tm, NSA_WIDTH), lambda i: (i, 0)),
                  pl.BlockSpec((tm, NSA_WIDTH), lambda i: (i, C_ZN // NSA_WIDTH)),
                  pl.BlockSpec((tm, NSA_WIDTH), lambda i: (i, 0)),
                  pl.BlockSpec((tm, D_MODEL), lambda i: (i, C_MN // D_MODEL)),
                  pl.BlockSpec((tm, D_MODEL), lambda i: (i, C_MG // D_MODEL)),
                  pl.BlockSpec((tm, D_MODEL), lambda i: (i, 0)),
                  gate_spec, resident(w_o_nsa), resident(w_o_gla), resident(w_out),
                  pl.BlockSpec((1, D_MODEL), lambda i: (0, 0))],
        out_specs=pl.BlockSpec((tm, D_MODEL), lambda i: (i, 0)),
        out_shape=jax.ShapeDtypeStruct((rows, D_MODEL), F32),
        compiler_params=_cparams(("parallel",)),
        name="outproj",
    )(o_nsa, proj, o_gla, proj, proj, x, gate, w_o_nsa, w_o_gla, w_out, final_norm_w.reshape(1, D_MODEL))


def _regroup_w_in(w_in):
    w_t = jnp.transpose(w_in)
    cuts = np.cumsum([0, 1024, 1536, 48, 1024, 512, 512, 1024, 16, 1024, 2048, 2048])
    q, kv, gn, zn, qg, kg, vg, ag, zg, mn, mg = [w_t[cuts[k]:cuts[k + 1]] for k in range(11)]
    pad = lambda n: jnp.zeros((n, w_t.shape[1]), w_t.dtype)
    w_fm = jnp.concatenate([q, kv, gn, pad(FM_ROWS - R_GN - 48)], axis=0).astype(BF16)
    w_rm = jnp.concatenate([mn, mg, zn, vg, zg, qg, kg, ag, pad(LANE - GLA_RANK)], axis=0).astype(BF16)
    return w_fm, w_rm


def _feature_major(a):
    lead = a.shape[:-4]
    n = len(lead)
    a = jnp.transpose(a, tuple(range(n)) + (n + 1, n + 2, n + 3, n))
    return a.reshape(lead + (KV_ROW, a.shape[-1]))


def _token_major(a_t, lead):
    rows = a_t.shape[-1]
    a = a_t.reshape(a_t.shape[:-2] + (2, NSA_KV_HEADS, HEAD_DIM, rows))
    n = a.ndim - 4
    a = jnp.transpose(a, tuple(range(n)) + (n + 3, n, n + 1, n + 2))
    return a.reshape(lead + (rows, 2, NSA_KV_HEADS, HEAD_DIM))


def kernel(x_prompt, x_sample, cache_kv_cmp, cache_kv_sel, cache_kv_win, state_gla, page_table, c_prompt, c_sample, norm_w, w_ada, b_ada, w_in, cmp_pos, cmp_w1, cmp_b1, cmp_w2, cmp_b2, w_a2, b_a, gla_norm_w, w_o_nsa, w_o_gla, w_out, final_norm_w):
    assert x_prompt.shape[0] == 1 and norm_w.shape[0] == 1, "one prompt sequence, one layer"
    t_p = x_prompt.shape[1]
    b_s, t_s = x_sample.shape[:2]
    past = page_table.shape[1] * PAGE
    wb = cache_kv_win.shape[2]
    assert t_s == 8 and wb == WINDOW and past % (S_PAGES * PAGE) == 0 and t_p % SEL_TILE == 0
    rows_s = b_s * t_s

    c_rows = jnp.zeros((40, D_MODEL), F32).at[0:1].set(c_prompt).at[1:1 + b_s].set(c_sample)
    mod = ada_mod(c_rows, w_ada[0], b_ada[0])
    shift, scale, gate = mod[:, :D_MODEL], mod[:, D_MODEL:2 * D_MODEL], mod[:, 2 * D_MODEL:]
    per_row = lambda a: jnp.repeat(a[1:1 + b_s], t_s, axis=0)

    w_fm, w_rm = _regroup_w_in(w_in[0])
    xp = x_prompt.reshape(t_p, D_MODEL)
    xs = x_sample.reshape(rows_s, D_MODEL)
    proj_p = in_proj(xp, scale[0:1], shift[0:1], norm_w[0], w_rm, 512, 640, False)
    projt_p = in_proj(xp, scale[0:1], shift[0:1], norm_w[0], w_fm, 512, 384, True)
    proj_s = in_proj(xs, per_row(scale), per_row(shift), norm_w[0], w_rm, rows_s, 640, False)
    projt_s = in_proj(xs, per_row(scale), per_row(shift), norm_w[0], w_fm, rows_s, 384, True)

    cos_p, sin_p = _rope_tables(jnp.arange(t_p, dtype=jnp.int32))
    cos_s, sin_s = _rope_tables(jnp.tile(past + jnp.arange(t_s, dtype=jnp.int32), b_s))
    qt_p, kvc_p, kvs_p, kvw_p, ks_p, kw_p, vst_p, vwt_p = rope_stage(projt_p, cos_p, sin_p, 512, True)
    qt_s, kvc_s, kvs_s, kvw_s = rope_stage(projt_s, cos_s, sin_s, rows_s, False)

    pb = pos_bias(cmp_pos[0], cmp_w1[0], cmp_b1[0])
    cmp_consts = _compress_weights(cmp_w1[0], cmp_w2[0], cmp_b2[0])
    perm, w1t, w2k, w2vt, b2k, b2vc = cmp_consts
    ident = jnp.arange(t_p // PAGE, dtype=jnp.int32)[None, :]
    pages_p = jnp.transpose(kvc_p.reshape(KV_ROW, t_p // PAGE, PAGE), (1, 0, 2))
    kc_p, vct_p = compress(pages_p, ident, perm, w1t, pb, w2k, w2vt, b2k, b2vc)
    pool_c = _feature_major(cache_kv_cmp[0])
    kc_s, vct_s = compress(pool_c, page_table, perm, w1t, pb, w2k, w2vt, b2k, b2vc)

    n_ent = kc_p.shape[1]
    kc_h = jnp.transpose(kc_p[0].reshape(n_ent, NSA_KV_HEADS, HEAD_DIM), (1, 0, 2)).astype(BF16)
    vct_h = vct_p[0].reshape(NSA_KV_HEADS, HEAD_DIM, n_ent).astype(BF16)
    kw_h = jnp.pad(kw_p, ((0, 0), (WINDOW, 0), (0, 0)))
    vwt_h = jnp.pad(vwt_p, ((0, 0), (WINDOW // Q_BLOCK, 0), (0, 0), (0, 0)))
    g_p = projt_p[R_GN:R_GN + 48].reshape(NSA_KV_HEADS, 12, t_p)
    g_p = jnp.pad(g_p, ((0, 0), (0, 4), (0, 0)))
    mt_p = _cmp_to_sel_t(n_ent, t_p // SEL_BLOCK, t_p // SEL_BLOCK)
    o_nsa_p = nsa_prompt(qt_p, kc_h, vct_h, ks_p, vst_p, kw_h, vwt_h, g_p, mt_p, t_p)

    q5 = qt_s.reshape(NSA_KV_HEADS, NSA_GROUP, HEAD_DIM, b_s, t_s)
    q_t = jnp.transpose(q5, (3, 0, 2, 1, 4)).reshape(b_s, NSA_KV_HEADS, HEAD_DIM, NSA_GROUP * t_s)
    eye = jnp.eye(NSA_KV_HEADS, dtype=BF16)
    qbd = jnp.einsum('bhdc,hk->bhdkc', q_t, eye).reshape(b_s, HALF_ROW, S_COLS)
    new_keys = lambda a_t: jnp.pad(jnp.transpose(a_t.reshape(KV_ROW, b_s, t_s), (1, 0, 2)), ((0, 0), (0, 0), (0, PAGE - t_s)))
    g_s = projt_s[R_GN:R_GN + 48].reshape(NSA_KV_HEADS, NSA_GROUP, 3, b_s, t_s)
    g_s = jnp.transpose(g_s, (3, 2, 0, 1, 4)).reshape(b_s, 3, S_COLS)
    g_s = jnp.pad(g_s, ((0, 0), (0, 5), (0, 0)))
    n_blk_s = -(-(past + t_s) // SEL_BLOCK)
    mt_s = _cmp_to_sel_t(kc_s.shape[1], n_blk_s, -(-n_blk_s // 8) * 8)
    col = np.arange(S_COLS)
    gsum = jnp.asarray(((col[:, None] // 32 == col[None, :] // 32) & (col[:, None] % 8 == col[None, :] % 8)).astype(np.float32), dtype=BF16)
    cache_wt = _feature_major(cache_kv_win[0])
    o_t = nsa_sample(_feature_major(cache_kv_sel[0]), page_table, qbd, kc_s, vct_s, cache_wt,
                     new_keys(kvw_s), new_keys(kvs_s), g_s, mt_s, gsum)
    o6 = o_t.reshape(b_s, NSA_KV_HEADS, HEAD_DIM, NSA_KV_HEADS, NSA_GROUP, t_s)
    o_nsa_s = jnp.stack([o6[:, h, :, h] for h in range(NSA_KV_HEADS)], axis=1)
    o_nsa_s = jnp.transpose(o_nsa_s, (0, 4, 1, 3, 2)).reshape(rows_s, NSA_WIDTH)

    w_a2p = jnp.zeros((LANE, GLA_HEADS * GLA_DK), F32).at[:GLA_RANK].set(w_a2[0])
    s0_p = jnp.zeros((1, GLA_HEADS, GLA_DK, GLA_DV), F32)
    o_gla_p, st_p = gla(proj_p, w_a2p, b_a[0], gla_norm_w[0], s0_p, 1, GLA_CHUNK)
    o_gla_s, st_s = gla(proj_s, w_a2p, b_a[0], gla_norm_w[0], state_gla[0], b_s, t_s)

    wn, wg, wo = w_o_nsa[0].astype(BF16), w_o_gla[0].astype(BF16), w_out[0].astype(BF16)
    y_p = out_proj(o_nsa_p, o_gla_p, proj_p, xp, gate[0:1], wn, wg, wo, final_norm_w, 256)
    y_s = out_proj(o_nsa_s, o_gla_s, proj_s, xs, per_row(gate), wn, wg, wo, final_norm_w, rows_s)

    sample_rows = lambda a_t: _token_major(jnp.transpose(a_t.reshape(KV_ROW, b_s, t_s), (1, 0, 2)), (1, b_s))
    win_t = jnp.concatenate([cache_wt, jnp.transpose(kvw_s.reshape(KV_ROW, b_s, t_s), (1, 0, 2))], axis=2)[:, :, t_s:]
    n_win = min(WINDOW, t_p)
    return (y_p.reshape(x_prompt.shape), y_s.reshape(x_sample.shape),
            _token_major(kvc_p, (1, 1)), sample_rows(kvc_s), _token_major(kvs_p, (1, 1)), sample_rows(kvs_s),
            _token_major(kvw_p[:, t_p - n_win:], (1, 1)), _token_major(win_t, (1, b_s)),
            st_p[None], st_s[None])
```

```python
import functools

import jax
import jax.numpy as jnp
import numpy as np
from jax import lax
from jax.experimental import pallas as pl
from jax.experimental.pallas import tpu as pltpu

F32 = jnp.float32
BF16 = jnp.bfloat16

D_MODEL = 2048
HEAD_DIM = 64
NSA_HEADS = 16
NSA_KV_HEADS = 4
NSA_GROUP = 4
NSA_WIDTH = 1024
HALF_ROW = NSA_KV_HEADS * HEAD_DIM
KV_ROW = 2 * HALF_ROW
CMP_LEN = 32
CMP_STRIDE = 16
CMP_HIDDEN = 128
SEL_BLOCK = 64
N_SELECT = 16
N_LOCAL = 2
WINDOW = 512
Q_BLOCK = 128
PAGE = 128
GLA_HEADS = 4
GLA_DK = 128
GLA_DV = 256
GLA_RANK = 16
GLA_TAU = 16.0
GLA_CHUNK = 64
ROPE_THETA = 10000.0
EPS = 1e-6
NEG = -1e30
BIG = 1e30
TINY = 1e-30
REMOVED = -3e38
LOG2E = 1.4426950408889634
SEL_TILE = 512
V_AUG = HEAD_DIM + 16

LANE = 128
VMEM_LIMIT = 48 * 1024 * 1024

C_MN, C_MG, C_ZN, C_VG, C_ZG, C_QG, C_KG, C_AG = 0, 2048, 4096, 5120, 6144, 7168, 7680, 8192
RM_COLS = 8448
R_Q, R_KV, R_GN = 0, 1024, 2560
FM_ROWS = 2688


def _cparams(sem):
    return pltpu.CompilerParams(dimension_semantics=sem, vmem_limit_bytes=VMEM_LIMIT)


def _dot(a, b):
    return jnp.dot(a, b, preferred_element_type=F32)


def _dot_nt(a, b):
    return lax.dot_general(a, b, (((1,), (1,)), ((), ())), preferred_element_type=F32)


def _silu(x):
    return x * jax.nn.sigmoid(x)


def _ada_kernel(c_ref, w_ref, b_ref, o_ref):
    o_ref[...] = _dot(c_ref[...].astype(BF16), w_ref[...].astype(BF16)) + b_ref[...]


def ada_mod(c_rows, w_ada, b_ada):
    rows, tn = c_rows.shape[0], 512
    n = w_ada.shape[1]
    return pl.pallas_call(
        _ada_kernel,
        grid=(n // tn,),
        in_specs=[pl.BlockSpec((rows, D_MODEL), lambda j: (0, 0)),
                  pl.BlockSpec((D_MODEL, tn), lambda j: (0, j)),
                  pl.BlockSpec((1, tn), lambda j: (0, j))],
        out_specs=pl.BlockSpec((rows, tn), lambda j: (0, j)),
        out_shape=jax.ShapeDtypeStruct((rows, n), F32),
        compiler_params=_cparams(("parallel",)),
        name="ada",
    )(c_rows, w_ada, b_ada.reshape(1, n))


def _modulated_norm(x_ref, sc_ref, sh_ref, nw_ref, h_ref):
    x = x_ref[...]
    y = x * lax.rsqrt(jnp.mean(x * x, axis=-1, keepdims=True) + EPS) * nw_ref[...]
    h_ref[...] = (y * (1.0 + sc_ref[...]) + sh_ref[...]).astype(BF16)


def _inproj_rm_kernel(x_ref, sc_ref, sh_ref, nw_ref, w_ref, o_ref, h_ref):
    @pl.when(pl.program_id(1) == 0)
    def _():
        _modulated_norm(x_ref, sc_ref, sh_ref, nw_ref, h_ref)

    o_ref[...] = _dot_nt(h_ref[...], w_ref[...])


def _inproj_fm_kernel(x_ref, sc_ref, sh_ref, nw_ref, w_ref, o_ref, h_ref):
    @pl.when(pl.program_id(1) == 0)
    def _():
        _modulated_norm(x_ref, sc_ref, sh_ref, nw_ref, h_ref)

    o_ref[...] = _dot_nt(w_ref[...], h_ref[...])


def in_proj(x, scale, shift, norm_w, w_t, tm, tn, feature_major):
    rows, n = x.shape[0], w_t.shape[0]
    per_row = scale.shape[0] != 1
    mod_spec = pl.BlockSpec((tm, D_MODEL), lambda i, j: (i, 0)) if per_row else pl.BlockSpec((1, D_MODEL), lambda i, j: (0, 0))
    if feature_major:
        body, out_spec, out_shape = _inproj_fm_kernel, pl.BlockSpec((tn, tm), lambda i, j: (j, i)), (n, rows)
    else:
        body, out_spec, out_shape = _inproj_rm_kernel, pl.BlockSpec((tm, tn), lambda i, j: (i, j)), (rows, n)
    return pl.pallas_call(
        body,
        grid=(rows // tm, n // tn),
        in_specs=[pl.BlockSpec((tm, D_MODEL), lambda i, j: (i, 0)),
                  mod_spec, mod_spec,
                  pl.BlockSpec((1, D_MODEL), lambda i, j: (0, 0)),
                  pl.BlockSpec((tn, D_MODEL), lambda i, j: (j, 0))],
        out_specs=out_spec,
        out_shape=jax.ShapeDtypeStruct(out_shape, F32),
        scratch_shapes=[pltpu.VMEM((tm, D_MODEL), BF16)],
        compiler_params=_cparams(("parallel", "arbitrary")),
        name="inproj_fm" if feature_major else "inproj_rm",
    )(x, scale, shift, norm_w.reshape(1, D_MODEL), w_t)


def _rope_kernel(q_ref, c_ref, s_ref, w_ref, cos_ref, sin_ref, qo_ref, co_ref, so_ref, wo_ref, *tile_refs):
    cos, sin = cos_ref[...], sin_ref[...]
    hh = HEAD_DIM // 2
    tr = cos.shape[1]

    def rot(src, head):
        x1 = src[head * HEAD_DIM:head * HEAD_DIM + hh, :]
        x2 = src[head * HEAD_DIM + hh:(head + 1) * HEAD_DIM, :]
        return x1 * cos - x2 * sin, x2 * cos + x1 * sin

    q_scale = HEAD_DIM ** -0.5 * LOG2E
    for head in range(NSA_HEADS):
        o1, o2 = rot(q_ref, head)
        qo_ref[head * HEAD_DIM:head * HEAD_DIM + hh, :] = (o1 * q_scale).astype(BF16)
        qo_ref[head * HEAD_DIM + hh:(head + 1) * HEAD_DIM, :] = (o2 * q_scale).astype(BF16)
    for src, dst in ((c_ref, co_ref), (s_ref, so_ref), (w_ref, wo_ref)):
        for head in range(NSA_KV_HEADS):
            o1, o2 = rot(src, head)
            dst[head * HEAD_DIM:head * HEAD_DIM + hh, :] = o1
            dst[head * HEAD_DIM + hh:(head + 1) * HEAD_DIM, :] = o2
        dst[HALF_ROW:, :] = src[HALF_ROW:, :]
    if tile_refs:
        ks_ref, kw_ref, vs_ref, vw_ref = tile_refs
        lane = lax.broadcasted_iota(jnp.int32, (1, LANE), 1)
        r = lax.broadcasted_iota(jnp.int32, (tr, 1), 0)
        onehot = jnp.where(lane - HEAD_DIM == (r // SEL_BLOCK) % (SEL_TILE // SEL_BLOCK), 1.0, 0.0)
        ones_row = jnp.where(lax.broadcasted_iota(jnp.int32, (V_AUG - HEAD_DIM, SEL_TILE), 0) == 0, 1.0, 0.0)
        for pair in range(NSA_KV_HEADS // 2):
            k_pair = so_ref[pair * LANE:(pair + 1) * LANE, :].T
            ks_ref[2 * pair] = jnp.where(lane < HEAD_DIM, k_pair, onehot).astype(BF16)
            ks_ref[2 * pair + 1] = jnp.where(lane < HEAD_DIM, pltpu.roll(k_pair, HEAD_DIM, 1), onehot).astype(BF16)
            kw_pair = wo_ref[pair * LANE:(pair + 1) * LANE, :].T.astype(BF16)
            kw_ref[2 * pair] = kw_pair[:, :HEAD_DIM]
            kw_ref[2 * pair + 1] = kw_pair[:, HEAD_DIM:]
        for head in range(NSA_KV_HEADS):
            rows = slice(HALF_ROW + head * HEAD_DIM, HALF_ROW + (head + 1) * HEAD_DIM)
            v = so_ref[rows, :]
            for w in range(tr // SEL_TILE):
                vs_ref[head, w] = jnp.concatenate([v[:, w * SEL_TILE:(w + 1) * SEL_TILE], ones_row], axis=0).astype(BF16)
            v = wo_ref[rows, :].astype(BF16)
            for w in range(tr // Q_BLOCK):
                vw_ref[head, w] = v[:, w * Q_BLOCK:(w + 1) * Q_BLOCK]


def rope_stage(proj_t, cos_t, sin_t, tr, with_tiles):
    tok = proj_t.shape[1]
    kv_spec = lambda k: pl.BlockSpec((KV_ROW, tr), lambda i, k=k: (R_KV // KV_ROW + k, i))
    out_kv = jax.ShapeDtypeStruct((KV_ROW, tok), F32)
    tab = pl.BlockSpec((HEAD_DIM // 2, tr), lambda i: (0, i))
    out_specs = [pl.BlockSpec((NSA_WIDTH, tr), lambda i: (0, i))] + [pl.BlockSpec((KV_ROW, tr), lambda i: (0, i))] * 3
    out_shape = [jax.ShapeDtypeStruct((NSA_WIDTH, tok), BF16), out_kv, out_kv, out_kv]
    if with_tiles:
        k_rows = lambda width: jax.ShapeDtypeStruct((NSA_KV_HEADS, tok, width), BF16)
        k_spec = lambda width: pl.BlockSpec((NSA_KV_HEADS, tr, width), lambda i: (0, i, 0))
        v_tiles = lambda rows, tile: jax.ShapeDtypeStruct((NSA_KV_HEADS, tok // tile, rows, tile), BF16)
        v_spec = lambda rows, tile: pl.BlockSpec((NSA_KV_HEADS, tr // tile, rows, tile), lambda i: (0, i, 0, 0))
        out_specs += [k_spec(LANE), k_spec(HEAD_DIM), v_spec(V_AUG, SEL_TILE), v_spec(HEAD_DIM, Q_BLOCK)]
        out_shape += [k_rows(LANE), k_rows(HEAD_DIM), v_tiles(V_AUG, SEL_TILE), v_tiles(HEAD_DIM, Q_BLOCK)]
    return pl.pallas_call(
        _rope_kernel,
        grid=(tok // tr,),
        in_specs=[pl.BlockSpec((NSA_WIDTH, tr), lambda i: (R_Q // NSA_WIDTH, i)), kv_spec(0), kv_spec(1), kv_spec(2), tab, tab],
        out_specs=out_specs,
        out_shape=out_shape,
        compiler_params=_cparams(("parallel",)),
        name="rope",
    )(proj_t, proj_t, proj_t, proj_t, cos_t, sin_t)


def _rope_tables(pos):
    half = HEAD_DIM // 2
    inv = ROPE_THETA ** (-jnp.arange(half, dtype=F32) / half)
    ang = inv[:, None] * pos.astype(F32)[None, :]
    return jnp.cos(ang), jnp.sin(ang)


def _posbias_kernel(p_ref, w_ref, b_ref, o_ref):
    for x in range(2):
        o_ref[x] = _dot(p_ref[x], w_ref[x]) + b_ref[x]


def pos_bias(cmp_pos, cmp_w1, cmp_b1):
    k = CMP_LEN * HEAD_DIM
    pos = jnp.zeros((2, 8, k), F32).at[:, 0].set(cmp_pos.reshape(2, k))
    out = pl.pallas_call(
        _posbias_kernel,
        out_shape=jax.ShapeDtypeStruct((2, 8, CMP_HIDDEN), F32),
        compiler_params=pltpu.CompilerParams(vmem_limit_bytes=VMEM_LIMIT),
        name="posbias",
    )(pos, cmp_w1.reshape(2, k, CMP_HIDDEN), cmp_b1.reshape(2, 1, CMP_HIDDEN))
    return out[:, 0]


CMP_PAGES = 16
CMP_CHUNKS = CMP_PAGES * PAGE // CMP_STRIDE
CHUNKS_PER_PAGE = PAGE // CMP_STRIDE


def _compress_kernel(pt_ref, *refs):
    pages = refs[:CMP_PAGES]
    perm_ref, w1_ref, pb_ref, w2_ref, w2t_ref, b2_ref, b2c_ref, k_ref, vt_ref, carry_ref = refs[CMP_PAGES:]
    s = pl.program_id(1)

    @pl.when(s == 0)
    def _():
        carry_ref[...] = jnp.zeros_like(carry_ref)

    n = CMP_CHUNKS
    perm = perm_ref[...]
    rows_by_p = [_dot_nt(perm, pg[0].astype(BF16)) for pg in pages]
    row0 = lax.broadcasted_iota(jnp.int32, (n, 1), 0) == 0
    for t in range(KV_ROW // LANE):
        x = t // 2
        sl = slice(t * LANE, (t + 1) * LANE)
        acc = jnp.zeros((n, 4 * CMP_HIDDEN), F32)
        for pp in range(CMP_STRIDE // 2):
            parts = []
            for p in (2 * pp, 2 * pp + 1):
                parts.append(jnp.concatenate(
                    [r[p * CHUNKS_PER_PAGE:(p + 1) * CHUNKS_PER_PAGE, sl] for r in rows_by_p], axis=0))
            lhs = jnp.concatenate(parts, axis=1).astype(BF16)
            acc = acc + _dot(lhs, w1_ref[x, pp])
        hid = []
        for hh in range(2):
            part0 = acc[:, hh * 256:hh * 256 + CMP_HIDDEN]
            part1 = acc[:, hh * 256 + CMP_HIDDEN:(hh + 1) * 256]
            csl = slice((t * 2 + hh) * CMP_HIDDEN, (t * 2 + hh + 1) * CMP_HIDDEN)
            prev = jnp.where(row0, carry_ref[0:1, csl], pltpu.roll(part0, 1, 0))
            carry_ref[0:1, csl] = part0[n - 1:n, :]
            hid.append(_silu(prev + part1 + pb_ref[x:x + 1, :]))
        hid = jnp.concatenate(hid, axis=1).astype(BF16)
        if x == 0:
            k_ref[0, :, sl] = _dot(hid, w2_ref[...]) + b2_ref[...]
        else:
            tv = t - 2
            vt_ref[0, tv * LANE:(tv + 1) * LANE, :] = _dot_nt(w2t_ref[...], hid) + b2c_ref[...]


def compress(pool_t, page_table, perm, w1t, pb, w2k, w2vt, b2k, b2vc):
    b, n_pages = page_table.shape
    steps = n_pages // CMP_PAGES
    n_blk = n_pages * CHUNKS_PER_PAGE
    page_spec = lambda k: pl.BlockSpec((1, KV_ROW, PAGE), lambda bi, si, pt, k=k: (pt[bi, si * CMP_PAGES + k], 0, 0))
    const = lambda a: pl.BlockSpec(a.shape, lambda bi, si, pt: (0,) * a.ndim)
    consts = (perm, w1t, pb, w2k, w2vt, b2k, b2vc)
    grid_spec = pltpu.PrefetchScalarGridSpec(
        num_scalar_prefetch=1,
        grid=(b, steps),
        in_specs=[page_spec(k) for k in range(CMP_PAGES)] + [const(a) for a in consts],
        out_specs=[pl.BlockSpec((1, CMP_CHUNKS, HALF_ROW), lambda bi, si, pt: (bi, si, 0)),
                   pl.BlockSpec((1, HALF_ROW, CMP_CHUNKS), lambda bi, si, pt: (bi, 0, si))],
        scratch_shapes=[pltpu.VMEM((8, 8 * CMP_HIDDEN), F32)],
    )
    return pl.pallas_call(
        _compress_kernel,
        grid_spec=grid_spec,
        out_shape=[jax.ShapeDtypeStruct((b, n_blk, HALF_ROW), F32), jax.ShapeDtypeStruct((b, HALF_ROW, n_blk), F32)],
        compiler_params=_cparams(("parallel", "arbitrary")),
        name="compress",
    )(page_table, *([pool_t] * CMP_PAGES), *consts)


def _compress_weights(cmp_w1, cmp_w2, cmp_b2):
    w1 = cmp_w1.reshape(2, 2, CMP_STRIDE // 2, 2, HEAD_DIM, CMP_HIDDEN)
    w1 = jnp.transpose(w1, (0, 2, 3, 4, 1, 5))
    eye = jnp.eye(2, dtype=F32)
    w1t = jnp.einsum('xqpdje,hk->xqphdkje', w1, eye).reshape(2, CMP_STRIDE // 2, 256, 512).astype(BF16)
    w2bd = jnp.einsum('xed,hk->xhekd', cmp_w2, eye).reshape(2, 256, LANE).astype(BF16)
    b2t = jnp.concatenate([cmp_b2, cmp_b2], axis=1)
    r = np.arange(PAGE)
    perm = np.zeros((PAGE, PAGE), np.float32)
    perm[(r % CMP_STRIDE) * CHUNKS_PER_PAGE + r // CMP_STRIDE, r] = 1.0
    return (jnp.asarray(perm, dtype=BF16), w1t, w2bd[0], jnp.transpose(w2bd[1]), b2t[0:1], b2t[1].reshape(LANE, 1))


def _softmax0(s, mask):
    s = jnp.where(mask, s, NEG)
    m = jnp.max(s, axis=0, keepdims=True)
    p = jnp.where(mask, jnp.exp2(s - m), 0.0)
    return p / jnp.maximum(jnp.sum(p, axis=0, keepdims=True), TINY)


def _split_dot(a, x):
    hi = x.astype(BF16)
    lo = (x - hi.astype(F32)).astype(BF16)
    return _dot(a, hi) + _dot(a, lo)


def _split_dot_r(x, a):
    hi = x.astype(BF16)
    lo = (x - hi.astype(F32)).astype(BF16)
    return _dot(hi, a) + _dot(lo, a)


def _top_blocks(imp, blk, cur):
    forced = (blk == 0) | ((blk <= cur) & (blk > cur - N_LOCAL))
    imp = jnp.where(forced, BIG, imp)
    imp = jnp.where(blk > cur, -BIG, imp)
    blk_f = blk.astype(F32)

    def pick(_, carry):
        imp, sel = carry
        mx = jnp.max(imp, axis=0, keepdims=True)
        first = jnp.min(jnp.where(imp == mx, blk_f, 1e9), axis=0, keepdims=True)
        hit = blk_f == first
        return jnp.where(hit, REMOVED, imp), jnp.where(hit, 1.0, sel)

    _, sel = lax.fori_loop(0, N_SELECT, pick, (imp, jnp.zeros_like(imp)))
    return sel


def _flash_update(state, s, mask, v_t):
    m, l, acc = state
    s = jnp.where(mask, s, NEG)
    m_new = jnp.maximum(m, jnp.max(s, axis=0, keepdims=True))
    p = jnp.where(mask, jnp.exp2(s - m_new), 0.0)
    alpha = jnp.exp2(m - m_new)
    l = alpha * l + jnp.sum(p, axis=0, keepdims=True)
    acc = alpha * acc + _dot(v_t, p.astype(BF16))
    return m_new, l, acc


def _flash_update_biased(state, s, v_aug):
    m, acc = state
    m_new = jnp.maximum(m, jnp.max(s, axis=0, keepdims=True))
    p = jnp.exp2(s - m_new)
    acc = jnp.exp2(m - m_new) * acc + _dot(v_aug, p.astype(BF16))
    return m_new, acc


def _repeat_rows(grp, rows, reps):
    return jnp.concatenate([jnp.broadcast_to(grp[r:r + 1, :], (reps, grp.shape[1])) for r in range(rows)], axis=0)


def _cmp_mask(n_rows, pos_q):
    r = lax.broadcasted_iota(jnp.int32, (n_rows, 1), 0)
    return (r >= 1) & (r * CMP_STRIDE + (CMP_LEN - CMP_STRIDE - 1) <= pos_q)


WIN_KEYS = WINDOW + Q_BLOCK
P_HEADS = 2
BLOCKS_PER_TILE = SEL_TILE // SEL_BLOCK


def _nsa_prompt_kernel(qt_ref, kc_ref, vct_ref, ks_ref, vst_ref, kw_ref, vwt_ref, gt_ref, mt_ref, o_ref, sel_ref):
    i = pl.program_id(1)
    cols = NSA_GROUP * Q_BLOCK
    lane = lax.broadcasted_iota(jnp.int32, (1, Q_BLOCK), 1)
    pos_q = i * Q_BLOCK + lane
    tile4 = lambda a: jnp.concatenate([a] * NSA_GROUP, axis=1)
    heads = range(P_HEADS)
    q_ts = []
    for h in heads:
        q_blk = qt_ref[h * NSA_GROUP * HEAD_DIM:(h + 1) * NSA_GROUP * HEAD_DIM, :]
        q_ts.append(jnp.concatenate([q_blk[g * HEAD_DIM:(g + 1) * HEAD_DIM, :] for g in range(NSA_GROUP)], axis=1))

    nc = kc_ref.shape[1]
    mask_c = tile4(_cmp_mask(nc, pos_q))
    o_c = []
    for h in heads:
        p_c = _softmax0(_dot(kc_ref[h], q_ts[h]), mask_c)
        o_c.append(_dot(vct_ref[h], p_c.astype(BF16)))
        pg = p_c[:, 0:Q_BLOCK]
        for g in range(1, NSA_GROUP):
            pg = pg + p_c[:, g * Q_BLOCK:(g + 1) * Q_BLOCK]
        imp = _split_dot(mt_ref[...], pg)
        blk = lax.broadcasted_iota(jnp.int32, (imp.shape[0], 1), 0)
        sel_ref[h] = _top_blocks(imp, blk, pos_q // SEL_BLOCK)

    zeros_q = jnp.zeros((LANE - HEAD_DIM - 16, cols), BF16)

    def q_aug(h, j):
        grp = sel_ref[h, pl.ds(pl.multiple_of(j * BLOCKS_PER_TILE, BLOCKS_PER_TILE), BLOCKS_PER_TILE), :]
        bias = jnp.concatenate([jnp.where(grp > 0.5, 0.0, NEG), jnp.zeros_like(grp)], axis=0)
        return jnp.concatenate([q_ts[h], tile4(bias).astype(BF16), zeros_q], axis=0)

    def scores(h, j):
        return _dot(ks_ref[h, pl.ds(pl.multiple_of(j * SEL_TILE, SEL_TILE), SEL_TILE), :], q_aug(h, j))

    def sel_body(j, states):
        return tuple(_flash_update_biased(states[h], scores(h, j), vst_ref[h, j]) for h in heads)

    init = tuple((jnp.full((1, cols), NEG, F32), jnp.zeros((V_AUG, cols), F32)) for _ in heads)
    j_diag = (i * Q_BLOCK) // SEL_TILE
    states = lax.fori_loop(0, j_diag, sel_body, init)
    key_pos = j_diag * SEL_TILE + lax.broadcasted_iota(jnp.int32, (SEL_TILE, 1), 0)
    causal = tile4(jnp.where(key_pos <= pos_q, 0.0, NEG))
    o_s = []
    for h in heads:
        _, acc = _flash_update_biased(states[h], scores(h, j_diag) + causal, vst_ref[h, j_diag])
        o_s.append(acc[:HEAD_DIM] / jnp.maximum(acc[HEAD_DIM:HEAD_DIM + 1], TINY))

    w_pos = i * Q_BLOCK - WINDOW + lax.broadcasted_iota(jnp.int32, (WIN_KEYS, 1), 0)
    mask_w = tile4((w_pos <= pos_q) & (w_pos > pos_q - WINDOW) & (w_pos >= 0))
    o_w = []
    for h in heads:
        kw = kw_ref[h, pl.ds(pl.multiple_of(i * Q_BLOCK, Q_BLOCK), WIN_KEYS), :]
        p_w = _softmax0(_dot(kw, q_ts[h]), mask_w).astype(BF16)
        acc = jnp.zeros((HEAD_DIM, cols), F32)
        for w in range(WIN_KEYS // Q_BLOCK):
            acc = acc + _dot(vwt_ref[h, i + w], p_w[w * Q_BLOCK:(w + 1) * Q_BLOCK, :])
        o_w.append(acc)

    for h in heads:
        gt = jax.nn.sigmoid(gt_ref[h])
        outs = []
        for g in range(NSA_GROUP):
            sl = slice(g * Q_BLOCK, (g + 1) * Q_BLOCK)
            outs.append(gt[3 * g:3 * g + 1, :] * o_c[h][:, sl] + gt[3 * g + 1:3 * g + 2, :] * o_s[h][:, sl]
                        + gt[3 * g + 2:3 * g + 3, :] * o_w[h][:, sl])
        for pair in range(NSA_GROUP // 2):
            both = jnp.concatenate([outs[2 * pair], outs[2 * pair + 1]], axis=0)
            lo = (h * NSA_GROUP // 2 + pair) * LANE
            o_ref[:, lo:lo + LANE] = both.T


def nsa_prompt(qt, kc, vct, ks, vst, kw, vwt, gt, mt, t):
    nq = t // Q_BLOCK
    head = lambda a: pl.BlockSpec((P_HEADS,) + a.shape[1:], lambda h, i: (h,) + (0,) * (a.ndim - 1),
                                  pipeline_mode=pl.Buffered(1))
    width = P_HEADS * NSA_GROUP * HEAD_DIM
    return pl.pallas_call(
        _nsa_prompt_kernel,
        grid=(NSA_KV_HEADS // P_HEADS, nq),
        in_specs=[pl.BlockSpec((width, Q_BLOCK), lambda h, i: (h, i)),
                  head(kc), head(vct), head(ks), head(vst), head(kw), head(vwt),
                  pl.BlockSpec((P_HEADS, 16, Q_BLOCK), lambda h, i: (h, 0, i)),
                  pl.BlockSpec(mt.shape, lambda h, i: (0, 0))],
        out_specs=pl.BlockSpec((Q_BLOCK, width), lambda h, i: (i, h)),
        out_shape=jax.ShapeDtypeStruct((t, NSA_WIDTH), F32),
        scratch_shapes=[pltpu.VMEM((P_HEADS, mt.shape[0], Q_BLOCK), F32)],
        compiler_params=_cparams(("parallel", "arbitrary")),
        name="nsa_p",
    )(qt, kc, vct, ks, vst, kw, vwt, gt, mt)


def _cmp_to_sel_t(n_rows, n_blk, n_blk_pad):
    cs = (np.arange(n_rows)[None, :] - 1) * CMP_STRIDE
    js = np.arange(n_blk_pad)[:, None] * SEL_BLOCK
    m = (cs < js + SEL_BLOCK) & (cs + CMP_LEN > js) & (np.arange(n_rows)[None, :] >= 1) & (np.arange(n_blk_pad)[:, None] < n_blk)
    return jnp.asarray(m.astype(np.float32), dtype=BF16)


S_PAGES = 8
S_COLS = NSA_HEADS * 8


def _nsa_sample_kernel(n_steps, pt_ref, *refs):
    pages = refs[:S_PAGES]
    (qbd_ref, kc_ref, vct_ref, cw_ref, nw_ref, ns_ref, gt_ref, mt_ref, gsum_ref, o_ref,
     sel_ref, m_ref, l_ref, acc_ref, oc_ref, ow_ref) = refs[S_PAGES:]
    s_id = pl.program_id(1)
    past = n_steps * S_PAGES * PAGE
    qbd = qbd_ref[0]
    col = lax.broadcasted_iota(jnp.int32, (1, S_COLS), 1)
    pos_q = past + col % 8
    row = lax.broadcasted_iota(jnp.int32, (PAGE, 1), 0)

    def tile_update(state, tile, mask):
        s = _dot(tile[:HALF_ROW, :].T.astype(BF16), qbd)
        return _flash_update(state, s, mask, tile[HALF_ROW:, :].astype(BF16))

    def fresh():
        return (jnp.full((1, S_COLS), NEG, F32), jnp.zeros((1, S_COLS), F32), jnp.zeros((HALF_ROW, S_COLS), F32))

    @pl.when(s_id == 0)
    def _():
        nc = kc_ref.shape[1]
        s = _dot(kc_ref[0].astype(BF16), qbd)
        p_c = _softmax0(s, _cmp_mask(nc, pos_q))
        oc_ref[...] = _dot(vct_ref[0].astype(BF16), p_c.astype(BF16))
        imp = _split_dot(mt_ref[...], p_c)
        imp = _split_dot_r(imp, gsum_ref[...])
        blk = lax.broadcasted_iota(jnp.int32, (imp.shape[0], 1), 0)
        sel_ref[...] = _top_blocks(imp, blk, pos_q // SEL_BLOCK)

        st = fresh()
        wb = cw_ref.shape[2]
        for w in range(wb // PAGE):
            w_pos = past - wb + w * PAGE + row
            mask = (w_pos <= pos_q) & (w_pos > pos_q - WINDOW) & (w_pos >= 0)
            st = tile_update(st, cw_ref[0, :, w * PAGE:(w + 1) * PAGE], mask)
        w_pos = past + row
        mask = (w_pos <= pos_q) & (w_pos > pos_q - WINDOW)
        _, l_w, acc_w = tile_update(st, nw_ref[0], mask)
        ow_ref[...] = acc_w / jnp.maximum(l_w, TINY)

        nblk0 = past // SEL_BLOCK
        grp = sel_ref[nblk0:nblk0 + 8, :]
        mask = (_repeat_rows(grp, 2, SEL_BLOCK) > 0.5) & (past + row <= pos_q)
        m0, l0, a0 = tile_update(fresh(), ns_ref[0], mask)
        m_ref[...] = jnp.broadcast_to(m0, m_ref.shape)
        l_ref[...] = jnp.broadcast_to(l0, l_ref.shape)
        acc_ref[...] = a0

    blocks_per_step = S_PAGES * PAGE // SEL_BLOCK
    grp = sel_ref[pl.ds(pl.multiple_of(s_id * blocks_per_step, blocks_per_step), blocks_per_step), :]
    st = (m_ref[0:1, :], l_ref[0:1, :], acc_ref[...])
    for k in range(S_PAGES):
        key_pos = (s_id * S_PAGES + k) * PAGE + row
        mask = (_repeat_rows(grp[2 * k:2 * k + 2, :], 2, SEL_BLOCK) > 0.5) & (key_pos <= pos_q)
        st = tile_update(st, pages[k][0], mask)
    m_ref[...] = jnp.broadcast_to(st[0], m_ref.shape)
    l_ref[...] = jnp.broadcast_to(st[1], l_ref.shape)
    acc_ref[...] = st[2]

    @pl.when(s_id == n_steps - 1)
    def _():
        gt = jax.nn.sigmoid(gt_ref[0])
        o_s = st[2] / jnp.maximum(st[1], TINY)
        o_ref[0] = gt[0:1, :] * oc_ref[...] + gt[1:2, :] * o_s + gt[2:3, :] * ow_ref[...]


def nsa_sample(pool_t, page_table, qbd, kc, vct, cache_wt, new_w, new_s, gt, mt, gsum):
    b, n_pages = page_table.shape
    steps = n_pages // S_PAGES
    page_spec = lambda k: pl.BlockSpec((1, KV_ROW, PAGE), lambda bi, si, pt, k=k: (pt[bi, si * S_PAGES + k], 0, 0))
    per_b = lambda a: pl.BlockSpec((1,) + a.shape[1:], lambda bi, si, pt: (bi,) + (0,) * (a.ndim - 1))
    const = lambda a: pl.BlockSpec(a.shape, lambda bi, si, pt: (0,) * a.ndim)
    grid_spec = pltpu.PrefetchScalarGridSpec(
        num_scalar_prefetch=1,
        grid=(b, steps),
        in_specs=[page_spec(k) for k in range(S_PAGES)] + [
            per_b(qbd), per_b(kc), per_b(vct), per_b(cache_wt), per_b(new_w), per_b(new_s), per_b(gt), const(mt), const(gsum)],
        out_specs=pl.BlockSpec((1, HALF_ROW, S_COLS), lambda bi, si, pt: (bi, 0, 0)),
        scratch_shapes=[pltpu.VMEM((mt.shape[0], S_COLS), F32), pltpu.VMEM((8, S_COLS), F32), pltpu.VMEM((8, S_COLS), F32),
                        pltpu.VMEM((HALF_ROW, S_COLS), F32), pltpu.VMEM((HALF_ROW, S_COLS), F32), pltpu.VMEM((HALF_ROW, S_COLS), F32)],
    )
    return pl.pallas_call(
        functools.partial(_nsa_sample_kernel, steps),
        grid_spec=grid_spec,
        out_shape=jax.ShapeDtypeStruct((b, HALF_ROW, S_COLS), F32),
        compiler_params=_cparams(("parallel", "arbitrary")),
        name="nsa_s",
    )(page_table, *([pool_t] * S_PAGES), qbd, kc, vct, cache_wt, new_w, new_s, gt, mt, gsum)


GLA_SUB = 16


def _gla_head(q, k, v, cum, state):
    c = q.shape[0]
    sub = min(GLA_SUB, c)
    lane = lax.broadcasted_iota(jnp.int32, (1, LANE), 1)
    t_sub = lax.broadcasted_iota(jnp.int32, (sub, 1), 0)
    row_pad = lambda a: jnp.concatenate([a, jnp.zeros((LANE - c, a.shape[1]), F32)], axis=0).astype(BF16)
    v_pad = row_pad(v)
    o = _dot((q * jnp.exp(cum)).astype(BF16), state.astype(BF16))
    blocks = []
    for r0 in range(0, c, sub):
        q_i, cum_i = q[r0:r0 + sub], cum[r0:r0 + sub]
        if r0 == 0:
            att_i = jnp.zeros((sub, LANE), F32)
        else:
            base = cum[r0 - 1:r0]
            q_dec = (q_i * jnp.exp(cum_i - base)).astype(BF16)
            k_dec = row_pad(k * jnp.exp(jnp.minimum(base - cum, 0.0)))
            att_i = jnp.where(lane < r0, _dot_nt(q_dec, k_dec), 0.0)
        for s in range(r0, r0 + sub):
            decay = jnp.exp(jnp.where(t_sub >= s - r0, cum_i - cum[s:s + 1], NEG))
            column = jnp.sum(q_i * k[s:s + 1] * decay, axis=-1, keepdims=True)
            att_i = jnp.where(lane == s, column, att_i)
        blocks.append(att_i)
    att = jnp.concatenate(blocks, axis=0)
    o = o + _dot(att.astype(BF16), v_pad)

    c_last = cum[c - 1:c]
    k_end = jnp.concatenate([k * jnp.exp(c_last - cum), jnp.zeros((LANE - c, GLA_DK), F32)], axis=0)
    eye = lax.broadcasted_iota(jnp.int32, (GLA_DK, GLA_DK), 0) == lax.broadcasted_iota(jnp.int32, (GLA_DK, GLA_DK), 1)
    decay_col = jnp.sum(jnp.where(eye, jnp.exp(c_last), 0.0), axis=1, keepdims=True)
    return o, decay_col * state + _dot(k_end.T.astype(BF16), v_pad)


def _gla_kernel(q_ref, k_ref, v_ref, a_ref, z_ref, wa_ref, ba_ref, nw_ref, s0_ref, o_ref, so_ref, s_ref):
    c = q_ref.shape[0]
    ci = pl.program_id(1)

    @pl.when(ci == 0)
    def _():
        s_ref[...] = s0_ref[0]

    pre = _dot(a_ref[...].astype(BF16), wa_ref[...].astype(BF16)) + ba_ref[...]
    log_a = (jnp.minimum(pre, 0.0) - jnp.log1p(jnp.exp(-jnp.abs(pre)))) / GLA_TAU
    t_idx = lax.broadcasted_iota(jnp.int32, (c, 1), 0)
    cum = log_a
    sh = 1
    while sh < c:
        cum = cum + jnp.where(t_idx >= sh, pltpu.roll(cum, sh, 0), 0.0)
        sh *= 2
    for h in range(GLA_HEADS):
        ks = slice(h * GLA_DK, (h + 1) * GLA_DK)
        vs = slice(h * GLA_DV, (h + 1) * GLA_DV)
        o, new_state = _gla_head(q_ref[:, ks] * (GLA_DK ** -0.5), k_ref[:, ks], v_ref[:, vs], cum[:, ks], s_ref[h])
        s_ref[h] = new_state
        y = o * lax.rsqrt(jnp.mean(o * o, axis=-1, keepdims=True) + EPS) * nw_ref[...]
        o_ref[:, vs] = y * _silu(z_ref[:, vs])

    @pl.when(ci == pl.num_programs(1) - 1)
    def _():
        so_ref[0] = s_ref[...]


def gla(proj, w_a2p, b_a, gla_norm_w, s0, n_seq, chunk):
    rows = proj.shape[0]
    n_chunk = rows // (n_seq * chunk)
    kw, vw = GLA_HEADS * GLA_DK, GLA_HEADS * GLA_DV
    rows_at = lambda width, col: pl.BlockSpec((chunk, width), lambda b, c: (b * n_chunk + c, col // width))
    state_spec = pl.BlockSpec((1, GLA_HEADS, GLA_DK, GLA_DV), lambda b, c: (b, 0, 0, 0))
    return pl.pallas_call(
        _gla_kernel,
        grid=(n_seq, n_chunk),
        in_specs=[rows_at(kw, C_QG), rows_at(kw, C_KG), rows_at(vw, C_VG), rows_at(LANE, C_AG), rows_at(vw, C_ZG),
                  pl.BlockSpec((LANE, kw), lambda b, c: (0, 0)),
                  pl.BlockSpec((1, kw), lambda b, c: (0, 0)),
                  pl.BlockSpec((1, GLA_DV), lambda b, c: (0, 0)),
                  state_spec],
        out_specs=[pl.BlockSpec((chunk, vw), lambda b, c: (b * n_chunk + c, 0)), state_spec],
        out_shape=[jax.ShapeDtypeStruct((rows, vw), F32),
                   jax.ShapeDtypeStruct((n_seq, GLA_HEADS, GLA_DK, GLA_DV), F32)],
        scratch_shapes=[pltpu.VMEM((GLA_HEADS, GLA_DK, GLA_DV), F32)],
        compiler_params=_cparams(("parallel", "arbitrary")),
        name="gla",
    )(proj, proj, proj, proj, proj, w_a2p, b_a.reshape(1, -1), gla_norm_w.reshape(1, -1), s0)


def _out_kernel(on_ref, zn_ref, og_ref, mn_ref, mg_ref, x_ref, gate_ref, wn_ref, wg_ref, wo_ref, fw_ref, y_ref):
    o_nsa = (on_ref[...] * _silu(zn_ref[...])).astype(BF16)
    merged = (jax.nn.sigmoid(mn_ref[...]) * _dot(o_nsa, wn_ref[...])
              + jax.nn.sigmoid(mg_ref[...]) * _dot(og_ref[...].astype(BF16), wg_ref[...]))
    y = x_ref[...] + gate_ref[...] * _dot(merged.astype(BF16), wo_ref[...])
    y_ref[...] = y * lax.rsqrt(jnp.mean(y * y, axis=-1, keepdims=True) + EPS) * fw_ref[...]


def out_proj(o_nsa, o_gla, proj, x, gate, w_o_nsa, w_o_gla, w_out, final_norm_w, tm):
    rows = x.shape[0]
    per_row = gate.shape[0] != 1
    gate_spec = pl.BlockSpec((tm, D_MODEL), lambda i: (i, 0)) if per_row else pl.BlockSpec((1, D_MODEL), lambda i: (0, 0))
    resident = lambda a: pl.BlockSpec(a.shape, lambda i: (0, 0), pipeline_mode=pl.Buffered(1))
    return pl.pallas_call(
        _out_kernel,
        grid=(rows // tm,),
        in_specs=[pl.BlockSpec((tm, NSA_WIDTH), lambda i: (i, 0)),
                  pl.BlockSpec((tm, NSA_WIDTH), lambda i: (i, C_ZN // NSA_WIDTH)),
                  pl.BlockSpec((tm, NSA_WIDTH), lambda i: (i, 0)),
                  pl.BlockSpec((tm, D_MODEL), lambda i: (i, C_MN // D_MODEL)),
                  pl.BlockSpec((tm, D_MODEL), lambda i: (i, C_MG // D_MODEL)),
                  pl.BlockSpec((tm, D_MODEL), lambda i: (i, 0)),
                  gate_spec, resident(w_o_nsa), resident(w_o_gla), resident(w_out),
                  pl.BlockSpec((1, D_MODEL), lambda i: (0, 0))],
        out_specs=pl.BlockSpec((tm, D_MODEL), lambda i: (i, 0)),
        out_shape=jax.ShapeDtypeStruct((rows, D_MODEL), F32),
        compiler_params=_cparams(("parallel",)),
        name="outproj",
    )(o_nsa, proj, o_gla, proj, proj, x, gate, w_o_nsa, w_o_gla, w_out, final_norm_w.reshape(1, D_MODEL))


def _regroup_w_in(w_in):
    w_t = jnp.transpose(w_in)
    cuts = np.cumsum([0, 1024, 1536, 48, 1024, 512, 512, 1024, 16, 1024, 2048, 2048])
    q, kv, gn, zn, qg, kg, vg, ag, zg, mn, mg = [w_t[cuts[k]:cuts[k + 1]] for k in range(11)]
    pad = lambda n: jnp.zeros((n, w_t.shape[1]), w_t.dtype)
    w_fm = jnp.concatenate([q, kv, gn, pad(FM_ROWS - R_GN - 48)], axis=0).astype(BF16)
    w_rm = jnp.concatenate([mn, mg, zn, vg, zg, qg, kg, ag, pad(RM_COLS - C_AG - GLA_RANK)], axis=0).astype(BF16)
    return w_fm, w_rm


def _feature_major(a):
    lead = a.shape[:-4]
    n = len(lead)
    a = jnp.transpose(a, tuple(range(n)) + (n + 1, n + 2, n + 3, n))
    return a.reshape(lead + (KV_ROW, a.shape[-1]))


def _token_major(a_t, lead):
    rows = a_t.shape[-1]
    a = a_t.reshape(a_t.shape[:-2] + (2, NSA_KV_HEADS, HEAD_DIM, rows))
    n = a.ndim - 4
    a = jnp.transpose(a, tuple(range(n)) + (n + 3, n, n + 1, n + 2))
    return a.reshape(lead + (rows, 2, NSA_KV_HEADS, HEAD_DIM))


def kernel(x_prompt, x_sample, cache_kv_cmp, cache_kv_sel, cache_kv_win, state_gla, page_table, c_prompt, c_sample, norm_w, w_ada, b_ada, w_in, cmp_pos, cmp_w1, cmp_b1, cmp_w2, cmp_b2, w_a2, b_a, gla_norm_w, w_o_nsa, w_o_gla, w_out, final_norm_w):
    assert x_prompt.shape[0] == 1 and norm_w.shape[0] == 1, "one prompt sequence, one layer"
    t_p = x_prompt.shape[1]
    b_s, t_s = x_sample.shape[:2]
    past = page_table.shape[1] * PAGE
    wb = cache_kv_win.shape[2]
    assert t_s == 8 and wb == WINDOW and past % (S_PAGES * PAGE) == 0 and t_p % SEL_TILE == 0
    rows_s = b_s * t_s

    c_rows = jnp.zeros((40, D_MODEL), F32).at[0:1].set(c_prompt).at[1:1 + b_s].set(c_sample)
    mod = ada_mod(c_rows, w_ada[0], b_ada[0])
    shift, scale, gate = mod[:, :D_MODEL], mod[:, D_MODEL:2 * D_MODEL], mod[:, 2 * D_MODEL:]
    per_row = lambda a: jnp.repeat(a[1:1 + b_s], t_s, axis=0)

    w_fm, w_rm = _regroup_w_in(w_in[0])
    xp = x_prompt.reshape(t_p, D_MODEL)
    xs = x_sample.reshape(rows_s, D_MODEL)
    proj_p = in_proj(xp, scale[0:1], shift[0:1], norm_w[0], w_rm, 1024, 768, False)
    projt_p = in_proj(xp, scale[0:1], shift[0:1], norm_w[0], w_fm, 512, 1344, True)
    proj_s = in_proj(xs, per_row(scale), per_row(shift), norm_w[0], w_rm, rows_s, 768, False)
    projt_s = in_proj(xs, per_row(scale), per_row(shift), norm_w[0], w_fm, rows_s, 1344, True)

    cos_p, sin_p = _rope_tables(jnp.arange(t_p, dtype=jnp.int32))
    cos_s, sin_s = _rope_tables(jnp.tile(past + jnp.arange(t_s, dtype=jnp.int32), b_s))
    qt_p, kvc_p, kvs_p, kvw_p, ks_p, kw_p, vst_p, vwt_p = rope_stage(projt_p, cos_p, sin_p, 512, True)
    qt_s, kvc_s, kvs_s, kvw_s = rope_stage(projt_s, cos_s, sin_s, rows_s, False)

    pb = pos_bias(cmp_pos[0], cmp_w1[0], cmp_b1[0])
    cmp_consts = _compress_weights(cmp_w1[0], cmp_w2[0], cmp_b2[0])
    perm, w1t, w2k, w2vt, b2k, b2vc = cmp_consts
    ident = jnp.arange(t_p // PAGE, dtype=jnp.int32)[None, :]
    pages_p = jnp.transpose(kvc_p.reshape(KV_ROW, t_p // PAGE, PAGE), (1, 0, 2))
    kc_p, vct_p = compress(pages_p, ident, perm, w1t, pb, w2k, w2vt, b2k, b2vc)
    pool_c = _feature_major(cache_kv_cmp[0])
    kc_s, vct_s = compress(pool_c, page_table, perm, w1t, pb, w2k, w2vt, b2k, b2vc)

    n_ent = kc_p.shape[1]
    kc_h = jnp.transpose(kc_p[0].reshape(n_ent, NSA_KV_HEADS, HEAD_DIM), (1, 0, 2)).astype(BF16)
    vct_h = vct_p[0].reshape(NSA_KV_HEADS, HEAD_DIM, n_ent).astype(BF16)
    kw_h = jnp.pad(kw_p, ((0, 0), (WINDOW, 0), (0, 0)))
    vwt_h = jnp.pad(vwt_p, ((0, 0), (WINDOW // Q_BLOCK, 0), (0, 0), (0, 0)))
    g_p = projt_p[R_GN:R_GN + 48].reshape(NSA_KV_HEADS, 12, t_p)
    g_p = jnp.pad(g_p, ((0, 0), (0, 4), (0, 0)))
    mt_p = _cmp_to_sel_t(n_ent, t_p // SEL_BLOCK, t_p // SEL_BLOCK)
    o_nsa_p = nsa_prompt(qt_p, kc_h, vct_h, ks_p, vst_p, kw_h, vwt_h, g_p, mt_p, t_p)

    q5 = qt_s.reshape(NSA_KV_HEADS, NSA_GROUP, HEAD_DIM, b_s, t_s)
    q_t = jnp.transpose(q5, (3, 0, 2, 1, 4)).reshape(b_s, NSA_KV_HEADS, HEAD_DIM, NSA_GROUP * t_s)
    eye = jnp.eye(NSA_KV_HEADS, dtype=BF16)
    qbd = jnp.einsum('bhdc,hk->bhdkc', q_t, eye).reshape(b_s, HALF_ROW, S_COLS)
    new_keys = lambda a_t: jnp.pad(jnp.transpose(a_t.reshape(KV_ROW, b_s, t_s), (1, 0, 2)), ((0, 0), (0, 0), (0, PAGE - t_s)))
    g_s = projt_s[R_GN:R_GN + 48].reshape(NSA_KV_HEADS, NSA_GROUP, 3, b_s, t_s)
    g_s = jnp.transpose(g_s, (3, 2, 0, 1, 4)).reshape(b_s, 3, S_COLS)
    g_s = jnp.pad(g_s, ((0, 0), (0, 5), (0, 0)))
    n_blk_s = -(-(past + t_s) // SEL_BLOCK)
    mt_s = _cmp_to_sel_t(kc_s.shape[1], n_blk_s, -(-n_blk_s // 8) * 8)
    col = np.arange(S_COLS)
    gsum = jnp.asarray(((col[:, None] // 32 == col[None, :] // 32) & (col[:, None] % 8 == col[None, :] % 8)).astype(np.float32), dtype=BF16)
    cache_wt = _feature_major(cache_kv_win[0])
    o_t = nsa_sample(_feature_major(cache_kv_sel[0]), page_table, qbd, kc_s, vct_s, cache_wt,
                     new_keys(kvw_s), new_keys(kvs_s), g_s, mt_s, gsum)
    o6 = o_t.reshape(b_s, NSA_KV_HEADS, HEAD_DIM, NSA_KV_HEADS, NSA_GROUP, t_s)
    o_nsa_s = jnp.stack([o6[:, h, :, h] for h in range(NSA_KV_HEADS)], axis=1)
    o_nsa_s = jnp.transpose(o_nsa_s, (0, 4, 1, 3, 2)).reshape(rows_s, NSA_WIDTH)

    w_a2p = jnp.zeros((LANE, GLA_HEADS * GLA_DK), F32).at[:GLA_RANK].set(w_a2[0])
    s0_p = jnp.zeros((1, GLA_HEADS, GLA_DK, GLA_DV), F32)
    o_gla_p, st_p = gla(proj_p, w_a2p, b_a[0], gla_norm_w[0], s0_p, 1, GLA_CHUNK)
    o_gla_s, st_s = gla(proj_s, w_a2p, b_a[0], gla_norm_w[0], state_gla[0], b_s, t_s)

    wn, wg, wo = w_o_nsa[0].astype(BF16), w_o_gla[0].astype(BF16), w_out[0].astype(BF16)
    y_p = out_proj(o_nsa_p, o_gla_p, proj_p, xp, gate[0:1], wn, wg, wo, final_norm_w, 256)
    y_s = out_proj(o_nsa_s, o_gla_s, proj_s, xs, per_row(gate), wn, wg, wo, final_norm_w, rows_s)

    sample_rows = lambda a_t: _token_major(jnp.transpose(a_t.reshape(KV_ROW, b_s, t_s), (1, 0, 2)), (1, b_s))
    win_t = jnp.concatenate([cache_wt, jnp.transpose(kvw_s.reshape(KV_ROW, b_s, t_s), (1, 0, 2))], axis=2)[:, :, t_s:]
    n_win = min(WINDOW, t_p)
    return (y_p.reshape(x_prompt.shape), y_s.reshape(x_sample.shape),
            _token_major(kvc_p, (1, 1)), sample_rows(kvc_s), _token_major(kvs_p, (1, 1)), sample_rows(kvs_s),
            _token_major(kvw_p[:, t_p - n_win:], (1, 1)), _token_major(win_t, (1, b_s)),
            st_p[None], st_s[None])
```

```python
import functools

import jax
import jax.numpy as jnp
import numpy as np
from jax import lax
from jax.experimental import pallas as pl
from jax.experimental.pallas import tpu as pltpu

F32 = jnp.float32
BF16 = jnp.bfloat16

D_MODEL = 2048
HEAD_DIM = 64
NSA_HEADS = 16
NSA_KV_HEADS = 4
NSA_GROUP = 4
NSA_WIDTH = 1024
HALF_ROW = NSA_KV_HEADS * HEAD_DIM
KV_ROW = 2 * HALF_ROW
CMP_LEN = 32
CMP_STRIDE = 16
CMP_HIDDEN = 128
SEL_BLOCK = 64
N_SELECT = 16
N_LOCAL = 2
WINDOW = 512
Q_BLOCK = 128
PAGE = 128
GLA_HEADS = 4
GLA_DK = 128
GLA_DV = 256
GLA_RANK = 16
GLA_TAU = 16.0
GLA_CHUNK = 64
ROPE_THETA = 10000.0
EPS = 1e-6
NEG = -1e30
BIG = 1e30
TINY = 1e-30
REMOVED = -3e38
LOG2E = 1.4426950408889634
SEL_TILE = 512
V_AUG = HEAD_DIM + 16

LANE = 128
VMEM_LIMIT = 48 * 1024 * 1024

(W_Q, W_KV, W_GN, W_ZN, W_QG, W_KG, W_VG, W_AG, W_ZG, W_MN, W_MG) = (
    0, 1024, 2560, 2608, 3632, 4144, 4656, 5680, 5696, 6720, 8768)
RM_TILE = 512
W_ALIGN = 16
C_MN, C_MG, C_ZN, C_VG, C_ZG, C_QG, C_KG, C_AG = 0, 2048, 4096, 5120, 6144, 7168, 7680, 8192
RM_SOURCES = ((W_MN, 4), (W_MG, 4), (W_ZN, 2), (W_VG, 2), (W_ZG, 2), (W_QG, 1), (W_KG, 1), (W_AG, 1))
RM_OFFSETS = tuple(start + RM_TILE * k for start, tiles in RM_SOURCES for k in range(tiles))
RM_COLS = RM_TILE * len(RM_OFFSETS)
R_Q, R_KV, R_GN = W_Q, W_KV, W_GN
FM_TILE = 1344
FM_ROWS = 2 * FM_TILE
FM_OFFSETS = (0, FM_TILE)


def _cparams(sem):
    return pltpu.CompilerParams(dimension_semantics=sem, vmem_limit_bytes=VMEM_LIMIT)


def _dot(a, b):
    return jnp.dot(a, b, preferred_element_type=F32)


def _dot_nt(a, b):
    return lax.dot_general(a, b, (((1,), (1,)), ((), ())), preferred_element_type=F32)


def _silu(x):
    return x * jax.nn.sigmoid(x)


def _ada_kernel(c_ref, w_ref, b_ref, o_ref):
    o_ref[...] = _dot(c_ref[...].astype(BF16), w_ref[...].astype(BF16)) + b_ref[...]


def ada_mod(c_rows, w_ada, b_ada):
    rows, tn = c_rows.shape[0], 512
    n = w_ada.shape[1]
    return pl.pallas_call(
        _ada_kernel,
        grid=(n // tn,),
        in_specs=[pl.BlockSpec((rows, D_MODEL), lambda j: (0, 0)),
                  pl.BlockSpec((D_MODEL, tn), lambda j: (0, j)),
                  pl.BlockSpec((1, tn), lambda j: (0, j))],
        out_specs=pl.BlockSpec((rows, tn), lambda j: (0, j)),
        out_shape=jax.ShapeDtypeStruct((rows, n), F32),
        compiler_params=_cparams(("parallel",)),
        name="ada",
    )(c_rows, w_ada, b_ada.reshape(1, n))


def _modulated_norm(x_ref, sc_ref, sh_ref, nw_ref, h_ref):
    x = x_ref[...]
    y = x * lax.rsqrt(jnp.mean(x * x, axis=-1, keepdims=True) + EPS) * nw_ref[...]
    h_ref[...] = (y * (1.0 + sc_ref[...]) + sh_ref[...]).astype(BF16)


def _inproj_rm_kernel(off_ref, x_ref, sc_ref, sh_ref, nw_ref, w_ref, o_ref, h_ref):
    @pl.when(pl.program_id(1) == 0)
    def _():
        _modulated_norm(x_ref, sc_ref, sh_ref, nw_ref, h_ref)

    o_ref[...] = _dot_nt(h_ref[...], w_ref[...].astype(BF16))


def _inproj_fm_kernel(off_ref, x_ref, sc_ref, sh_ref, nw_ref, w_ref, o_ref, h_ref):
    @pl.when(pl.program_id(1) == 0)
    def _():
        _modulated_norm(x_ref, sc_ref, sh_ref, nw_ref, h_ref)

    o_ref[...] = _dot_nt(w_ref[...].astype(BF16), h_ref[...])


def in_proj(x, scale, shift, norm_w, w_t, row_offsets, tm, tn, feature_major):
    rows, n = x.shape[0], len(row_offsets) * tn
    per_row = scale.shape[0] != 1
    mod_spec = (pl.BlockSpec((tm, D_MODEL), lambda i, j, off: (i, 0)) if per_row
                else pl.BlockSpec((1, D_MODEL), lambda i, j, off: (0, 0)))
    if feature_major:
        body, out_spec, out_shape = _inproj_fm_kernel, pl.BlockSpec((tn, tm), lambda i, j, off: (j, i)), (n, rows)
    else:
        body, out_spec, out_shape = _inproj_rm_kernel, pl.BlockSpec((tm, tn), lambda i, j, off: (i, j)), (rows, n)
    grid_spec = pltpu.PrefetchScalarGridSpec(
        num_scalar_prefetch=1,
        grid=(rows // tm, len(row_offsets)),
        in_specs=[pl.BlockSpec((tm, D_MODEL), lambda i, j, off: (i, 0)),
                  mod_spec, mod_spec,
                  pl.BlockSpec((1, D_MODEL), lambda i, j, off: (0, 0)),
                  pl.BlockSpec((pl.Element(tn), pl.Element(D_MODEL)), lambda i, j, off: (off[j] * W_ALIGN, 0))],
        out_specs=out_spec,
        scratch_shapes=[pltpu.VMEM((tm, D_MODEL), BF16)],
    )
    return pl.pallas_call(
        body,
        grid_spec=grid_spec,
        out_shape=jax.ShapeDtypeStruct(out_shape, F32),
        compiler_params=_cparams(("parallel", "arbitrary")),
        name="inproj_fm" if feature_major else "inproj_rm",
    )(jnp.asarray([o // W_ALIGN for o in row_offsets], jnp.int32), x, scale, shift, norm_w.reshape(1, D_MODEL), w_t)


def _rope_kernel(q_ref, c_ref, s_ref, w_ref, cos_ref, sin_ref, qo_ref, co_ref, so_ref, wo_ref, *tile_refs):
    cos, sin = cos_ref[...], sin_ref[...]
    hh = HEAD_DIM // 2
    tr = cos.shape[1]

    def rot(src, head):
        x1 = src[head * HEAD_DIM:head * HEAD_DIM + hh, :]
        x2 = src[head * HEAD_DIM + hh:(head + 1) * HEAD_DIM, :]
        return x1 * cos - x2 * sin, x2 * cos + x1 * sin

    q_scale = HEAD_DIM ** -0.5 * LOG2E
    for head in range(NSA_HEADS):
        o1, o2 = rot(q_ref, head)
        qo_ref[head * HEAD_DIM:head * HEAD_DIM + hh, :] = (o1 * q_scale).astype(BF16)
        qo_ref[head * HEAD_DIM + hh:(head + 1) * HEAD_DIM, :] = (o2 * q_scale).astype(BF16)
    for src, dst in ((c_ref, co_ref), (s_ref, so_ref), (w_ref, wo_ref)):
        for head in range(NSA_KV_HEADS):
            o1, o2 = rot(src, head)
            dst[head * HEAD_DIM:head * HEAD_DIM + hh, :] = o1
            dst[head * HEAD_DIM + hh:(head + 1) * HEAD_DIM, :] = o2
        dst[HALF_ROW:, :] = src[HALF_ROW:, :]
    if tile_refs:
        ks_ref, kw_ref, vs_ref, vw_ref = tile_refs
        lane = lax.broadcasted_iota(jnp.int32, (1, LANE), 1)
        r = lax.broadcasted_iota(jnp.int32, (tr, 1), 0)
        onehot = jnp.where(lane - HEAD_DIM == (r // SEL_BLOCK) % (SEL_TILE // SEL_BLOCK), 1.0, 0.0)
        ones_row = jnp.where(lax.broadcasted_iota(jnp.int32, (V_AUG - HEAD_DIM, SEL_TILE), 0) == 0, 1.0, 0.0)
        for pair in range(NSA_KV_HEADS // 2):
            k_pair = so_ref[pair * LANE:(pair + 1) * LANE, :].T
            ks_ref[2 * pair] = jnp.where(lane < HEAD_DIM, k_pair, onehot).astype(BF16)
            ks_ref[2 * pair + 1] = jnp.where(lane < HEAD_DIM, pltpu.roll(k_pair, HEAD_DIM, 1), onehot).astype(BF16)
            kw_pair = wo_ref[pair * LANE:(pair + 1) * LANE, :].T.astype(BF16)
            kw_ref[2 * pair] = kw_pair[:, :HEAD_DIM]
            kw_ref[2 * pair + 1] = kw_pair[:, HEAD_DIM:]
        for head in range(NSA_KV_HEADS):
            rows = slice(HALF_ROW + head * HEAD_DIM, HALF_ROW + (head + 1) * HEAD_DIM)
            v = so_ref[rows, :]
            for w in range(tr // SEL_TILE):
                vs_ref[head, w] = jnp.concatenate([v[:, w * SEL_TILE:(w + 1) * SEL_TILE], ones_row], axis=0).astype(BF16)
            v = wo_ref[rows, :].astype(BF16)
            for w in range(tr // Q_BLOCK):
                vw_ref[head, w] = v[:, w * Q_BLOCK:(w + 1) * Q_BLOCK]


def rope_stage(proj_t, cos_t, sin_t, tr, with_tiles):
    tok = proj_t.shape[1]
    kv_spec = lambda k: pl.BlockSpec((KV_ROW, tr), lambda i, k=k: (R_KV // KV_ROW + k, i))
    out_kv = jax.ShapeDtypeStruct((KV_ROW, tok), F32)
    tab = pl.BlockSpec((HEAD_DIM // 2, tr), lambda i: (0, i))
    out_specs = [pl.BlockSpec((NSA_WIDTH, tr), lambda i: (0, i))] + [pl.BlockSpec((KV_ROW, tr), lambda i: (0, i))] * 3
    out_shape = [jax.ShapeDtypeStruct((NSA_WIDTH, tok), BF16), out_kv, out_kv, out_kv]
    if with_tiles:
        k_rows = lambda width: jax.ShapeDtypeStruct((NSA_KV_HEADS, tok, width), BF16)
        k_spec = lambda width: pl.BlockSpec((NSA_KV_HEADS, tr, width), lambda i: (0, i, 0))
        v_tiles = lambda rows, tile: jax.ShapeDtypeStruct((NSA_KV_HEADS, tok // tile, rows, tile), BF16)
        v_spec = lambda rows, tile: pl.BlockSpec((NSA_KV_HEADS, tr // tile, rows, tile), lambda i: (0, i, 0, 0))
        out_specs += [k_spec(LANE), k_spec(HEAD_DIM), v_spec(V_AUG, SEL_TILE), v_spec(HEAD_DIM, Q_BLOCK)]
        out_shape += [k_rows(LANE), k_rows(HEAD_DIM), v_tiles(V_AUG, SEL_TILE), v_tiles(HEAD_DIM, Q_BLOCK)]
    return pl.pallas_call(
        _rope_kernel,
        grid=(tok // tr,),
        in_specs=[pl.BlockSpec((NSA_WIDTH, tr), lambda i: (R_Q // NSA_WIDTH, i)), kv_spec(0), kv_spec(1), kv_spec(2), tab, tab],
        out_specs=out_specs,
        out_shape=out_shape,
        compiler_params=_cparams(("parallel",)),
        name="rope",
    )(proj_t, proj_t, proj_t, proj_t, cos_t, sin_t)


def _rope_tables(pos):
    half = HEAD_DIM // 2
    inv = ROPE_THETA ** (-jnp.arange(half, dtype=F32) / half)
    ang = inv[:, None] * pos.astype(F32)[None, :]
    return jnp.cos(ang), jnp.sin(ang)


def _posbias_kernel(p_ref, w_ref, b_ref, o_ref):
    for x in range(2):
        o_ref[x] = _dot(p_ref[x], w_ref[x]) + b_ref[x]


def pos_bias(cmp_pos, cmp_w1, cmp_b1):
    k = CMP_LEN * HEAD_DIM
    pos = jnp.zeros((2, 8, k), F32).at[:, 0].set(cmp_pos.reshape(2, k))
    out = pl.pallas_call(
        _posbias_kernel,
        out_shape=jax.ShapeDtypeStruct((2, 8, CMP_HIDDEN), F32),
        compiler_params=pltpu.CompilerParams(vmem_limit_bytes=VMEM_LIMIT),
        name="posbias",
    )(pos, cmp_w1.reshape(2, k, CMP_HIDDEN), cmp_b1.reshape(2, 1, CMP_HIDDEN))
    return out[:, 0]


CMP_PAGES = 16
CMP_CHUNKS = CMP_PAGES * PAGE // CMP_STRIDE
CHUNKS_PER_PAGE = PAGE // CMP_STRIDE


def _compress_kernel(pt_ref, *refs):
    pages = refs[:CMP_PAGES]
    perm_ref, w1_ref, pb_ref, w2_ref, w2t_ref, b2_ref, b2c_ref, k_ref, vt_ref, carry_ref = refs[CMP_PAGES:]
    s = pl.program_id(1)

    @pl.when(s == 0)
    def _():
        carry_ref[...] = jnp.zeros_like(carry_ref)

    n = CMP_CHUNKS
    perm = perm_ref[...]
    rows_by_p = [_dot_nt(perm, pg[0].astype(BF16)) for pg in pages]
    row0 = lax.broadcasted_iota(jnp.int32, (n, 1), 0) == 0
    for t in range(KV_ROW // LANE):
        x = t // 2
        sl = slice(t * LANE, (t + 1) * LANE)
        acc = jnp.zeros((n, 4 * CMP_HIDDEN), F32)
        for pp in range(CMP_STRIDE // 2):
            parts = []
            for p in (2 * pp, 2 * pp + 1):
                parts.append(jnp.concatenate(
                    [r[p * CHUNKS_PER_PAGE:(p + 1) * CHUNKS_PER_PAGE, sl] for r in rows_by_p], axis=0))
            lhs = jnp.concatenate(parts, axis=1).astype(BF16)
            acc = acc + _dot(lhs, w1_ref[x, pp])
        hid = []
        for hh in range(2):
            part0 = acc[:, hh * 256:hh * 256 + CMP_HIDDEN]
            part1 = acc[:, hh * 256 + CMP_HIDDEN:(hh + 1) * 256]
            csl = slice((t * 2 + hh) * CMP_HIDDEN, (t * 2 + hh + 1) * CMP_HIDDEN)
            prev = jnp.where(row0, carry_ref[0:1, csl], pltpu.roll(part0, 1, 0))
            carry_ref[0:1, csl] = part0[n - 1:n, :]
            hid.append(_silu(prev + part1 + pb_ref[x:x + 1, :]))
        hid = jnp.concatenate(hid, axis=1).astype(BF16)
        if x == 0:
            k_ref[0, :, sl] = _dot(hid, w2_ref[...]) + b2_ref[...]
        else:
            tv = t - 2
            vt_ref[0, tv * LANE:(tv + 1) * LANE, :] = _dot_nt(w2t_ref[...], hid) + b2c_ref[...]


def compress(pool_t, page_table, perm, w1t, pb, w2k, w2vt, b2k, b2vc):
    b, n_pages = page_table.shape
    steps = n_pages // CMP_PAGES
    n_blk = n_pages * CHUNKS_PER_PAGE
    page_spec = lambda k: pl.BlockSpec((1, KV_ROW, PAGE), lambda bi, si, pt, k=k: (pt[bi, si * CMP_PAGES + k], 0, 0))
    const = lambda a: pl.BlockSpec(a.shape, lambda bi, si, pt: (0,) * a.ndim)
    consts = (perm, w1t, pb, w2k, w2vt, b2k, b2vc)
    grid_spec = pltpu.PrefetchScalarGridSpec(
        num_scalar_prefetch=1,
        grid=(b, steps),
        in_specs=[page_spec(k) for k in range(CMP_PAGES)] + [const(a) for a in consts],
        out_specs=[pl.BlockSpec((1, CMP_CHUNKS, HALF_ROW), lambda bi, si, pt: (bi, si, 0)),
                   pl.BlockSpec((1, HALF_ROW, CMP_CHUNKS), lambda bi, si, pt: (bi, 0, si))],
        scratch_shapes=[pltpu.VMEM((8, 8 * CMP_HIDDEN), F32)],
    )
    return pl.pallas_call(
        _compress_kernel,
        grid_spec=grid_spec,
        out_shape=[jax.ShapeDtypeStruct((b, n_blk, HALF_ROW), F32), jax.ShapeDtypeStruct((b, HALF_ROW, n_blk), F32)],
        compiler_params=_cparams(("parallel", "arbitrary")),
        name="compress",
    )(page_table, *([pool_t] * CMP_PAGES), *consts)


def _compress_weights(cmp_w1, cmp_w2, cmp_b2):
    w1 = cmp_w1.reshape(2, 2, CMP_STRIDE // 2, 2, HEAD_DIM, CMP_HIDDEN)
    w1 = jnp.transpose(w1, (0, 2, 3, 4, 1, 5))
    eye = jnp.eye(2, dtype=F32)
    w1t = jnp.einsum('xqpdje,hk->xqphdkje', w1, eye).reshape(2, CMP_STRIDE // 2, 256, 512).astype(BF16)
    w2bd = jnp.einsum('xed,hk->xhekd', cmp_w2, eye).reshape(2, 256, LANE).astype(BF16)
    b2t = jnp.concatenate([cmp_b2, cmp_b2], axis=1)
    r = np.arange(PAGE)
    perm = np.zeros((PAGE, PAGE), np.float32)
    perm[(r % CMP_STRIDE) * CHUNKS_PER_PAGE + r // CMP_STRIDE, r] = 1.0
    return (jnp.asarray(perm, dtype=BF16), w1t, w2bd[0], jnp.transpose(w2bd[1]), b2t[0:1], b2t[1].reshape(LANE, 1))


def _softmax0(s, mask):
    s = jnp.where(mask, s, NEG)
    m = jnp.max(s, axis=0, keepdims=True)
    p = jnp.where(mask, jnp.exp2(s - m), 0.0)
    return p / jnp.maximum(jnp.sum(p, axis=0, keepdims=True), TINY)


def _split_dot(a, x):
    hi = x.astype(BF16)
    lo = (x - hi.astype(F32)).astype(BF16)
    return _dot(a, hi) + _dot(a, lo)


def _split_dot_r(x, a):
    hi = x.astype(BF16)
    lo = (x - hi.astype(F32)).astype(BF16)
    return _dot(hi, a) + _dot(lo, a)


def _top_blocks(imps, cur):
    blk = lax.broadcasted_iota(jnp.int32, (imps[0].shape[0], 1), 0)
    forced = (blk == 0) | ((blk <= cur) & (blk > cur - N_LOCAL))
    imps = tuple(jnp.where(blk > cur, -BIG, jnp.where(forced, BIG, imp)) for imp in imps)
    blk_f = blk.astype(F32)

    def pick(_, carry):
        out = []
        for imp, sel in carry:
            mx = jnp.max(imp, axis=0, keepdims=True)
            first = jnp.min(jnp.where(imp == mx, blk_f, 1e9), axis=0, keepdims=True)
            hit = blk_f == first
            out.append((jnp.where(hit, REMOVED, imp), jnp.where(hit, 1.0, sel)))
        return tuple(out)

    final = lax.fori_loop(0, N_SELECT, pick, tuple((imp, jnp.zeros_like(imp)) for imp in imps))
    return [sel for _, sel in final]


def _flash_update(state, s, mask, v_t):
    m, l, acc = state
    s = jnp.where(mask, s, NEG)
    m_new = jnp.maximum(m, jnp.max(s, axis=0, keepdims=True))
    p = jnp.where(mask, jnp.exp2(s - m_new), 0.0)
    alpha = jnp.exp2(m - m_new)
    l = alpha * l + jnp.sum(p, axis=0, keepdims=True)
    acc = alpha * acc + _dot(v_t, p.astype(BF16))
    return m_new, l, acc


def _flash_update_biased(state, s, v_aug):
    m, acc = state
    m_new = jnp.maximum(m, jnp.max(s, axis=0, keepdims=True))
    p = jnp.exp2(s - m_new)
    acc = jnp.exp2(m - m_new) * acc + _dot(v_aug, p.astype(BF16))
    return m_new, acc


def _repeat_rows(grp, rows, reps):
    return jnp.concatenate([jnp.broadcast_to(grp[r:r + 1, :], (reps, grp.shape[1])) for r in range(rows)], axis=0)


def _cmp_mask(n_rows, pos_q):
    r = lax.broadcasted_iota(jnp.int32, (n_rows, 1), 0)
    return (r >= 1) & (r * CMP_STRIDE + (CMP_LEN - CMP_STRIDE - 1) <= pos_q)


WIN_KEYS = WINDOW + Q_BLOCK
P_HEADS = 2
BLOCKS_PER_TILE = SEL_TILE // SEL_BLOCK


def _nsa_prompt_kernel(qt_ref, kc_ref, vct_ref, ks_ref, vst_ref, kw_ref, vwt_ref, gt_ref, mt_ref, o_ref, sel_ref, s_ref):
    i = pl.program_id(1)
    cols = NSA_GROUP * Q_BLOCK
    lane = lax.broadcasted_iota(jnp.int32, (1, Q_BLOCK), 1)
    pos_q = i * Q_BLOCK + lane
    tile4 = lambda a: jnp.concatenate([a] * NSA_GROUP, axis=1)
    heads = range(P_HEADS)
    q_ts = []
    for h in heads:
        q_blk = qt_ref[h * NSA_GROUP * HEAD_DIM:(h + 1) * NSA_GROUP * HEAD_DIM, :]
        q_ts.append(jnp.concatenate([q_blk[g * HEAD_DIM:(g + 1) * HEAD_DIM, :] for g in range(NSA_GROUP)], axis=1))

    nc = kc_ref.shape[1]
    s_cmp = [_dot(kc_ref[h], q_ts[h]) for h in heads]
    s_win = [_dot(kw_ref[h, pl.ds(pl.multiple_of(i * Q_BLOCK, Q_BLOCK), WIN_KEYS), :], q_ts[h]) for h in heads]

    mask_c = tile4(_cmp_mask(nc, pos_q))
    o_c, imps = [], []
    for h in heads:
        p_c = _softmax0(s_cmp[h], mask_c)
        o_c.append(_dot(vct_ref[h], p_c.astype(BF16)))
        pg = p_c[:, 0:Q_BLOCK]
        for g in range(1, NSA_GROUP):
            pg = pg + p_c[:, g * Q_BLOCK:(g + 1) * Q_BLOCK]
        imps.append(_split_dot(mt_ref[...], pg))
    for h, sel in enumerate(_top_blocks(imps, pos_q // SEL_BLOCK)):
        sel_ref[h] = sel

    zeros_q = jnp.zeros((LANE - HEAD_DIM - 16, cols), BF16)

    def q_aug(h, j):
        grp = sel_ref[h, pl.ds(pl.multiple_of(j * BLOCKS_PER_TILE, BLOCKS_PER_TILE), BLOCKS_PER_TILE), :]
        bias = jnp.concatenate([jnp.where(grp > 0.5, 0.0, NEG), jnp.zeros_like(grp)], axis=0)
        return jnp.concatenate([q_ts[h], tile4(bias).astype(BF16), zeros_q], axis=0)

    def scores(h, j):
        return _dot(ks_ref[h, pl.ds(pl.multiple_of(j * SEL_TILE, SEL_TILE), SEL_TILE), :], q_aug(h, j))

    def sel_body(j, states):
        st0, st1 = states
        s1 = scores(1, j)
        st0 = _flash_update_biased(st0, s_ref[...], vst_ref[0, j])
        s_ref[...] = scores(0, j + 1)
        st1 = _flash_update_biased(st1, s1, vst_ref[1, j])
        return st0, st1

    init = tuple((jnp.full((1, cols), NEG, F32), jnp.zeros((V_AUG, cols), F32)) for _ in heads)
    j_diag = (i * Q_BLOCK) // SEL_TILE
    s_ref[...] = scores(0, 0)
    states = lax.fori_loop(0, j_diag, sel_body, init)
    key_pos = j_diag * SEL_TILE + lax.broadcasted_iota(jnp.int32, (SEL_TILE, 1), 0)
    causal = tile4(jnp.where(key_pos <= pos_q, 0.0, NEG))
    last = (s_ref[...], scores(1, j_diag))
    o_s = []
    for h in heads:
        _, acc = _flash_update_biased(states[h], last[h] + causal, vst_ref[h, j_diag])
        o_s.append(acc[:HEAD_DIM] / jnp.maximum(acc[HEAD_DIM:HEAD_DIM + 1], TINY))

    w_pos = i * Q_BLOCK - WINDOW + lax.broadcasted_iota(jnp.int32, (WIN_KEYS, 1), 0)
    mask_w = tile4((w_pos <= pos_q) & (w_pos > pos_q - WINDOW) & (w_pos >= 0))
    o_w = []
    for h in heads:
        p_w = _softmax0(s_win[h], mask_w).astype(BF16)
        acc = jnp.zeros((HEAD_DIM, cols), F32)
        for w in range(WIN_KEYS // Q_BLOCK):
            acc = acc + _dot(vwt_ref[h, i + w], p_w[w * Q_BLOCK:(w + 1) * Q_BLOCK, :])
        o_w.append(acc)

    for h in heads:
        gt = jax.nn.sigmoid(gt_ref[h])
        outs = []
        for g in range(NSA_GROUP):
            sl = slice(g * Q_BLOCK, (g + 1) * Q_BLOCK)
            outs.append(gt[3 * g:3 * g + 1, :] * o_c[h][:, sl] + gt[3 * g + 1:3 * g + 2, :] * o_s[h][:, sl]
                        + gt[3 * g + 2:3 * g + 3, :] * o_w[h][:, sl])
        for pair in range(NSA_GROUP // 2):
            both = jnp.concatenate([outs[2 * pair], outs[2 * pair + 1]], axis=0)
            lo = (h * NSA_GROUP // 2 + pair) * LANE
            o_ref[:, lo:lo + LANE] = both.T


def nsa_prompt(qt, kc, vct, ks, vst, kw, vwt, gt, mt, t):
    nq = t // Q_BLOCK
    head = lambda a: pl.BlockSpec((P_HEADS,) + a.shape[1:], lambda h, i: (h,) + (0,) * (a.ndim - 1),
                                  pipeline_mode=pl.Buffered(1))
    width = P_HEADS * NSA_GROUP * HEAD_DIM
    return pl.pallas_call(
        _nsa_prompt_kernel,
        grid=(NSA_KV_HEADS // P_HEADS, nq),
        in_specs=[pl.BlockSpec((width, Q_BLOCK), lambda h, i: (h, i)),
                  head(kc), head(vct), head(ks), head(vst), head(kw), head(vwt),
                  pl.BlockSpec((P_HEADS, 16, Q_BLOCK), lambda h, i: (h, 0, i)),
                  pl.BlockSpec(mt.shape, lambda h, i: (0, 0))],
        out_specs=pl.BlockSpec((Q_BLOCK, width), lambda h, i: (i, h)),
        out_shape=jax.ShapeDtypeStruct((t, NSA_WIDTH), F32),
        scratch_shapes=[pltpu.VMEM((P_HEADS, mt.shape[0], Q_BLOCK), F32),
                        pltpu.VMEM((SEL_TILE, NSA_GROUP * Q_BLOCK), F32)],
        compiler_params=_cparams(("parallel", "arbitrary")),
        name="nsa_p",
    )(qt, kc, vct, ks, vst, kw, vwt, gt, mt)


def _cmp_to_sel_t(n_rows, n_blk, n_blk_pad):
    cs = (np.arange(n_rows)[None, :] - 1) * CMP_STRIDE
    js = np.arange(n_blk_pad)[:, None] * SEL_BLOCK
    m = (cs < js + SEL_BLOCK) & (cs + CMP_LEN > js) & (np.arange(n_rows)[None, :] >= 1) & (np.arange(n_blk_pad)[:, None] < n_blk)
    return jnp.asarray(m.astype(np.float32), dtype=BF16)


S_PAGES = 8
S_COLS = NSA_HEADS * 8


def _nsa_sample_kernel(n_steps, pt_ref, *refs):
    pages = refs[:S_PAGES]
    (qbd_ref, kc_ref, vct_ref, cw_ref, nw_ref, ns_ref, gt_ref, mt_ref, gsum_ref, o_ref,
     sel_ref, m_ref, l_ref, acc_ref, oc_ref, ow_ref) = refs[S_PAGES:]
    s_id = pl.program_id(1)
    past = n_steps * S_PAGES * PAGE
    qbd = qbd_ref[0]
    col = lax.broadcasted_iota(jnp.int32, (1, S_COLS), 1)
    pos_q = past + col % 8
    row = lax.broadcasted_iota(jnp.int32, (PAGE, 1), 0)

    def tile_update(state, tile, mask):
        s = _dot(tile[:HALF_ROW, :].T.astype(BF16), qbd)
        return _flash_update(state, s, mask, tile[HALF_ROW:, :].astype(BF16))

    def fresh():
        return (jnp.full((1, S_COLS), NEG, F32), jnp.zeros((1, S_COLS), F32), jnp.zeros((HALF_ROW, S_COLS), F32))

    @pl.when(s_id == 0)
    def _():
        nc = kc_ref.shape[1]
        s = _dot(kc_ref[0].astype(BF16), qbd)
        p_c = _softmax0(s, _cmp_mask(nc, pos_q))
        oc_ref[...] = _dot(vct_ref[0].astype(BF16), p_c.astype(BF16))
        imp = _split_dot(mt_ref[...], p_c)
        imp = _split_dot_r(imp, gsum_ref[...])
        sel_ref[...] = _top_blocks([imp], pos_q // SEL_BLOCK)[0]

        st = fresh()
        wb = cw_ref.shape[2]
        for w in range(wb // PAGE):
            w_pos = past - wb + w * PAGE + row
            mask = (w_pos <= pos_q) & (w_pos > pos_q - WINDOW) & (w_pos >= 0)
            st = tile_update(st, cw_ref[0, :, w * PAGE:(w + 1) * PAGE], mask)
        w_pos = past + row
        mask = (w_pos <= pos_q) & (w_pos > pos_q - WINDOW)
        _, l_w, acc_w = tile_update(st, nw_ref[0], mask)
        ow_ref[...] = acc_w / jnp.maximum(l_w, TINY)

        nblk0 = past // SEL_BLOCK
        grp = sel_ref[nblk0:nblk0 + 8, :]
        mask = (_repeat_rows(grp, 2, SEL_BLOCK) > 0.5) & (past + row <= pos_q)
        m0, l0, a0 = tile_update(fresh(), ns_ref[0], mask)
        m_ref[...] = jnp.broadcast_to(m0, m_ref.shape)
        l_ref[...] = jnp.broadcast_to(l0, l_ref.shape)
        acc_ref[...] = a0

    blocks_per_step = S_PAGES * PAGE // SEL_BLOCK
    grp = sel_ref[pl.ds(pl.multiple_of(s_id * blocks_per_step, blocks_per_step), blocks_per_step), :]
    st = (m_ref[0:1, :], l_ref[0:1, :], acc_ref[...])
    for k in range(S_PAGES):
        key_pos = (s_id * S_PAGES + k) * PAGE + row
        mask = (_repeat_rows(grp[2 * k:2 * k + 2, :], 2, SEL_BLOCK) > 0.5) & (key_pos <= pos_q)
        st = tile_update(st, pages[k][0], mask)
    m_ref[...] = jnp.broadcast_to(st[0], m_ref.shape)
    l_ref[...] = jnp.broadcast_to(st[1], l_ref.shape)
    acc_ref[...] = st[2]

    @pl.when(s_id == n_steps - 1)
    def _():
        gt = jax.nn.sigmoid(gt_ref[0])
        o_s = st[2] / jnp.maximum(st[1], TINY)
        o_ref[0] = gt[0:1, :] * oc_ref[...] + gt[1:2, :] * o_s + gt[2:3, :] * ow_ref[...]


def nsa_sample(pool_t, page_table, qbd, kc, vct, cache_wt, new_w, new_s, gt, mt, gsum):
    b, n_pages = page_table.shape
    steps = n_pages // S_PAGES
    page_spec = lambda k: pl.BlockSpec((1, KV_ROW, PAGE), lambda bi, si, pt, k=k: (pt[bi, si * S_PAGES + k], 0, 0))
    per_b = lambda a: pl.BlockSpec((1,) + a.shape[1:], lambda bi, si, pt: (bi,) + (0,) * (a.ndim - 1))
    const = lambda a: pl.BlockSpec(a.shape, lambda bi, si, pt: (0,) * a.ndim)
    grid_spec = pltpu.PrefetchScalarGridSpec(
        num_scalar_prefetch=1,
        grid=(b, steps),
        in_specs=[page_spec(k) for k in range(S_PAGES)] + [
            per_b(qbd), per_b(kc), per_b(vct), per_b(cache_wt), per_b(new_w), per_b(new_s), per_b(gt), const(mt), const(gsum)],
        out_specs=pl.BlockSpec((1, HALF_ROW, S_COLS), lambda bi, si, pt: (bi, 0, 0)),
        scratch_shapes=[pltpu.VMEM((mt.shape[0], S_COLS), F32), pltpu.VMEM((8, S_COLS), F32), pltpu.VMEM((8, S_COLS), F32),
                        pltpu.VMEM((HALF_ROW, S_COLS), F32), pltpu.VMEM((HALF_ROW, S_COLS), F32), pltpu.VMEM((HALF_ROW, S_COLS), F32)],
    )
    return pl.pallas_call(
        functools.partial(_nsa_sample_kernel, steps),
        grid_spec=grid_spec,
        out_shape=jax.ShapeDtypeStruct((b, HALF_ROW, S_COLS), F32),
        compiler_params=_cparams(("parallel", "arbitrary")),
        name="nsa_s",
    )(page_table, *([pool_t] * S_PAGES), qbd, kc, vct, cache_wt, new_w, new_s, gt, mt, gsum)


GLA_SUB = 16


def _gla_head(q, k, v, cum, state):
    c = q.shape[0]
    sub = min(GLA_SUB, c)
    lane = lax.broadcasted_iota(jnp.int32, (1, LANE), 1)
    t_sub = lax.broadcasted_iota(jnp.int32, (sub, 1), 0)
    row_pad = lambda a: jnp.concatenate([a, jnp.zeros((LANE - c, a.shape[1]), F32)], axis=0).astype(BF16)
    v_pad = row_pad(v)
    o = _dot((q * jnp.exp(cum)).astype(BF16), state.astype(BF16))
    blocks = []
    for r0 in range(0, c, sub):
        q_i, cum_i = q[r0:r0 + sub], cum[r0:r0 + sub]
        if r0 == 0:
            att_i = jnp.zeros((sub, LANE), F32)
        else:
            base = cum[r0 - 1:r0]
            q_dec = (q_i * jnp.exp(cum_i - base)).astype(BF16)
            k_dec = row_pad(k * jnp.exp(jnp.minimum(base - cum, 0.0)))
            att_i = jnp.where(lane < r0, _dot_nt(q_dec, k_dec), 0.0)
        for s in range(r0, r0 + sub):
            decay = jnp.exp(jnp.where(t_sub >= s - r0, cum_i - cum[s:s + 1], NEG))
            column = jnp.sum(q_i * k[s:s + 1] * decay, axis=-1, keepdims=True)
            att_i = jnp.where(lane == s, column, att_i)
        blocks.append(att_i)
    att = jnp.concatenate(blocks, axis=0)
    o = o + _dot(att.astype(BF16), v_pad)

    c_last = cum[c - 1:c]
    k_end = jnp.concatenate([k * jnp.exp(c_last - cum), jnp.zeros((LANE - c, GLA_DK), F32)], axis=0)
    eye = lax.broadcasted_iota(jnp.int32, (GLA_DK, GLA_DK), 0) == lax.broadcasted_iota(jnp.int32, (GLA_DK, GLA_DK), 1)
    decay_col = jnp.sum(jnp.where(eye, jnp.exp(c_last), 0.0), axis=1, keepdims=True)
    return o, decay_col * state + _dot(k_end.T.astype(BF16), v_pad)


def _gla_kernel(q_ref, k_ref, v_ref, a_ref, z_ref, wa_ref, ba_ref, nw_ref, s0_ref, o_ref, so_ref, s_ref):
    c = q_ref.shape[0]
    ci = pl.program_id(1)

    @pl.when(ci == 0)
    def _():
        s_ref[...] = s0_ref[0]

    pre = _dot(a_ref[...].astype(BF16), wa_ref[...].astype(BF16)) + ba_ref[...]
    log_a = (jnp.minimum(pre, 0.0) - jnp.log1p(jnp.exp(-jnp.abs(pre)))) / GLA_TAU
    t_idx = lax.broadcasted_iota(jnp.int32, (c, 1), 0)
    cum = log_a
    sh = 1
    while sh < c:
        cum = cum + jnp.where(t_idx >= sh, pltpu.roll(cum, sh, 0), 0.0)
        sh *= 2
    for h in range(GLA_HEADS):
        ks = slice(h * GLA_DK, (h + 1) * GLA_DK)
        vs = slice(h * GLA_DV, (h + 1) * GLA_DV)
        o, new_state = _gla_head(q_ref[:, ks] * (GLA_DK ** -0.5), k_ref[:, ks], v_ref[:, vs], cum[:, ks], s_ref[h])
        s_ref[h] = new_state
        y = o * lax.rsqrt(jnp.mean(o * o, axis=-1, keepdims=True) + EPS) * nw_ref[...]
        o_ref[:, vs] = y * _silu(z_ref[:, vs])

    @pl.when(ci == pl.num_programs(1) - 1)
    def _():
        so_ref[0] = s_ref[...]


def gla(proj, w_a2p, b_a, gla_norm_w, s0, n_seq, chunk):
    rows = proj.shape[0]
    n_chunk = rows // (n_seq * chunk)
    kw, vw = GLA_HEADS * GLA_DK, GLA_HEADS * GLA_DV
    rows_at = lambda width, col: pl.BlockSpec((chunk, width), lambda b, c: (b * n_chunk + c, col // width))
    state_spec = pl.BlockSpec((1, GLA_HEADS, GLA_DK, GLA_DV), lambda b, c: (b, 0, 0, 0))
    return pl.pallas_call(
        _gla_kernel,
        grid=(n_seq, n_chunk),
        in_specs=[rows_at(kw, C_QG), rows_at(kw, C_KG), rows_at(vw, C_VG), rows_at(LANE, C_AG), rows_at(vw, C_ZG),
                  pl.BlockSpec((LANE, kw), lambda b, c: (0, 0)),
                  pl.BlockSpec((1, kw), lambda b, c: (0, 0)),
                  pl.BlockSpec((1, GLA_DV), lambda b, c: (0, 0)),
                  state_spec],
        out_specs=[pl.BlockSpec((chunk, vw), lambda b, c: (b * n_chunk + c, 0)), state_spec],
        out_shape=[jax.ShapeDtypeStruct((rows, vw), F32),
                   jax.ShapeDtypeStruct((n_seq, GLA_HEADS, GLA_DK, GLA_DV), F32)],
        scratch_shapes=[pltpu.VMEM((GLA_HEADS, GLA_DK, GLA_DV), F32)],
        compiler_params=_cparams(("parallel", "arbitrary")),
        name="gla",
    )(proj, proj, proj, proj, proj, w_a2p, b_a.reshape(1, -1), gla_norm_w.reshape(1, -1), s0)


def _out_kernel(on_ref, zn_ref, og_ref, mn_ref, mg_ref, x_ref, gate_ref, wn_ref, wg_ref, wo_ref, fw_ref, y_ref):
    o_nsa = (on_ref[...] * _silu(zn_ref[...])).astype(BF16)
    merged = (jax.nn.sigmoid(mn_ref[...]) * _dot(o_nsa, wn_ref[...])
              + jax.nn.sigmoid(mg_ref[...]) * _dot(og_ref[...].astype(BF16), wg_ref[...]))
    y = x_ref[...] + gate_ref[...] * _dot(merged.astype(BF16), wo_ref[...])
    y_ref[...] = y * lax.rsqrt(jnp.mean(y * y, axis=-1, keepdims=True) + EPS) * fw_ref[...]


def out_proj(o_nsa, o_gla, proj, x, gate, w_o_nsa, w_o_gla, w_out, final_norm_w, tm):
    rows = x.shape[0]
    per_row = gate.shape[0] != 1
    gate_spec = pl.BlockSpec((tm, D_MODEL), lambda i: (i, 0)) if per_row else pl.BlockSpec((1, D_MODEL), lambda i: (0, 0))
    resident = lambda a: pl.BlockSpec(a.shape, lambda i: (0, 0), pipeline_mode=pl.Buffered(1))
    return pl.pallas_call(
        _out_kernel,
        grid=(rows // tm,),
        in_specs=[pl.BlockSpec((tm, NSA_WIDTH), lambda i: (i, 0)),
                  pl.BlockSpec((tm, NSA_WIDTH), lambda i: (i, C_ZN // NSA_WIDTH)),
                  pl.BlockSpec((tm, NSA_WIDTH), lambda i: (i, 0)),
                  pl.BlockSpec((tm, D_MODEL), lambda i: (i, C_MN // D_MODEL)),
                  pl.BlockSpec((tm, D_MODEL), lambda i: (i, C_MG // D_MODEL)),
                  pl.BlockSpec((tm, D_MODEL), lambda i: (i, 0)),
                  gate_spec, resident(w_o_nsa), resident(w_o_gla), resident(w_out),
                  pl.BlockSpec((1, D_MODEL), lambda i: (0, 0))],
        out_specs=pl.BlockSpec((tm, D_MODEL), lambda i: (i, 0)),
        out_shape=jax.ShapeDtypeStruct((rows, D_MODEL), F32),
        compiler_params=_cparams(("parallel",)),
        name="outproj",
    )(o_nsa, proj, o_gla, proj, proj, x, gate, w_o_nsa, w_o_gla, w_out, final_norm_w.reshape(1, D_MODEL))


def _feature_major(a):
    lead = a.shape[:-4]
    n = len(lead)
    a = jnp.transpose(a, tuple(range(n)) + (n + 1, n + 2, n + 3, n))
    return a.reshape(lead + (KV_ROW, a.shape[-1]))


def _token_major(a_t, lead):
    rows = a_t.shape[-1]
    a = a_t.reshape(a_t.shape[:-2] + (2, NSA_KV_HEADS, HEAD_DIM, rows))
    n = a.ndim - 4
    a = jnp.transpose(a, tuple(range(n)) + (n + 3, n, n + 1, n + 2))
    return a.reshape(lead + (rows, 2, NSA_KV_HEADS, HEAD_DIM))


def kernel(x_prompt, x_sample, cache_kv_cmp, cache_kv_sel, cache_kv_win, state_gla, page_table, c_prompt, c_sample, norm_w, w_ada, b_ada, w_in, cmp_pos, cmp_w1, cmp_b1, cmp_w2, cmp_b2, w_a2, b_a, gla_norm_w, w_o_nsa, w_o_gla, w_out, final_norm_w):
    assert x_prompt.shape[0] == 1 and norm_w.shape[0] == 1, "one prompt sequence, one layer"
    t_p = x_prompt.shape[1]
    b_s, t_s = x_sample.shape[:2]
    past = page_table.shape[1] * PAGE
    wb = cache_kv_win.shape[2]
    assert t_s == 8 and wb == WINDOW and past % (S_PAGES * PAGE) == 0 and t_p % SEL_TILE == 0
    rows_s = b_s * t_s

    c_rows = jnp.zeros((40, D_MODEL), F32).at[0:1].set(c_prompt).at[1:1 + b_s].set(c_sample)
    mod = ada_mod(c_rows, w_ada[0], b_ada[0])
    shift, scale, gate = mod[:, :D_MODEL], mod[:, D_MODEL:2 * D_MODEL], mod[:, 2 * D_MODEL:]
    per_row = lambda a: jnp.repeat(a[1:1 + b_s], t_s, axis=0)

    w_t = jnp.transpose(w_in[0])
    xp = x_prompt.reshape(t_p, D_MODEL)
    xs = x_sample.reshape(rows_s, D_MODEL)
    proj_p = in_proj(xp, scale[0:1], shift[0:1], norm_w[0], w_t, RM_OFFSETS, 1024, RM_TILE, False)
    projt_p = in_proj(xp, scale[0:1], shift[0:1], norm_w[0], w_t, FM_OFFSETS, 512, FM_TILE, True)
    proj_s = in_proj(xs, per_row(scale), per_row(shift), norm_w[0], w_t, RM_OFFSETS, rows_s, RM_TILE, False)
    projt_s = in_proj(xs, per_row(scale), per_row(shift), norm_w[0], w_t, FM_OFFSETS, rows_s, FM_TILE, True)

    cos_p, sin_p = _rope_tables(jnp.arange(t_p, dtype=jnp.int32))
    cos_s, sin_s = _rope_tables(jnp.tile(past + jnp.arange(t_s, dtype=jnp.int32), b_s))
    qt_p, kvc_p, kvs_p, kvw_p, ks_p, kw_p, vst_p, vwt_p = rope_stage(projt_p, cos_p, sin_p, 512, True)
    qt_s, kvc_s, kvs_s, kvw_s = rope_stage(projt_s, cos_s, sin_s, rows_s, False)

    pb = pos_bias(cmp_pos[0], cmp_w1[0], cmp_b1[0])
    cmp_consts = _compress_weights(cmp_w1[0], cmp_w2[0], cmp_b2[0])
    perm, w1t, w2k, w2vt, b2k, b2vc = cmp_consts
    ident = jnp.arange(t_p // PAGE, dtype=jnp.int32)[None, :]
    pages_p = jnp.transpose(kvc_p.reshape(KV_ROW, t_p // PAGE, PAGE), (1, 0, 2))
    kc_p, vct_p = compress(pages_p, ident, perm, w1t, pb, w2k, w2vt, b2k, b2vc)
    pool_c = _feature_major(cache_kv_cmp[0])
    kc_s, vct_s = compress(pool_c, page_table, perm, w1t, pb, w2k, w2vt, b2k, b2vc)

    n_ent = kc_p.shape[1]
    kc_h = jnp.transpose(kc_p[0].reshape(n_ent, NSA_KV_HEADS, HEAD_DIM), (1, 0, 2)).astype(BF16)
    vct_h = vct_p[0].reshape(NSA_KV_HEADS, HEAD_DIM, n_ent).astype(BF16)
    kw_h = jnp.pad(kw_p, ((0, 0), (WINDOW, 0), (0, 0)))
    vwt_h = jnp.pad(vwt_p, ((0, 0), (WINDOW // Q_BLOCK, 0), (0, 0), (0, 0)))
    g_p = projt_p[R_GN:R_GN + 48].reshape(NSA_KV_HEADS, 12, t_p)
    g_p = jnp.pad(g_p, ((0, 0), (0, 4), (0, 0)))
    mt_p = _cmp_to_sel_t(n_ent, t_p // SEL_BLOCK, t_p // SEL_BLOCK)
    o_nsa_p = nsa_prompt(qt_p, kc_h, vct_h, ks_p, vst_p, kw_h, vwt_h, g_p, mt_p, t_p)

    q5 = qt_s.reshape(NSA_KV_HEADS, NSA_GROUP, HEAD_DIM, b_s, t_s)
    q_t = jnp.transpose(q5, (3, 0, 2, 1, 4)).reshape(b_s, NSA_KV_HEADS, HEAD_DIM, NSA_GROUP * t_s)
    eye = jnp.eye(NSA_KV_HEADS, dtype=BF16)
    qbd = jnp.einsum('bhdc,hk->bhdkc', q_t, eye).reshape(b_s, HALF_ROW, S_COLS)
    new_keys = lambda a_t: jnp.pad(jnp.transpose(a_t.reshape(KV_ROW, b_s, t_s), (1, 0, 2)), ((0, 0), (0, 0), (0, PAGE - t_s)))
    g_s = projt_s[R_GN:R_GN + 48].reshape(NSA_KV_HEADS, NSA_GROUP, 3, b_s, t_s)
    g_s = jnp.transpose(g_s, (3, 2, 0, 1, 4)).reshape(b_s, 3, S_COLS)
    g_s = jnp.pad(g_s, ((0, 0), (0, 5), (0, 0)))
    n_blk_s = -(-(past + t_s) // SEL_BLOCK)
    mt_s = _cmp_to_sel_t(kc_s.shape[1], n_blk_s, -(-n_blk_s // 8) * 8)
    col = np.arange(S_COLS)
    gsum = jnp.asarray(((col[:, None] // 32 == col[None, :] // 32) & (col[:, None] % 8 == col[None, :] % 8)).astype(np.float32), dtype=BF16)
    cache_wt = _feature_major(cache_kv_win[0])
    o_t = nsa_sample(_feature_major(cache_kv_sel[0]), page_table, qbd, kc_s, vct_s, cache_wt,
                     new_keys(kvw_s), new_keys(kvs_s), g_s, mt_s, gsum)
    o6 = o_t.reshape(b_s, NSA_KV_HEADS, HEAD_DIM, NSA_KV_HEADS, NSA_GROUP, t_s)
    o_nsa_s = jnp.stack([o6[:, h, :, h] for h in range(NSA_KV_HEADS)], axis=1)
    o_nsa_s = jnp.transpose(o_nsa_s, (0, 4, 1, 3, 2)).reshape(rows_s, NSA_WIDTH)

    w_a2p = jnp.zeros((LANE, GLA_HEADS * GLA_DK), F32).at[:GLA_RANK].set(w_a2[0])
    s0_p = jnp.zeros((1, GLA_HEADS, GLA_DK, GLA_DV), F32)
    o_gla_p, st_p = gla(proj_p, w_a2p, b_a[0], gla_norm_w[0], s0_p, 1, GLA_CHUNK)
    o_gla_s, st_s = gla(proj_s, w_a2p, b_a[0], gla_norm_w[0], state_gla[0], b_s, t_s)

    wn, wg, wo = w_o_nsa[0].astype(BF16), w_o_gla[0].astype(BF16), w_out[0].astype(BF16)
    y_p = out_proj(o_nsa_p, o_gla_p, proj_p, xp, gate[0:1], wn, wg, wo, final_norm_w, 256)
    y_s = out_proj(o_nsa_s, o_gla_s, proj_s, xs, per_row(gate), wn, wg, wo, final_norm_w, rows_s)

    sample_rows = lambda a_t: _token_major(jnp.transpose(a_t.reshape(KV_ROW, b_s, t_s), (1, 0, 2)), (1, b_s))
    win_t = jnp.concatenate([cache_wt, jnp.transpose(kvw_s.reshape(KV_ROW, b_s, t_s), (1, 0, 2))], axis=2)[:, :, t_s:]
    n_win = min(WINDOW, t_p)
    return (y_p.reshape(x_prompt.shape), y_s.reshape(x_sample.shape),
            _token_major(kvc_p, (1, 1)), sample_rows(kvc_s), _token_major(kvs_p, (1, 1)), sample_rows(kvs_s),
            _token_major(kvw_p[:, t_p - n_win:], (1, 1)), _token_major(win_t, (1, b_s)),
            st_p[None], st_s[None])
```

```python
import functools

import jax
import jax.numpy as jnp
import numpy as np
from jax import lax
from jax.experimental import pallas as pl
from jax.experimental.pallas import tpu as pltpu

F32 = jnp.float32
BF16 = jnp.bfloat16

D_MODEL = 2048
HEAD_DIM = 64
NSA_HEADS = 16
NSA_KV_HEADS = 4
NSA_GROUP = 4
NSA_WIDTH = 1024
HALF_ROW = NSA_KV_HEADS * HEAD_DIM
KV_ROW = 2 * HALF_ROW
CMP_LEN = 32
CMP_STRIDE = 16
CMP_HIDDEN = 128
SEL_BLOCK = 64
N_SELECT = 16
N_LOCAL = 2
WINDOW = 512
Q_BLOCK = 128
PAGE = 128
GLA_HEADS = 4
GLA_DK = 128
GLA_DV = 256
GLA_RANK = 16
GLA_TAU = 16.0
GLA_CHUNK = 64
ROPE_THETA = 10000.0
EPS = 1e-6
NEG = -1e30
BIG = 1e30
TINY = 1e-30
REMOVED = -3e38
LOG2E = 1.4426950408889634
SEL_TILE = 512
V_AUG = HEAD_DIM + 16

LANE = 128
VMEM_LIMIT = 48 * 1024 * 1024

(W_Q, W_KV, W_GN, W_ZN, W_QG, W_KG, W_VG, W_AG, W_ZG, W_MN, W_MG) = (
    0, 1024, 2560, 2608, 3632, 4144, 4656, 5680, 5696, 6720, 8768)
RM_TILE = 512
W_ALIGN = 16
C_MN, C_MG, C_ZN, C_VG, C_ZG, C_QG, C_KG, C_AG = 0, 2048, 4096, 5120, 6144, 7168, 7680, 8192
RM_SOURCES = ((W_MN, 4), (W_MG, 4), (W_ZN, 2), (W_VG, 2), (W_ZG, 2), (W_QG, 1), (W_KG, 1), (W_AG, 1))
RM_OFFSETS = tuple(start + RM_TILE * k for start, tiles in RM_SOURCES for k in range(tiles))
RM_COLS = RM_TILE * len(RM_OFFSETS)
R_Q, R_KV, R_GN = W_Q, W_KV, W_GN
FM_TILE = 1344
FM_ROWS = 2 * FM_TILE
FM_OFFSETS = (0, FM_TILE)


def _cparams(sem):
    return pltpu.CompilerParams(dimension_semantics=sem, vmem_limit_bytes=VMEM_LIMIT)


def _dot(a, b):
    return jnp.dot(a, b, preferred_element_type=F32)


def _dot_nt(a, b):
    return lax.dot_general(a, b, (((1,), (1,)), ((), ())), preferred_element_type=F32)


def _silu(x):
    return x * jax.nn.sigmoid(x)


def _ada_kernel(c_ref, w_ref, b_ref, o_ref):
    o_ref[...] = _dot(c_ref[...].astype(BF16), w_ref[...].astype(BF16)) + b_ref[...]


def ada_mod(c_rows, w_ada, b_ada):
    rows, tn = c_rows.shape[0], 512
    n = w_ada.shape[1]
    return pl.pallas_call(
        _ada_kernel,
        grid=(n // tn,),
        in_specs=[pl.BlockSpec((rows, D_MODEL), lambda j: (0, 0)),
                  pl.BlockSpec((D_MODEL, tn), lambda j: (0, j)),
                  pl.BlockSpec((1, tn), lambda j: (0, j))],
        out_specs=pl.BlockSpec((rows, tn), lambda j: (0, j)),
        out_shape=jax.ShapeDtypeStruct((rows, n), F32),
        compiler_params=_cparams(("parallel",)),
        name="ada",
    )(c_rows, w_ada, b_ada.reshape(1, n))


def _modulated_norm(x_ref, sc_ref, sh_ref, nw_ref, h_ref):
    x = x_ref[...]
    y = x * lax.rsqrt(jnp.mean(x * x, axis=-1, keepdims=True) + EPS) * nw_ref[...]
    h_ref[...] = (y * (1.0 + sc_ref[...]) + sh_ref[...]).astype(BF16)


def _inproj_rm_kernel(off_ref, x_ref, sc_ref, sh_ref, nw_ref, w_ref, o_ref, h_ref):
    @pl.when(pl.program_id(1) == 0)
    def _():
        _modulated_norm(x_ref, sc_ref, sh_ref, nw_ref, h_ref)

    o_ref[...] = _dot_nt(h_ref[...], w_ref[...].astype(BF16))


def _inproj_fm_kernel(off_ref, x_ref, sc_ref, sh_ref, nw_ref, w_ref, o_ref, h_ref):
    @pl.when(pl.program_id(1) == 0)
    def _():
        _modulated_norm(x_ref, sc_ref, sh_ref, nw_ref, h_ref)

    o_ref[...] = _dot_nt(w_ref[...].astype(BF16), h_ref[...])


def in_proj(x, scale, shift, norm_w, w_t, row_offsets, tm, tn, feature_major):
    rows, n = x.shape[0], len(row_offsets) * tn
    per_row = scale.shape[0] != 1
    mod_spec = (pl.BlockSpec((tm, D_MODEL), lambda i, j, off: (i, 0)) if per_row
                else pl.BlockSpec((1, D_MODEL), lambda i, j, off: (0, 0)))
    if feature_major:
        body, out_spec, out_shape = _inproj_fm_kernel, pl.BlockSpec((tn, tm), lambda i, j, off: (j, i)), (n, rows)
    else:
        body, out_spec, out_shape = _inproj_rm_kernel, pl.BlockSpec((tm, tn), lambda i, j, off: (i, j)), (rows, n)
    grid_spec = pltpu.PrefetchScalarGridSpec(
        num_scalar_prefetch=1,
        grid=(rows // tm, len(row_offsets)),
        in_specs=[pl.BlockSpec((tm, D_MODEL), lambda i, j, off: (i, 0)),
                  mod_spec, mod_spec,
                  pl.BlockSpec((1, D_MODEL), lambda i, j, off: (0, 0)),
                  pl.BlockSpec((pl.Element(tn), pl.Element(D_MODEL)), lambda i, j, off: (off[j] * W_ALIGN, 0))],
        out_specs=out_spec,
        scratch_shapes=[pltpu.VMEM((tm, D_MODEL), BF16)],
    )
    return pl.pallas_call(
        body,
        grid_spec=grid_spec,
        out_shape=jax.ShapeDtypeStruct(out_shape, F32),
        compiler_params=_cparams(("parallel", "arbitrary")),
        name="inproj_fm" if feature_major else "inproj_rm",
    )(jnp.asarray([o // W_ALIGN for o in row_offsets], jnp.int32), x, scale, shift, norm_w.reshape(1, D_MODEL), w_t)


def _rope_kernel(q_ref, c_ref, s_ref, w_ref, cos_ref, sin_ref, qo_ref, co_ref, so_ref, wo_ref, *tile_refs):
    cos, sin = cos_ref[...], sin_ref[...]
    hh = HEAD_DIM // 2
    tr = cos.shape[1]

    def rot(src, head):
        x1 = src[head * HEAD_DIM:head * HEAD_DIM + hh, :]
        x2 = src[head * HEAD_DIM + hh:(head + 1) * HEAD_DIM, :]
        return x1 * cos - x2 * sin, x2 * cos + x1 * sin

    q_scale = HEAD_DIM ** -0.5 * LOG2E
    for head in range(NSA_HEADS):
        o1, o2 = rot(q_ref, head)
        qo_ref[head * HEAD_DIM:head * HEAD_DIM + hh, :] = (o1 * q_scale).astype(BF16)
        qo_ref[head * HEAD_DIM + hh:(head + 1) * HEAD_DIM, :] = (o2 * q_scale).astype(BF16)
    for src, dst in ((c_ref, co_ref), (s_ref, so_ref), (w_ref, wo_ref)):
        for head in range(NSA_KV_HEADS):
            o1, o2 = rot(src, head)
            dst[head * HEAD_DIM:head * HEAD_DIM + hh, :] = o1
            dst[head * HEAD_DIM + hh:(head + 1) * HEAD_DIM, :] = o2
        dst[HALF_ROW:, :] = src[HALF_ROW:, :]
    if tile_refs:
        ks_ref, kw_ref, vs_ref, vw_ref = tile_refs
        lane = lax.broadcasted_iota(jnp.int32, (1, LANE), 1)
        r = lax.broadcasted_iota(jnp.int32, (tr, 1), 0)
        onehot = jnp.where(lane - HEAD_DIM == (r // SEL_BLOCK) % (SEL_TILE // SEL_BLOCK), 1.0, 0.0)
        ones_row = jnp.where(lax.broadcasted_iota(jnp.int32, (V_AUG - HEAD_DIM, SEL_TILE), 0) == 0, 1.0, 0.0)
        for pair in range(NSA_KV_HEADS // 2):
            k_pair = so_ref[pair * LANE:(pair + 1) * LANE, :].T
            ks_ref[2 * pair] = jnp.where(lane < HEAD_DIM, k_pair, onehot).astype(BF16)
            ks_ref[2 * pair + 1] = jnp.where(lane < HEAD_DIM, pltpu.roll(k_pair, HEAD_DIM, 1), onehot).astype(BF16)
            kw_pair = wo_ref[pair * LANE:(pair + 1) * LANE, :].T.astype(BF16)
            kw_ref[2 * pair] = kw_pair[:, :HEAD_DIM]
            kw_ref[2 * pair + 1] = kw_pair[:, HEAD_DIM:]
        for head in range(NSA_KV_HEADS):
            rows = slice(HALF_ROW + head * HEAD_DIM, HALF_ROW + (head + 1) * HEAD_DIM)
            v = so_ref[rows, :]
            for w in range(tr // SEL_TILE):
                vs_ref[head, w] = jnp.concatenate([v[:, w * SEL_TILE:(w + 1) * SEL_TILE], ones_row], axis=0).astype(BF16)
            v = wo_ref[rows, :].astype(BF16)
            for w in range(tr // Q_BLOCK):
                vw_ref[head, w] = v[:, w * Q_BLOCK:(w + 1) * Q_BLOCK]


def rope_stage(proj_t, cos_t, sin_t, tr, with_tiles):
    tok = proj_t.shape[1]
    kv_spec = lambda k: pl.BlockSpec((KV_ROW, tr), lambda i, k=k: (R_KV // KV_ROW + k, i))
    out_kv = jax.ShapeDtypeStruct((KV_ROW, tok), F32)
    tab = pl.BlockSpec((HEAD_DIM // 2, tr), lambda i: (0, i))
    out_specs = [pl.BlockSpec((NSA_WIDTH, tr), lambda i: (0, i))] + [pl.BlockSpec((KV_ROW, tr), lambda i: (0, i))] * 3
    out_shape = [jax.ShapeDtypeStruct((NSA_WIDTH, tok), BF16), out_kv, out_kv, out_kv]
    if with_tiles:
        k_rows = lambda width: jax.ShapeDtypeStruct((NSA_KV_HEADS, tok, width), BF16)
        k_spec = lambda width: pl.BlockSpec((NSA_KV_HEADS, tr, width), lambda i: (0, i, 0))
        v_tiles = lambda rows, tile: jax.ShapeDtypeStruct((NSA_KV_HEADS, tok // tile, rows, tile), BF16)
        v_spec = lambda rows, tile: pl.BlockSpec((NSA_KV_HEADS, tr // tile, rows, tile), lambda i: (0, i, 0, 0))
        out_specs += [k_spec(LANE), k_spec(HEAD_DIM), v_spec(V_AUG, SEL_TILE), v_spec(HEAD_DIM, Q_BLOCK)]
        out_shape += [k_rows(LANE), k_rows(HEAD_DIM), v_tiles(V_AUG, SEL_TILE), v_tiles(HEAD_DIM, Q_BLOCK)]
    return pl.pallas_call(
        _rope_kernel,
        grid=(tok // tr,),
        in_specs=[pl.BlockSpec((NSA_WIDTH, tr), lambda i: (R_Q // NSA_WIDTH, i)), kv_spec(0), kv_spec(1), kv_spec(2), tab, tab],
        out_specs=out_specs,
        out_shape=out_shape,
        compiler_params=_cparams(("parallel",)),
        name="rope",
    )(proj_t, proj_t, proj_t, proj_t, cos_t, sin_t)


def _rope_tables(pos):
    half = HEAD_DIM // 2
    inv = ROPE_THETA ** (-jnp.arange(half, dtype=F32) / half)
    ang = inv[:, None] * pos.astype(F32)[None, :]
    return jnp.cos(ang), jnp.sin(ang)


def _posbias_kernel(p_ref, w_ref, b_ref, o_ref):
    for x in range(2):
        o_ref[x] = _dot(p_ref[x], w_ref[x]) + b_ref[x]


def pos_bias(cmp_pos, cmp_w1, cmp_b1):
    k = CMP_LEN * HEAD_DIM
    pos = jnp.zeros((2, 8, k), F32).at[:, 0].set(cmp_pos.reshape(2, k))
    out = pl.pallas_call(
        _posbias_kernel,
        out_shape=jax.ShapeDtypeStruct((2, 8, CMP_HIDDEN), F32),
        compiler_params=pltpu.CompilerParams(vmem_limit_bytes=VMEM_LIMIT),
        name="posbias",
    )(pos, cmp_w1.reshape(2, k, CMP_HIDDEN), cmp_b1.reshape(2, 1, CMP_HIDDEN))
    return out[:, 0]


CMP_PAGES = 16
CMP_CHUNKS = CMP_PAGES * PAGE // CMP_STRIDE
CHUNKS_PER_PAGE = PAGE // CMP_STRIDE


def _compress_kernel(pt_ref, *refs):
    pages = refs[:CMP_PAGES]
    perm_ref, w1_ref, pb_ref, w2_ref, w2t_ref, b2_ref, b2c_ref, k_ref, kt_ref, vt_ref, carry_ref = refs[CMP_PAGES:]
    s = pl.program_id(1)

    @pl.when(s == 0)
    def _():
        carry_ref[...] = jnp.zeros_like(carry_ref)

    n = CMP_CHUNKS
    perm = perm_ref[...]
    rows_by_p = [_dot_nt(perm, pg[0].astype(BF16)) for pg in pages]
    row0 = lax.broadcasted_iota(jnp.int32, (n, 1), 0) == 0
    for t in range(KV_ROW // LANE):
        x = t // 2
        sl = slice(t * LANE, (t + 1) * LANE)
        acc = jnp.zeros((n, 4 * CMP_HIDDEN), F32)
        for pp in range(CMP_STRIDE // 2):
            parts = []
            for p in (2 * pp, 2 * pp + 1):
                parts.append(jnp.concatenate(
                    [r[p * CHUNKS_PER_PAGE:(p + 1) * CHUNKS_PER_PAGE, sl] for r in rows_by_p], axis=0))
            lhs = jnp.concatenate(parts, axis=1).astype(BF16)
            acc = acc + _dot(lhs, w1_ref[x, pp])
        hid = []
        for hh in range(2):
            part0 = acc[:, hh * 256:hh * 256 + CMP_HIDDEN]
            part1 = acc[:, hh * 256 + CMP_HIDDEN:(hh + 1) * 256]
            csl = slice((t * 2 + hh) * CMP_HIDDEN, (t * 2 + hh + 1) * CMP_HIDDEN)
            prev = jnp.where(row0, carry_ref[0:1, csl], pltpu.roll(part0, 1, 0))
            carry_ref[0:1, csl] = part0[n - 1:n, :]
            hid.append(_silu(prev + part1 + pb_ref[x:x + 1, :]))
        hid = jnp.concatenate(hid, axis=1).astype(BF16)
        out_t = _dot_nt(w2t_ref[x], hid) + b2c_ref[x]
        rows = slice((t % 2) * LANE, (t % 2 + 1) * LANE)
        if x == 0:
            k_ref[0, :, sl] = _dot(hid, w2_ref[...]) + b2_ref[...]
            kt_ref[0, rows, :] = out_t
        else:
            vt_ref[0, rows, :] = out_t


def compress(pool_t, page_table, perm, w1t, pb, w2k, w2t, b2k, b2c):
    b, n_pages = page_table.shape
    steps = n_pages // CMP_PAGES
    n_blk = n_pages * CHUNKS_PER_PAGE
    page_spec = lambda k: pl.BlockSpec((1, KV_ROW, PAGE), lambda bi, si, pt, k=k: (pt[bi, si * CMP_PAGES + k], 0, 0))
    const = lambda a: pl.BlockSpec(a.shape, lambda bi, si, pt: (0,) * a.ndim)
    consts = (perm, w1t, pb, w2k, w2t, b2k, b2c)
    fm_spec = pl.BlockSpec((1, HALF_ROW, CMP_CHUNKS), lambda bi, si, pt: (bi, 0, si))
    fm_shape = jax.ShapeDtypeStruct((b, HALF_ROW, n_blk), F32)
    grid_spec = pltpu.PrefetchScalarGridSpec(
        num_scalar_prefetch=1,
        grid=(b, steps),
        in_specs=[page_spec(k) for k in range(CMP_PAGES)] + [const(a) for a in consts],
        out_specs=[pl.BlockSpec((1, CMP_CHUNKS, HALF_ROW), lambda bi, si, pt: (bi, si, 0)), fm_spec, fm_spec],
        scratch_shapes=[pltpu.VMEM((8, 8 * CMP_HIDDEN), F32)],
    )
    return pl.pallas_call(
        _compress_kernel,
        grid_spec=grid_spec,
        out_shape=[jax.ShapeDtypeStruct((b, n_blk, HALF_ROW), F32), fm_shape, fm_shape],
        compiler_params=_cparams(("parallel", "arbitrary")),
        name="compress",
    )(page_table, *([pool_t] * CMP_PAGES), *consts)


def _compress_weights(cmp_w1, cmp_w2, cmp_b2):
    w1 = cmp_w1.reshape(2, 2, CMP_STRIDE // 2, 2, HEAD_DIM, CMP_HIDDEN)
    w1 = jnp.transpose(w1, (0, 2, 3, 4, 1, 5))
    eye = jnp.eye(2, dtype=F32)
    w1t = jnp.einsum('xqpdje,hk->xqphdkje', w1, eye).reshape(2, CMP_STRIDE // 2, 256, 512).astype(BF16)
    w2bd = jnp.einsum('xed,hk->xhekd', cmp_w2, eye).reshape(2, 256, LANE).astype(BF16)
    b2t = jnp.concatenate([cmp_b2, cmp_b2], axis=1)
    r = np.arange(PAGE)
    perm = np.zeros((PAGE, PAGE), np.float32)
    perm[(r % CMP_STRIDE) * CHUNKS_PER_PAGE + r // CMP_STRIDE, r] = 1.0
    return (jnp.asarray(perm, dtype=BF16), w1t, w2bd[0], jnp.transpose(w2bd, (0, 2, 1)), b2t[0:1], b2t.reshape(2, LANE, 1))


def _softmax0(s, mask):
    s = jnp.where(mask, s, NEG)
    m = jnp.max(s, axis=0, keepdims=True)
    p = jnp.where(mask, jnp.exp2(s - m), 0.0)
    return p / jnp.maximum(jnp.sum(p, axis=0, keepdims=True), TINY)


def _split_dot(a, x):
    hi = x.astype(BF16)
    lo = (x - hi.astype(F32)).astype(BF16)
    return _dot(a, hi) + _dot(a, lo)


def _split_dot_r(x, a):
    hi = x.astype(BF16)
    lo = (x - hi.astype(F32)).astype(BF16)
    return _dot(hi, a) + _dot(lo, a)


def _top_blocks(imps, cur):
    blk = lax.broadcasted_iota(jnp.int32, (imps[0].shape[0], 1), 0)
    forced = (blk == 0) | ((blk <= cur) & (blk > cur - N_LOCAL))
    imps = tuple(jnp.where(blk > cur, -BIG, jnp.where(forced, BIG, imp)) for imp in imps)
    blk_f = blk.astype(F32)

    def pick(_, carry):
        out = []
        for imp, sel in carry:
            mx = jnp.max(imp, axis=0, keepdims=True)
            first = jnp.min(jnp.where(imp == mx, blk_f, 1e9), axis=0, keepdims=True)
            hit = blk_f == first
            out.append((jnp.where(hit, REMOVED, imp), jnp.where(hit, 1.0, sel)))
        return tuple(out)

    final = lax.fori_loop(0, N_SELECT, pick, tuple((imp, jnp.zeros_like(imp)) for imp in imps))
    return [sel for _, sel in final]


def _flash_update(state, s, mask, v_t):
    m, l, acc = state
    s = jnp.where(mask, s, NEG)
    m_new = jnp.maximum(m, jnp.max(s, axis=0, keepdims=True))
    p = jnp.where(mask, jnp.exp2(s - m_new), 0.0)
    alpha = jnp.exp2(m - m_new)
    l = alpha * l + jnp.sum(p, axis=0, keepdims=True)
    acc = alpha * acc + _dot(v_t, p.astype(BF16))
    return m_new, l, acc


def _flash_update_biased(state, s, v_aug):
    m, acc = state
    m_new = jnp.maximum(m, jnp.max(s, axis=0, keepdims=True))
    p = jnp.exp2(s - m_new)
    acc = jnp.exp2(m - m_new) * acc + _dot(v_aug, p.astype(BF16))
    return m_new, acc


def _repeat_rows(grp, rows, reps):
    return jnp.concatenate([jnp.broadcast_to(grp[r:r + 1, :], (reps, grp.shape[1])) for r in range(rows)], axis=0)


def _cmp_mask(n_rows, pos_q):
    r = lax.broadcasted_iota(jnp.int32, (n_rows, 1), 0)
    return (r >= 1) & (r * CMP_STRIDE + (CMP_LEN - CMP_STRIDE - 1) <= pos_q)


WIN_KEYS = WINDOW + Q_BLOCK
P_HEADS = 2
BLOCKS_PER_TILE = SEL_TILE // SEL_BLOCK


def _nsa_prompt_kernel(qt_ref, kc_ref, vct_ref, ks_ref, vst_ref, kw_ref, vwt_ref, gt_ref, mt_ref, o_ref, sel_ref, s_ref):
    i = pl.program_id(1)
    cols = NSA_GROUP * Q_BLOCK
    lane = lax.broadcasted_iota(jnp.int32, (1, Q_BLOCK), 1)
    pos_q = i * Q_BLOCK + lane
    tile4 = lambda a: jnp.concatenate([a] * NSA_GROUP, axis=1)
    heads = range(P_HEADS)
    q_ts = []
    for h in heads:
        q_blk = qt_ref[h * NSA_GROUP * HEAD_DIM:(h + 1) * NSA_GROUP * HEAD_DIM, :]
        q_ts.append(jnp.concatenate([q_blk[g * HEAD_DIM:(g + 1) * HEAD_DIM, :] for g in range(NSA_GROUP)], axis=1))

    nc = kc_ref.shape[1]
    s_cmp = [_dot(kc_ref[h], q_ts[h]) for h in heads]
    s_win = [_dot(kw_ref[h, pl.ds(pl.multiple_of(i * Q_BLOCK, Q_BLOCK), WIN_KEYS), :], q_ts[h]) for h in heads]

    mask_c = tile4(_cmp_mask(nc, pos_q))
    o_c, imps = [], []
    for h in heads:
        p_c = _softmax0(s_cmp[h], mask_c)
        o_c.append(_dot(vct_ref[h], p_c.astype(BF16)))
        pg = p_c[:, 0:Q_BLOCK]
        for g in range(1, NSA_GROUP):
            pg = pg + p_c[:, g * Q_BLOCK:(g + 1) * Q_BLOCK]
        imps.append(_split_dot(mt_ref[...], pg))
    for h, sel in enumerate(_top_blocks(imps, pos_q // SEL_BLOCK)):
        sel_ref[h] = sel

    zeros_q = jnp.zeros((LANE - HEAD_DIM - 16, cols), BF16)

    def q_aug(h, j):
        grp = sel_ref[h, pl.ds(pl.multiple_of(j * BLOCKS_PER_TILE, BLOCKS_PER_TILE), BLOCKS_PER_TILE), :]
        bias = jnp.concatenate([jnp.where(grp > 0.5, 0.0, NEG), jnp.zeros_like(grp)], axis=0)
        return jnp.concatenate([q_ts[h], tile4(bias).astype(BF16), zeros_q], axis=0)

    def scores(h, j):
        return _dot(ks_ref[h, pl.ds(pl.multiple_of(j * SEL_TILE, SEL_TILE), SEL_TILE), :], q_aug(h, j))

    def sel_body(j, states):
        st0, st1 = states
        s1 = scores(1, j)
        st0 = _flash_update_biased(st0, s_ref[...], vst_ref[0, j])
        s_ref[...] = scores(0, j + 1)
        st1 = _flash_update_biased(st1, s1, vst_ref[1, j])
        return st0, st1

    init = tuple((jnp.full((1, cols), NEG, F32), jnp.zeros((V_AUG, cols), F32)) for _ in heads)
    j_diag = (i * Q_BLOCK) // SEL_TILE
    s_ref[...] = scores(0, 0)
    states = lax.fori_loop(0, j_diag, sel_body, init)
    key_pos = j_diag * SEL_TILE + lax.broadcasted_iota(jnp.int32, (SEL_TILE, 1), 0)
    causal = tile4(jnp.where(key_pos <= pos_q, 0.0, NEG))
    last = (s_ref[...], scores(1, j_diag))
    o_s = []
    for h in heads:
        _, acc = _flash_update_biased(states[h], last[h] + causal, vst_ref[h, j_diag])
        o_s.append(acc[:HEAD_DIM] / jnp.maximum(acc[HEAD_DIM:HEAD_DIM + 1], TINY))

    w_pos = i * Q_BLOCK - WINDOW + lax.broadcasted_iota(jnp.int32, (WIN_KEYS, 1), 0)
    mask_w = tile4((w_pos <= pos_q) & (w_pos > pos_q - WINDOW) & (w_pos >= 0))
    o_w = []
    for h in heads:
        p_w = _softmax0(s_win[h], mask_w).astype(BF16)
        acc = jnp.zeros((HEAD_DIM, cols), F32)
        for w in range(WIN_KEYS // Q_BLOCK):
            acc = acc + _dot(vwt_ref[h, i + w], p_w[w * Q_BLOCK:(w + 1) * Q_BLOCK, :])
        o_w.append(acc)

    for h in heads:
        gt = jax.nn.sigmoid(gt_ref[h])
        outs = []
        for g in range(NSA_GROUP):
            sl = slice(g * Q_BLOCK, (g + 1) * Q_BLOCK)
            outs.append(gt[3 * g:3 * g + 1, :] * o_c[h][:, sl] + gt[3 * g + 1:3 * g + 2, :] * o_s[h][:, sl]
                        + gt[3 * g + 2:3 * g + 3, :] * o_w[h][:, sl])
        for pair in range(NSA_GROUP // 2):
            both = jnp.concatenate([outs[2 * pair], outs[2 * pair + 1]], axis=0)
            lo = (h * NSA_GROUP // 2 + pair) * LANE
            o_ref[:, lo:lo + LANE] = both.T


def nsa_prompt(qt, kc, vct, ks, vst, kw, vwt, gt, mt, t):
    nq = t // Q_BLOCK
    head = lambda a: pl.BlockSpec((P_HEADS,) + a.shape[1:], lambda h, i: (h,) + (0,) * (a.ndim - 1),
                                  pipeline_mode=pl.Buffered(1))
    width = P_HEADS * NSA_GROUP * HEAD_DIM
    return pl.pallas_call(
        _nsa_prompt_kernel,
        grid=(NSA_KV_HEADS // P_HEADS, nq),
        in_specs=[pl.BlockSpec((width, Q_BLOCK), lambda h, i: (h, i)),
                  head(kc), head(vct), head(ks), head(vst), head(kw), head(vwt),
                  pl.BlockSpec((P_HEADS, 16, Q_BLOCK), lambda h, i: (h, 0, i)),
                  pl.BlockSpec(mt.shape, lambda h, i: (0, 0))],
        out_specs=pl.BlockSpec((Q_BLOCK, width), lambda h, i: (i, h)),
        out_shape=jax.ShapeDtypeStruct((t, NSA_WIDTH), F32),
        scratch_shapes=[pltpu.VMEM((P_HEADS, mt.shape[0], Q_BLOCK), F32),
                        pltpu.VMEM((SEL_TILE, NSA_GROUP * Q_BLOCK), F32)],
        compiler_params=_cparams(("parallel", "arbitrary")),
        name="nsa_p",
    )(qt, kc, vct, ks, vst, kw, vwt, gt, mt)


def _cmp_to_sel_t(n_rows, n_blk, n_blk_pad):
    cs = (np.arange(n_rows)[None, :] - 1) * CMP_STRIDE
    js = np.arange(n_blk_pad)[:, None] * SEL_BLOCK
    m = (cs < js + SEL_BLOCK) & (cs + CMP_LEN > js) & (np.arange(n_rows)[None, :] >= 1) & (np.arange(n_blk_pad)[:, None] < n_blk)
    return jnp.asarray(m.astype(np.float32), dtype=BF16)


S_PAGES = 16
S_COLS = NSA_HEADS * 8


def _softmax_rows(s, mask):
    s = jnp.where(mask, s, NEG)
    m = jnp.max(s, axis=1, keepdims=True)
    p = jnp.where(mask, jnp.exp2(s - m), 0.0)
    return p / jnp.maximum(jnp.sum(p, axis=1, keepdims=True), TINY)


def _flash_rows(state, s, v_t):
    m, l, acc = state
    m_new = jnp.maximum(m, jnp.max(s, axis=1, keepdims=True))
    p = jnp.exp2(s - m_new)
    alpha = jnp.exp2(m - m_new)
    return m_new, alpha * l + jnp.sum(p, axis=1, keepdims=True), alpha * acc + _dot_nt(p.astype(BF16), v_t)


def _nsa_sample_kernel(n_steps, pt_ref, *refs):
    pages = refs[:S_PAGES]
    (qb_ref, kct_ref, vct_ref, cw_ref, nw_ref, ns_ref, g_ref, mt_ref, gsum_ref, emat_ref, o_ref,
     sel_ref, m_ref, l_ref, acc_ref, oc_ref, ow_ref) = refs[S_PAGES:]
    s_id = pl.program_id(1)
    past = n_steps * S_PAGES * PAGE
    qb = qb_ref[0]
    pos_q = past + lax.broadcasted_iota(jnp.int32, (S_COLS, 1), 0) % 8
    blocks_per_step = S_PAGES * PAGE // SEL_BLOCK

    def block_bias(grp, n_keys):
        flags = jnp.concatenate([grp, jnp.zeros((LANE - grp.shape[0], S_COLS), F32)], axis=0).T
        return _dot(jnp.where(flags > 0.5, 0.0, NEG).astype(BF16), emat_ref[:, :n_keys])

    @pl.when(s_id == 0)
    def _():
        n_ent = kct_ref.shape[2]
        s = _dot(qb, kct_ref[0].astype(BF16))
        ent = lax.broadcasted_iota(jnp.int32, (1, n_ent), 1)
        p_c = _softmax_rows(s, (ent >= 1) & (ent * CMP_STRIDE + (CMP_LEN - CMP_STRIDE - 1) <= pos_q))
        oc_ref[...] = _dot_nt(p_c.astype(BF16), vct_ref[0].astype(BF16))
        imp = _split_dot(mt_ref[...], p_c.T)
        imp = _split_dot_r(imp, gsum_ref[...])
        pos_row = past + lax.broadcasted_iota(jnp.int32, (1, S_COLS), 1) % 8
        sel_ref[...] = _top_blocks([imp], pos_row // SEL_BLOCK)[0]

        wb = cw_ref.shape[2]
        k_t = jnp.concatenate([cw_ref[0, :HALF_ROW, :], nw_ref[0, :HALF_ROW, :]], axis=1).astype(BF16)
        v_t = jnp.concatenate([cw_ref[0, HALF_ROW:, :], nw_ref[0, HALF_ROW:, :]], axis=1).astype(BF16)
        w_pos = past - wb + lax.broadcasted_iota(jnp.int32, (1, wb + PAGE), 1)
        p_w = _softmax_rows(_dot(qb, k_t), (w_pos <= pos_q) & (w_pos > pos_q - WINDOW) & (w_pos >= 0))
        ow_ref[...] = _dot_nt(p_w.astype(BF16), v_t)

        nblk0 = past // SEL_BLOCK
        key_pos = past + lax.broadcasted_iota(jnp.int32, (1, PAGE), 1)
        s = (_dot(qb, ns_ref[0, :HALF_ROW, :].astype(BF16)) + block_bias(sel_ref[nblk0:nblk0 + 8, :], PAGE)
             + jnp.where(key_pos <= pos_q, 0.0, NEG))
        init = (jnp.full((S_COLS, 1), NEG, F32), jnp.zeros((S_COLS, 1), F32), jnp.zeros((S_COLS, HALF_ROW), F32))
        m_ref[...], l_ref[...], acc_ref[...] = _flash_rows(init, s, ns_ref[0, HALF_ROW:, :].astype(BF16))

    grp = sel_ref[pl.ds(pl.multiple_of(s_id * blocks_per_step, blocks_per_step), blocks_per_step), :]
    k_t = jnp.concatenate([pg[0, :HALF_ROW, :] for pg in pages], axis=1).astype(BF16)
    v_t = jnp.concatenate([pg[0, HALF_ROW:, :] for pg in pages], axis=1).astype(BF16)
    s = _dot(qb, k_t) + block_bias(grp, S_PAGES * PAGE)
    st = _flash_rows((m_ref[...], l_ref[...], acc_ref[...]), s, v_t)
    m_ref[...], l_ref[...], acc_ref[...] = st

    @pl.when(s_id == n_steps - 1)
    def _():
        g = jax.nn.sigmoid(g_ref[0])
        o_s = st[2] / jnp.maximum(st[1], TINY)
        o_ref[0] = g[:, 0:1] * oc_ref[...] + g[:, 1:2] * o_s + g[:, 2:3] * ow_ref[...]


def nsa_sample(pool_t, page_table, qb, kct, vct, cache_wt, new_w, new_s, g, mt, gsum, emat):
    b, n_pages = page_table.shape
    steps = n_pages // S_PAGES
    page_spec = lambda k: pl.BlockSpec((1, KV_ROW, PAGE), lambda bi, si, pt, k=k: (pt[bi, si * S_PAGES + k], 0, 0))
    per_b = lambda a: pl.BlockSpec((1,) + a.shape[1:], lambda bi, si, pt: (bi,) + (0,) * (a.ndim - 1))
    const = lambda a: pl.BlockSpec(a.shape, lambda bi, si, pt: (0,) * a.ndim)
    grid_spec = pltpu.PrefetchScalarGridSpec(
        num_scalar_prefetch=1,
        grid=(b, steps),
        in_specs=[page_spec(k) for k in range(S_PAGES)] + [
            per_b(qb), per_b(kct), per_b(vct), per_b(cache_wt), per_b(new_w), per_b(new_s), per_b(g),
            const(mt), const(gsum), const(emat)],
        out_specs=pl.BlockSpec((1, S_COLS, HALF_ROW), lambda bi, si, pt: (bi, 0, 0)),
        scratch_shapes=[pltpu.VMEM((mt.shape[0], S_COLS), F32), pltpu.VMEM((S_COLS, 1), F32), pltpu.VMEM((S_COLS, 1), F32),
                        pltpu.VMEM((S_COLS, HALF_ROW), F32), pltpu.VMEM((S_COLS, HALF_ROW), F32),
                        pltpu.VMEM((S_COLS, HALF_ROW), F32)],
    )
    return pl.pallas_call(
        functools.partial(_nsa_sample_kernel, steps),
        grid_spec=grid_spec,
        out_shape=jax.ShapeDtypeStruct((b, S_COLS, HALF_ROW), F32),
        compiler_params=_cparams(("parallel", "arbitrary")),
        name="nsa_s",
    )(page_table, *([pool_t] * S_PAGES), qb, kct, vct, cache_wt, new_w, new_s, g, mt, gsum, emat)


GLA_SUB = 16


def _gla_head(q, k, v, cum, state):
    c = q.shape[0]
    sub = min(GLA_SUB, c)
    lane = lax.broadcasted_iota(jnp.int32, (1, LANE), 1)
    t_sub = lax.broadcasted_iota(jnp.int32, (sub, 1), 0)
    row_pad = lambda a: jnp.concatenate([a, jnp.zeros((LANE - c, a.shape[1]), F32)], axis=0).astype(BF16)
    v_pad = row_pad(v)
    o = _dot((q * jnp.exp(cum)).astype(BF16), state.astype(BF16))
    blocks = []
    for r0 in range(0, c, sub):
        q_i, cum_i = q[r0:r0 + sub], cum[r0:r0 + sub]
        if r0 == 0:
            att_i = jnp.zeros((sub, LANE), F32)
        else:
            base = cum[r0 - 1:r0]
            q_dec = (q_i * jnp.exp(cum_i - base)).astype(BF16)
            k_dec = row_pad(k * jnp.exp(jnp.minimum(base - cum, 0.0)))
            att_i = jnp.where(lane < r0, _dot_nt(q_dec, k_dec), 0.0)
        for s in range(r0, r0 + sub):
            decay = jnp.exp(jnp.where(t_sub >= s - r0, cum_i - cum[s:s + 1], NEG))
            column = jnp.sum(q_i * k[s:s + 1] * decay, axis=-1, keepdims=True)
            att_i = jnp.where(lane == s, column, att_i)
        blocks.append(att_i)
    att = jnp.concatenate(blocks, axis=0)
    o = o + _dot(att.astype(BF16), v_pad)

    c_last = cum[c - 1:c]
    k_end = jnp.concatenate([k * jnp.exp(c_last - cum), jnp.zeros((LANE - c, GLA_DK), F32)], axis=0)
    eye = lax.broadcasted_iota(jnp.int32, (GLA_DK, GLA_DK), 0) == lax.broadcasted_iota(jnp.int32, (GLA_DK, GLA_DK), 1)
    decay_col = jnp.sum(jnp.where(eye, jnp.exp(c_last), 0.0), axis=1, keepdims=True)
    return o, decay_col * state + _dot(k_end.T.astype(BF16), v_pad)


def _gla_kernel(q_ref, k_ref, v_ref, a_ref, z_ref, wa_ref, ba_ref, nw_ref, s0_ref, o_ref, so_ref, s_ref):
    c = q_ref.shape[0]
    ci = pl.program_id(1)

    @pl.when(ci == 0)
    def _():
        s_ref[...] = s0_ref[0]

    pre = _dot(a_ref[...].astype(BF16), wa_ref[...].astype(BF16)) + ba_ref[...]
    log_a = (jnp.minimum(pre, 0.0) - jnp.log1p(jnp.exp(-jnp.abs(pre)))) / GLA_TAU
    t_idx = lax.broadcasted_iota(jnp.int32, (c, 1), 0)
    cum = log_a
    sh = 1
    while sh < c:
        cum = cum + jnp.where(t_idx >= sh, pltpu.roll(cum, sh, 0), 0.0)
        sh *= 2
    for h in range(GLA_HEADS):
        ks = slice(h * GLA_DK, (h + 1) * GLA_DK)
        vs = slice(h * GLA_DV, (h + 1) * GLA_DV)
        o, new_state = _gla_head(q_ref[:, ks] * (GLA_DK ** -0.5), k_ref[:, ks], v_ref[:, vs], cum[:, ks], s_ref[h])
        s_ref[h] = new_state
        y = o * lax.rsqrt(jnp.mean(o * o, axis=-1, keepdims=True) + EPS) * nw_ref[...]
        o_ref[:, vs] = y * _silu(z_ref[:, vs])

    @pl.when(ci == pl.num_programs(1) - 1)
    def _():
        so_ref[0] = s_ref[...]


def gla(proj, w_a2p, b_a, gla_norm_w, s0, n_seq, chunk):
    rows = proj.shape[0]
    n_chunk = rows // (n_seq * chunk)
    kw, vw = GLA_HEADS * GLA_DK, GLA_HEADS * GLA_DV
    rows_at = lambda width, col: pl.BlockSpec((chunk, width), lambda b, c: (b * n_chunk + c, col // width))
    state_spec = pl.BlockSpec((1, GLA_HEADS, GLA_DK, GLA_DV), lambda b, c: (b, 0, 0, 0))
    return pl.pallas_call(
        _gla_kernel,
        grid=(n_seq, n_chunk),
        in_specs=[rows_at(kw, C_QG), rows_at(kw, C_KG), rows_at(vw, C_VG), rows_at(LANE, C_AG), rows_at(vw, C_ZG),
                  pl.BlockSpec((LANE, kw), lambda b, c: (0, 0)),
                  pl.BlockSpec((1, kw), lambda b, c: (0, 0)),
                  pl.BlockSpec((1, GLA_DV), lambda b, c: (0, 0)),
                  state_spec],
        out_specs=[pl.BlockSpec((chunk, vw), lambda b, c: (b * n_chunk + c, 0)), state_spec],
        out_shape=[jax.ShapeDtypeStruct((rows, vw), F32),
                   jax.ShapeDtypeStruct((n_seq, GLA_HEADS, GLA_DK, GLA_DV), F32)],
        scratch_shapes=[pltpu.VMEM((GLA_HEADS, GLA_DK, GLA_DV), F32)],
        compiler_params=_cparams(("parallel", "arbitrary")),
        name="gla",
    )(proj, proj, proj, proj, proj, w_a2p, b_a.reshape(1, -1), gla_norm_w.reshape(1, -1), s0)


def _out_kernel(on_ref, zn_ref, og_ref, mn_ref, mg_ref, x_ref, gate_ref, wn_ref, wg_ref, wo_ref, fw_ref, y_ref):
    o_nsa = (on_ref[...] * _silu(zn_ref[...])).astype(BF16)
    merged = (jax.nn.sigmoid(mn_ref[...]) * _dot(o_nsa, wn_ref[...])
              + jax.nn.sigmoid(mg_ref[...]) * _dot(og_ref[...].astype(BF16), wg_ref[...]))
    y = x_ref[...] + gate_ref[...] * _dot(merged.astype(BF16), wo_ref[...])
    y_ref[...] = y * lax.rsqrt(jnp.mean(y * y, axis=-1, keepdims=True) + EPS) * fw_ref[...]


def out_proj(o_nsa, o_gla, proj, x, gate, w_o_nsa, w_o_gla, w_out, final_norm_w, tm):
    rows = x.shape[0]
    per_row = gate.shape[0] != 1
    gate_spec = pl.BlockSpec((tm, D_MODEL), lambda i: (i, 0)) if per_row else pl.BlockSpec((1, D_MODEL), lambda i: (0, 0))
    resident = lambda a: pl.BlockSpec(a.shape, lambda i: (0, 0), pipeline_mode=pl.Buffered(1))
    return pl.pallas_call(
        _out_kernel,
        grid=(rows // tm,),
        in_specs=[pl.BlockSpec((tm, NSA_WIDTH), lambda i: (i, 0)),
                  pl.BlockSpec((tm, NSA_WIDTH), lambda i: (i, C_ZN // NSA_WIDTH)),
                  pl.BlockSpec((tm, NSA_WIDTH), lambda i: (i, 0)),
                  pl.BlockSpec((tm, D_MODEL), lambda i: (i, C_MN // D_MODEL)),
                  pl.BlockSpec((tm, D_MODEL), lambda i: (i, C_MG // D_MODEL)),
                  pl.BlockSpec((tm, D_MODEL), lambda i: (i, 0)),
                  gate_spec, resident(w_o_nsa), resident(w_o_gla), resident(w_out),
                  pl.BlockSpec((1, D_MODEL), lambda i: (0, 0))],
        out_specs=pl.BlockSpec((tm, D_MODEL), lambda i: (i, 0)),
        out_shape=jax.ShapeDtypeStruct((rows, D_MODEL), F32),
        compiler_params=_cparams(("parallel",)),
        name="outproj",
    )(o_nsa, proj, o_gla, proj, proj, x, gate, w_o_nsa, w_o_gla, w_out, final_norm_w.reshape(1, D_MODEL))


def _feature_major(a):
    lead = a.shape[:-4]
    n = len(lead)
    a = jnp.transpose(a, tuple(range(n)) + (n + 1, n + 2, n + 3, n))
    return a.reshape(lead + (KV_ROW, a.shape[-1]))


def _token_major(a_t, lead):
    rows = a_t.shape[-1]
    a = a_t.reshape(a_t.shape[:-2] + (2, NSA_KV_HEADS, HEAD_DIM, rows))
    n = a.ndim - 4
    a = jnp.transpose(a, tuple(range(n)) + (n + 3, n, n + 1, n + 2))
    return a.reshape(lead + (rows, 2, NSA_KV_HEADS, HEAD_DIM))


def kernel(x_prompt, x_sample, cache_kv_cmp, cache_kv_sel, cache_kv_win, state_gla, page_table, c_prompt, c_sample, norm_w, w_ada, b_ada, w_in, cmp_pos, cmp_w1, cmp_b1, cmp_w2, cmp_b2, w_a2, b_a, gla_norm_w, w_o_nsa, w_o_gla, w_out, final_norm_w):
    assert x_prompt.shape[0] == 1 and norm_w.shape[0] == 1, "one prompt sequence, one layer"
    t_p = x_prompt.shape[1]
    b_s, t_s = x_sample.shape[:2]
    past = page_table.shape[1] * PAGE
    wb = cache_kv_win.shape[2]
    assert t_s == 8 and wb == WINDOW and past % (S_PAGES * PAGE) == 0 and t_p % SEL_TILE == 0
    rows_s = b_s * t_s

    c_rows = jnp.zeros((40, D_MODEL), F32).at[0:1].set(c_prompt).at[1:1 + b_s].set(c_sample)
    mod = ada_mod(c_rows, w_ada[0], b_ada[0])
    shift, scale, gate = mod[:, :D_MODEL], mod[:, D_MODEL:2 * D_MODEL], mod[:, 2 * D_MODEL:]
    per_row = lambda a: jnp.repeat(a[1:1 + b_s], t_s, axis=0)

    w_t = jnp.transpose(w_in[0])
    xp = x_prompt.reshape(t_p, D_MODEL)
    xs = x_sample.reshape(rows_s, D_MODEL)
    proj_p = in_proj(xp, scale[0:1], shift[0:1], norm_w[0], w_t, RM_OFFSETS, 1024, RM_TILE, False)
    projt_p = in_proj(xp, scale[0:1], shift[0:1], norm_w[0], w_t, FM_OFFSETS, 512, FM_TILE, True)
    proj_s = in_proj(xs, per_row(scale), per_row(shift), norm_w[0], w_t, RM_OFFSETS, rows_s, RM_TILE, False)
    projt_s = in_proj(xs, per_row(scale), per_row(shift), norm_w[0], w_t, FM_OFFSETS, rows_s, FM_TILE, True)

    cos_p, sin_p = _rope_tables(jnp.arange(t_p, dtype=jnp.int32))
    cos_s, sin_s = _rope_tables(jnp.tile(past + jnp.arange(t_s, dtype=jnp.int32), b_s))
    qt_p, kvc_p, kvs_p, kvw_p, ks_p, kw_p, vst_p, vwt_p = rope_stage(projt_p, cos_p, sin_p, 512, True)
    qt_s, kvc_s, kvs_s, kvw_s = rope_stage(projt_s, cos_s, sin_s, rows_s, False)

    pb = pos_bias(cmp_pos[0], cmp_w1[0], cmp_b1[0])
    cmp_consts = _compress_weights(cmp_w1[0], cmp_w2[0], cmp_b2[0])
    perm, w1t, w2k, w2t, b2k, b2c = cmp_consts
    ident = jnp.arange(t_p // PAGE, dtype=jnp.int32)[None, :]
    pages_p = jnp.transpose(kvc_p.reshape(KV_ROW, t_p // PAGE, PAGE), (1, 0, 2))
    kc_p, _, vct_p = compress(pages_p, ident, perm, w1t, pb, w2k, w2t, b2k, b2c)
    pool_c = _feature_major(cache_kv_cmp[0])
    _, kct_s, vct_s = compress(pool_c, page_table, perm, w1t, pb, w2k, w2t, b2k, b2c)

    n_ent = kc_p.shape[1]
    kc_h = jnp.transpose(kc_p[0].reshape(n_ent, NSA_KV_HEADS, HEAD_DIM), (1, 0, 2)).astype(BF16)
    vct_h = vct_p[0].reshape(NSA_KV_HEADS, HEAD_DIM, n_ent).astype(BF16)
    kw_h = jnp.pad(kw_p, ((0, 0), (WINDOW, 0), (0, 0)))
    vwt_h = jnp.pad(vwt_p, ((0, 0), (WINDOW // Q_BLOCK, 0), (0, 0), (0, 0)))
    g_p = projt_p[R_GN:R_GN + 48].reshape(NSA_KV_HEADS, 12, t_p)
    g_p = jnp.pad(g_p, ((0, 0), (0, 4), (0, 0)))
    mt_p = _cmp_to_sel_t(n_ent, t_p // SEL_BLOCK, t_p // SEL_BLOCK)
    o_nsa_p = nsa_prompt(qt_p, kc_h, vct_h, ks_p, vst_p, kw_h, vwt_h, g_p, mt_p, t_p)

    q5 = qt_s.reshape(NSA_KV_HEADS, NSA_GROUP, HEAD_DIM, b_s, t_s)
    q_c = jnp.transpose(q5, (3, 0, 1, 4, 2)).reshape(b_s, NSA_KV_HEADS, NSA_GROUP * t_s, HEAD_DIM)
    eye = jnp.eye(NSA_KV_HEADS, dtype=BF16)
    qb = jnp.einsum('bhcd,hk->bhckd', q_c, eye).reshape(b_s, S_COLS, HALF_ROW)
    new_keys = lambda a_t: jnp.pad(jnp.transpose(a_t.reshape(KV_ROW, b_s, t_s), (1, 0, 2)), ((0, 0), (0, 0), (0, PAGE - t_s)))
    g_s = projt_s[R_GN:R_GN + 48].reshape(NSA_KV_HEADS, NSA_GROUP, 3, b_s, t_s)
    g_s = jnp.transpose(g_s, (3, 0, 1, 4, 2)).reshape(b_s, S_COLS, 3)
    g_s = jnp.pad(g_s, ((0, 0), (0, 0), (0, 5)))
    n_blk_s = -(-(past + t_s) // SEL_BLOCK)
    mt_s = _cmp_to_sel_t(kct_s.shape[2], n_blk_s, -(-n_blk_s // 8) * 8)
    col = np.arange(S_COLS)
    gsum = jnp.asarray(((col[:, None] // 32 == col[None, :] // 32) & (col[:, None] % 8 == col[None, :] % 8)).astype(np.float32), dtype=BF16)
    emat = jnp.asarray((np.arange(S_PAGES * PAGE)[None, :] // SEL_BLOCK == np.arange(LANE)[:, None]).astype(np.float32), dtype=BF16)
    cache_wt = _feature_major(cache_kv_win[0])
    o_all = nsa_sample(_feature_major(cache_kv_sel[0]), page_table, qb, kct_s, vct_s, cache_wt,
                       new_keys(kvw_s), new_keys(kvs_s), g_s, mt_s, gsum, emat)
    o6 = o_all.reshape(b_s, NSA_KV_HEADS, NSA_GROUP, t_s, NSA_KV_HEADS, HEAD_DIM)
    o_nsa_s = jnp.stack([o6[:, h, :, :, h, :] for h in range(NSA_KV_HEADS)], axis=1)
    o_nsa_s = jnp.transpose(o_nsa_s, (0, 3, 1, 2, 4)).reshape(rows_s, NSA_WIDTH)

    w_a2p = jnp.zeros((LANE, GLA_HEADS * GLA_DK), F32).at[:GLA_RANK].set(w_a2[0])
    s0_p = jnp.zeros((1, GLA_HEADS, GLA_DK, GLA_DV), F32)
    o_gla_p, st_p = gla(proj_p, w_a2p, b_a[0], gla_norm_w[0], s0_p, 1, GLA_CHUNK)
    o_gla_s, st_s = gla(proj_s, w_a2p, b_a[0], gla_norm_w[0], state_gla[0], b_s, t_s)

    wn, wg, wo = w_o_nsa[0].astype(BF16), w_o_gla[0].astype(BF16), w_out[0].astype(BF16)
    y_p = out_proj(o_nsa_p, o_gla_p, proj_p, xp, gate[0:1], wn, wg, wo, final_norm_w, 256)
    y_s = out_proj(o_nsa_s, o_gla_s, proj_s, xs, per_row(gate), wn, wg, wo, final_norm_w, rows_s)

    sample_rows = lambda a_t: _token_major(jnp.transpose(a_t.reshape(KV_ROW, b_s, t_s), (1, 0, 2)), (1, b_s))
    win_t = jnp.concatenate([cache_wt, jnp.transpose(kvw_s.reshape(KV_ROW, b_s, t_s), (1, 0, 2))], axis=2)[:, :, t_s:]
    n_win = min(WINDOW, t_p)
    return (y_p.reshape(x_prompt.shape), y_s.reshape(x_sample.shape),
            _token_major(kvc_p, (1, 1)), sample_rows(kvc_s), _token_major(kvs_p, (1, 1)), sample_rows(kvs_s),
            _token_major(kvw_p[:, t_p - n_win:], (1, 1)), _token_major(win_t, (1, b_s)),
            st_p[None], st_s[None])
```

```python
import functools

import jax
import jax.numpy as jnp
import numpy as np
from jax import lax
from jax.experimental import pallas as pl
from jax.experimental.pallas import tpu as pltpu

F32 = jnp.float32
BF16 = jnp.bfloat16

D_MODEL = 2048
HEAD_DIM = 64
NSA_HEADS = 16
NSA_KV_HEADS = 4
NSA_GROUP = 4
NSA_WIDTH = 1024
HALF_ROW = NSA_KV_HEADS * HEAD_DIM
KV_ROW = 2 * HALF_ROW
CMP_LEN = 32
CMP_STRIDE = 16
CMP_HIDDEN = 128
SEL_BLOCK = 64
N_SELECT = 16
N_LOCAL = 2
WINDOW = 512
Q_BLOCK = 128
PAGE = 128
GLA_HEADS = 4
GLA_DK = 128
GLA_DV = 256
GLA_RANK = 16
GLA_TAU = 16.0
GLA_CHUNK = 64
ROPE_THETA = 10000.0
EPS = 1e-6
NEG = -1e30
BIG = 1e30
TINY = 1e-30
REMOVED = -3e38
LOG2E = 1.4426950408889634
SEL_TILE = 512
V_AUG = HEAD_DIM + 16

LANE = 128
VMEM_LIMIT = 48 * 1024 * 1024

(W_Q, W_KV, W_GN, W_ZN, W_QG, W_KG, W_VG, W_AG, W_ZG, W_MN, W_MG) = (
    0, 1024, 2560, 2608, 3632, 4144, 4656, 5680, 5696, 6720, 8768)
RM_TILE = 1024
W_ALIGN = 16
C_MN, C_MG, C_ZN, C_VG, C_ZG, C_QG, C_KG, C_AG = 0, 2048, 4096, 5120, 6144, 7168, 7680, 8192
assert W_KG == W_QG + RM_TILE // 2
RM_SOURCES = ((W_MN, 2), (W_MG, 2), (W_ZN, 1), (W_VG, 1), (W_ZG, 1), (W_QG, 1), (W_AG, 1))
RM_OFFSETS = tuple(start + RM_TILE * k for start, tiles in RM_SOURCES for k in range(tiles))
RM_COLS = RM_TILE * len(RM_OFFSETS)
R_Q, R_KV, R_GN = W_Q, W_KV, W_GN
FM_TILE = 1344
FM_ROWS = 2 * FM_TILE
FM_OFFSETS = (0, FM_TILE)


def _cparams(sem):
    return pltpu.CompilerParams(dimension_semantics=sem, vmem_limit_bytes=VMEM_LIMIT)


def _dot(a, b):
    return jnp.dot(a, b, preferred_element_type=F32)


def _dot_nt(a, b):
    return lax.dot_general(a, b, (((1,), (1,)), ((), ())), preferred_element_type=F32)


def _silu(x):
    return x * jax.nn.sigmoid(x)


def _ada_kernel(c_ref, w_ref, b_ref, o_ref):
    o_ref[...] = _dot(c_ref[...].astype(BF16), w_ref[...].astype(BF16)) + b_ref[...]


def ada_mod(c_rows, w_ada, b_ada):
    rows, tn = c_rows.shape[0], 512
    n = w_ada.shape[1]
    return pl.pallas_call(
        _ada_kernel,
        grid=(n // tn,),
        in_specs=[pl.BlockSpec((rows, D_MODEL), lambda j: (0, 0)),
                  pl.BlockSpec((D_MODEL, tn), lambda j: (0, j)),
                  pl.BlockSpec((1, tn), lambda j: (0, j))],
        out_specs=pl.BlockSpec((rows, tn), lambda j: (0, j)),
        out_shape=jax.ShapeDtypeStruct((rows, n), F32),
        compiler_params=_cparams(("parallel",)),
        name="ada",
    )(c_rows, w_ada, b_ada.reshape(1, n))


def _norm_kernel(x_ref, sc_ref, sh_ref, nw_ref, h_ref):
    x = x_ref[...]
    y = x * lax.rsqrt(jnp.mean(x * x, axis=-1, keepdims=True) + EPS) * nw_ref[...]
    h_ref[...] = (y * (1.0 + sc_ref[...]) + sh_ref[...]).astype(BF16)


def modulated_norm(x, scale, shift, norm_w, tm):
    rows = x.shape[0]
    per_row = scale.shape[0] != 1
    mod_spec = pl.BlockSpec((tm, D_MODEL), lambda i: (i, 0)) if per_row else pl.BlockSpec((1, D_MODEL), lambda i: (0, 0))
    return pl.pallas_call(
        _norm_kernel,
        grid=(rows // tm,),
        in_specs=[pl.BlockSpec((tm, D_MODEL), lambda i: (i, 0)), mod_spec, mod_spec,
                  pl.BlockSpec((1, D_MODEL), lambda i: (0, 0))],
        out_specs=pl.BlockSpec((tm, D_MODEL), lambda i: (i, 0)),
        out_shape=jax.ShapeDtypeStruct((rows, D_MODEL), BF16),
        compiler_params=_cparams(("parallel",)),
        name="norm",
    )(x, scale, shift, norm_w.reshape(1, D_MODEL))


def _inproj_rm_kernel(off_ref, h_ref, w_ref, o_ref, wb_ref):
    @pl.when(pl.program_id(1) == 0)
    def _():
        wb_ref[...] = w_ref[...].astype(BF16)

    o_ref[...] = _dot_nt(h_ref[...], wb_ref[...])


def _inproj_fm_kernel(off_ref, h_ref, w_ref, o_ref, wb_ref):
    @pl.when(pl.program_id(1) == 0)
    def _():
        wb_ref[...] = w_ref[...].astype(BF16)

    o_ref[...] = _dot_nt(wb_ref[...], h_ref[...])


def in_proj(h, w_t, row_offsets, tm, tn, feature_major):
    rows, n = h.shape[0], len(row_offsets) * tn
    if feature_major:
        body, out_spec, out_shape = _inproj_fm_kernel, pl.BlockSpec((tn, tm), lambda j, i, off: (j, i)), (n, rows)
    else:
        body, out_spec, out_shape = _inproj_rm_kernel, pl.BlockSpec((tm, tn), lambda j, i, off: (i, j)), (rows, n)
    grid_spec = pltpu.PrefetchScalarGridSpec(
        num_scalar_prefetch=1,
        grid=(len(row_offsets), rows // tm),
        in_specs=[pl.BlockSpec((tm, D_MODEL), lambda j, i, off: (i, 0)),
                  pl.BlockSpec((pl.Element(tn), pl.Element(D_MODEL)), lambda j, i, off: (off[j] * W_ALIGN, 0))],
        out_specs=out_spec,
        scratch_shapes=[pltpu.VMEM((tn, D_MODEL), BF16)],
    )
    return pl.pallas_call(
        body,
        grid_spec=grid_spec,
        out_shape=jax.ShapeDtypeStruct(out_shape, F32),
        compiler_params=_cparams(("parallel", "arbitrary")),
        name="inproj_fm" if feature_major else "inproj_rm",
    )(jnp.asarray([o // W_ALIGN for o in row_offsets], jnp.int32), h, w_t)


def _rope_kernel(q_ref, c_ref, s_ref, w_ref, cos_ref, sin_ref, qo_ref, co_ref, so_ref, wo_ref, *tile_refs):
    cos, sin = cos_ref[...], sin_ref[...]
    hh = HEAD_DIM // 2
    tr = cos.shape[1]

    def rot(src, head):
        x1 = src[head * HEAD_DIM:head * HEAD_DIM + hh, :]
        x2 = src[head * HEAD_DIM + hh:(head + 1) * HEAD_DIM, :]
        return x1 * cos - x2 * sin, x2 * cos + x1 * sin

    q_scale = HEAD_DIM ** -0.5 * LOG2E
    for head in range(NSA_HEADS):
        o1, o2 = rot(q_ref, head)
        qo_ref[head * HEAD_DIM:head * HEAD_DIM + hh, :] = (o1 * q_scale).astype(BF16)
        qo_ref[head * HEAD_DIM + hh:(head + 1) * HEAD_DIM, :] = (o2 * q_scale).astype(BF16)
    for src, dst in ((c_ref, co_ref), (s_ref, so_ref), (w_ref, wo_ref)):
        for head in range(NSA_KV_HEADS):
            o1, o2 = rot(src, head)
            dst[head * HEAD_DIM:head * HEAD_DIM + hh, :] = o1
            dst[head * HEAD_DIM + hh:(head + 1) * HEAD_DIM, :] = o2
        dst[HALF_ROW:, :] = src[HALF_ROW:, :]
    if tile_refs:
        ks_ref, kw_ref, vs_ref, vw_ref = tile_refs
        lane = lax.broadcasted_iota(jnp.int32, (1, LANE), 1)
        r = lax.broadcasted_iota(jnp.int32, (tr, 1), 0)
        onehot = jnp.where(lane - HEAD_DIM == (r // SEL_BLOCK) % (SEL_TILE // SEL_BLOCK), 1.0, 0.0)
        ones_row = jnp.where(lax.broadcasted_iota(jnp.int32, (V_AUG - HEAD_DIM, SEL_TILE), 0) == 0, 1.0, 0.0)
        for pair in range(NSA_KV_HEADS // 2):
            k_pair = so_ref[pair * LANE:(pair + 1) * LANE, :].T
            ks_ref[2 * pair] = jnp.where(lane < HEAD_DIM, k_pair, onehot).astype(BF16)
            ks_ref[2 * pair + 1] = jnp.where(lane < HEAD_DIM, pltpu.roll(k_pair, HEAD_DIM, 1), onehot).astype(BF16)
            kw_pair = wo_ref[pair * LANE:(pair + 1) * LANE, :].T.astype(BF16)
            kw_ref[2 * pair] = kw_pair[:, :HEAD_DIM]
            kw_ref[2 * pair + 1] = kw_pair[:, HEAD_DIM:]
        for head in range(NSA_KV_HEADS):
            rows = slice(HALF_ROW + head * HEAD_DIM, HALF_ROW + (head + 1) * HEAD_DIM)
            v = so_ref[rows, :]
            for w in range(tr // SEL_TILE):
                vs_ref[head, w] = jnp.concatenate([v[:, w * SEL_TILE:(w + 1) * SEL_TILE], ones_row], axis=0).astype(BF16)
            v = wo_ref[rows, :].astype(BF16)
            for w in range(tr // Q_BLOCK):
                vw_ref[head, w] = v[:, w * Q_BLOCK:(w + 1) * Q_BLOCK]


def rope_stage(proj_t, cos_t, sin_t, tr, with_tiles):
    tok = proj_t.shape[1]
    kv_spec = lambda k: pl.BlockSpec((KV_ROW, tr), lambda i, k=k: (R_KV // KV_ROW + k, i))
    out_kv = jax.ShapeDtypeStruct((KV_ROW, tok), F32)
    tab = pl.BlockSpec((HEAD_DIM // 2, tr), lambda i: (0, i))
    out_specs = [pl.BlockSpec((NSA_WIDTH, tr), lambda i: (0, i))] + [pl.BlockSpec((KV_ROW, tr), lambda i: (0, i))] * 3
    out_shape = [jax.ShapeDtypeStruct((NSA_WIDTH, tok), BF16), out_kv, out_kv, out_kv]
    if with_tiles:
        k_rows = lambda width: jax.ShapeDtypeStruct((NSA_KV_HEADS, tok, width), BF16)
        k_spec = lambda width: pl.BlockSpec((NSA_KV_HEADS, tr, width), lambda i: (0, i, 0))
        v_tiles = lambda rows, tile: jax.ShapeDtypeStruct((NSA_KV_HEADS, tok // tile, rows, tile), BF16)
        v_spec = lambda rows, tile: pl.BlockSpec((NSA_KV_HEADS, tr // tile, rows, tile), lambda i: (0, i, 0, 0))
        out_specs += [k_spec(LANE), k_spec(HEAD_DIM), v_spec(V_AUG, SEL_TILE), v_spec(HEAD_DIM, Q_BLOCK)]
        out_shape += [k_rows(LANE), k_rows(HEAD_DIM), v_tiles(V_AUG, SEL_TILE), v_tiles(HEAD_DIM, Q_BLOCK)]
    return pl.pallas_call(
        _rope_kernel,
        grid=(tok // tr,),
        in_specs=[pl.BlockSpec((NSA_WIDTH, tr), lambda i: (R_Q // NSA_WIDTH, i)), kv_spec(0), kv_spec(1), kv_spec(2), tab, tab],
        out_specs=out_specs,
        out_shape=out_shape,
        compiler_params=_cparams(("parallel",)),
        name="rope",
    )(proj_t, proj_t, proj_t, proj_t, cos_t, sin_t)


def _rope_tables(pos):
    half = HEAD_DIM // 2
    inv = ROPE_THETA ** (-jnp.arange(half, dtype=F32) / half)
    ang = inv[:, None] * pos.astype(F32)[None, :]
    return jnp.cos(ang), jnp.sin(ang)


def _posbias_kernel(p_ref, w_ref, b_ref, o_ref):
    for x in range(2):
        o_ref[x] = _dot(p_ref[x], w_ref[x]) + b_ref[x]


def pos_bias(cmp_pos, cmp_w1, cmp_b1):
    k = CMP_LEN * HEAD_DIM
    pos = jnp.zeros((2, 8, k), F32).at[:, 0].set(cmp_pos.reshape(2, k))
    out = pl.pallas_call(
        _posbias_kernel,
        out_shape=jax.ShapeDtypeStruct((2, 8, CMP_HIDDEN), F32),
        compiler_params=pltpu.CompilerParams(vmem_limit_bytes=VMEM_LIMIT),
        name="posbias",
    )(pos, cmp_w1.reshape(2, k, CMP_HIDDEN), cmp_b1.reshape(2, 1, CMP_HIDDEN))
    return out[:, 0]


CMP_PAGES = 16
CMP_CHUNKS = CMP_PAGES * PAGE // CMP_STRIDE
CHUNKS_PER_PAGE = PAGE // CMP_STRIDE


def _compress_kernel(pt_ref, *refs):
    pages = refs[:CMP_PAGES]
    perm_ref, w1_ref, pb_ref, w2_ref, w2t_ref, b2_ref, b2c_ref, k_ref, kt_ref, vt_ref, carry_ref = refs[CMP_PAGES:]
    s = pl.program_id(1)

    @pl.when(s == 0)
    def _():
        carry_ref[...] = jnp.zeros_like(carry_ref)

    n = CMP_CHUNKS
    perm = perm_ref[...]
    rows_by_p = [_dot_nt(perm, pg[0].astype(BF16)) for pg in pages]
    row0 = lax.broadcasted_iota(jnp.int32, (n, 1), 0) == 0
    for t in range(KV_ROW // LANE):
        x = t // 2
        sl = slice(t * LANE, (t + 1) * LANE)
        acc = jnp.zeros((n, 4 * CMP_HIDDEN), F32)
        for pp in range(CMP_STRIDE // 2):
            parts = []
            for p in (2 * pp, 2 * pp + 1):
                parts.append(jnp.concatenate(
                    [r[p * CHUNKS_PER_PAGE:(p + 1) * CHUNKS_PER_PAGE, sl] for r in rows_by_p], axis=0))
            lhs = jnp.concatenate(parts, axis=1).astype(BF16)
            acc = acc + _dot(lhs, w1_ref[x, pp])
        hid = []
        for hh in range(2):
            part0 = acc[:, hh * 256:hh * 256 + CMP_HIDDEN]
            part1 = acc[:, hh * 256 + CMP_HIDDEN:(hh + 1) * 256]
            csl = slice((t * 2 + hh) * CMP_HIDDEN, (t * 2 + hh + 1) * CMP_HIDDEN)
            prev = jnp.where(row0, carry_ref[0:1, csl], pltpu.roll(part0, 1, 0))
            carry_ref[0:1, csl] = part0[n - 1:n, :]
            hid.append(_silu(prev + part1 + pb_ref[x:x + 1, :]))
        hid = jnp.concatenate(hid, axis=1).astype(BF16)
        out_t = _dot_nt(w2t_ref[x], hid) + b2c_ref[x]
        rows = slice((t % 2) * LANE, (t % 2 + 1) * LANE)
        if x == 0:
            k_ref[0, :, sl] = _dot(hid, w2_ref[...]) + b2_ref[...]
            kt_ref[0, rows, :] = out_t
        else:
            vt_ref[0, rows, :] = out_t


def compress(pool_t, page_table, perm, w1t, pb, w2k, w2t, b2k, b2c):
    b, n_pages = page_table.shape
    steps = n_pages // CMP_PAGES
    n_blk = n_pages * CHUNKS_PER_PAGE
    page_spec = lambda k: pl.BlockSpec((1, KV_ROW, PAGE), lambda bi, si, pt, k=k: (pt[bi, si * CMP_PAGES + k], 0, 0))
    const = lambda a: pl.BlockSpec(a.shape, lambda bi, si, pt: (0,) * a.ndim)
    consts = (perm, w1t, pb, w2k, w2t, b2k, b2c)
    fm_spec = pl.BlockSpec((1, HALF_ROW, CMP_CHUNKS), lambda bi, si, pt: (bi, 0, si))
    fm_shape = jax.ShapeDtypeStruct((b, HALF_ROW, n_blk), F32)
    grid_spec = pltpu.PrefetchScalarGridSpec(
        num_scalar_prefetch=1,
        grid=(b, steps),
        in_specs=[page_spec(k) for k in range(CMP_PAGES)] + [const(a) for a in consts],
        out_specs=[pl.BlockSpec((1, CMP_CHUNKS, HALF_ROW), lambda bi, si, pt: (bi, si, 0)), fm_spec, fm_spec],
        scratch_shapes=[pltpu.VMEM((8, 8 * CMP_HIDDEN), F32)],
    )
    return pl.pallas_call(
        _compress_kernel,
        grid_spec=grid_spec,
        out_shape=[jax.ShapeDtypeStruct((b, n_blk, HALF_ROW), F32), fm_shape, fm_shape],
        compiler_params=_cparams(("parallel", "arbitrary")),
        name="compress",
    )(page_table, *([pool_t] * CMP_PAGES), *consts)


def _compress_weights(cmp_w1, cmp_w2, cmp_b2):
    w1 = cmp_w1.reshape(2, 2, CMP_STRIDE // 2, 2, HEAD_DIM, CMP_HIDDEN)
    w1 = jnp.transpose(w1, (0, 2, 3, 4, 1, 5))
    eye = jnp.eye(2, dtype=F32)
    w1t = jnp.einsum('xqpdje,hk->xqphdkje', w1, eye).reshape(2, CMP_STRIDE // 2, 256, 512).astype(BF16)
    w2bd = jnp.einsum('xed,hk->xhekd', cmp_w2, eye).reshape(2, 256, LANE).astype(BF16)
    b2t = jnp.concatenate([cmp_b2, cmp_b2], axis=1)
    r = np.arange(PAGE)
    perm = np.zeros((PAGE, PAGE), np.float32)
    perm[(r % CMP_STRIDE) * CHUNKS_PER_PAGE + r // CMP_STRIDE, r] = 1.0
    return (jnp.asarray(perm, dtype=BF16), w1t, w2bd[0], jnp.transpose(w2bd, (0, 2, 1)), b2t[0:1], b2t.reshape(2, LANE, 1))


def _masked_exp0(s, mask):
    s = jnp.where(mask, s, NEG)
    m = jnp.max(s, axis=0, keepdims=True)
    p = jnp.where(mask, jnp.exp2(s - m), 0.0)
    return p, jnp.maximum(jnp.sum(p, axis=0, keepdims=True), TINY)


def _softmax0(s, mask):
    p, denom = _masked_exp0(s, mask)
    return p / denom


def _split_dot(a, x):
    hi = x.astype(BF16)
    lo = (x - hi.astype(F32)).astype(BF16)
    return _dot(a, hi) + _dot(a, lo)


def _split_dot_r(x, a):
    hi = x.astype(BF16)
    lo = (x - hi.astype(F32)).astype(BF16)
    return _dot(hi, a) + _dot(lo, a)


def _top_blocks(imps, cur):
    blk = lax.broadcasted_iota(jnp.int32, (imps[0].shape[0], 1), 0)
    forced = (blk == 0) | ((blk <= cur) & (blk > cur - N_LOCAL))
    imps = tuple(jnp.where(forced, REMOVED, jnp.where(blk > cur, -BIG, imp)) for imp in imps)
    picked = jnp.where(forced, 1.0, 0.0)
    blk_f = blk.astype(F32)

    def pick(_, carry):
        out = []
        for imp, sel in carry:
            mx = jnp.max(imp, axis=0, keepdims=True)
            first = jnp.min(jnp.where(imp == mx, blk_f, 1e9), axis=0, keepdims=True)
            hit = blk_f == first
            out.append((jnp.where(hit, REMOVED, imp), jnp.where(hit, 1.0, sel)))
        return tuple(out)

    final = lax.fori_loop(0, N_SELECT - 1 - N_LOCAL, pick, tuple((imp, picked) for imp in imps))
    return [sel for _, sel in final]


def _flash_update_biased(state, s, v_aug):
    m, acc = state
    m_new = jnp.maximum(m, jnp.max(s, axis=0, keepdims=True))
    p = jnp.exp2(s - m_new)
    acc = jnp.exp2(m - m_new) * acc + _dot(v_aug, p.astype(BF16))
    return m_new, acc


def _cmp_mask(n_rows, pos_q):
    r = lax.broadcasted_iota(jnp.int32, (n_rows, 1), 0)
    return (r >= 1) & (r * CMP_STRIDE + (CMP_LEN - CMP_STRIDE - 1) <= pos_q)


WIN_KEYS = WINDOW + Q_BLOCK
P_HEADS = 2
BLOCKS_PER_TILE = SEL_TILE // SEL_BLOCK


def _nsa_prompt_kernel(qt_ref, kc_ref, vct_ref, ks_ref, vst_ref, kw_ref, vwt_ref, gt_ref, mt_ref, o_ref, sel_ref, s_ref):
    i = pl.program_id(1)
    cols = NSA_GROUP * Q_BLOCK
    lane = lax.broadcasted_iota(jnp.int32, (1, Q_BLOCK), 1)
    pos_q = i * Q_BLOCK + lane
    tile4 = lambda a: jnp.concatenate([a] * NSA_GROUP, axis=1)
    heads = range(P_HEADS)
    q_ts = []
    for h in heads:
        q_blk = qt_ref[h * NSA_GROUP * HEAD_DIM:(h + 1) * NSA_GROUP * HEAD_DIM, :]
        q_ts.append(jnp.concatenate([q_blk[g * HEAD_DIM:(g + 1) * HEAD_DIM, :] for g in range(NSA_GROUP)], axis=1))

    nc = kc_ref.shape[1]
    s_cmp = [_dot(kc_ref[h], q_ts[h]) for h in heads]
    s_win = [_dot(kw_ref[h, pl.ds(pl.multiple_of(i * Q_BLOCK, Q_BLOCK), WIN_KEYS), :], q_ts[h]) for h in heads]

    mask_c = tile4(_cmp_mask(nc, pos_q))
    o_c, imps = [], []
    for h in heads:
        p_c = _softmax0(s_cmp[h], mask_c)
        o_c.append(_dot(vct_ref[h], p_c.astype(BF16)))
        pg = p_c[:, 0:Q_BLOCK]
        for g in range(1, NSA_GROUP):
            pg = pg + p_c[:, g * Q_BLOCK:(g + 1) * Q_BLOCK]
        imps.append(_split_dot(mt_ref[...], pg))
    for h, sel in enumerate(_top_blocks(imps, pos_q // SEL_BLOCK)):
        sel_ref[h] = sel

    zeros_q = jnp.zeros((LANE - HEAD_DIM - 16, cols), BF16)

    def q_aug(h, j):
        grp = sel_ref[h, pl.ds(pl.multiple_of(j * BLOCKS_PER_TILE, BLOCKS_PER_TILE), BLOCKS_PER_TILE), :]
        bias = jnp.concatenate([jnp.where(grp > 0.5, 0.0, NEG), jnp.zeros_like(grp)], axis=0)
        return jnp.concatenate([q_ts[h], tile4(bias).astype(BF16), zeros_q], axis=0)

    def scores(h, j):
        return _dot(ks_ref[h, pl.ds(pl.multiple_of(j * SEL_TILE, SEL_TILE), SEL_TILE), :], q_aug(h, j))

    def sel_body(j, states):
        st0, st1 = states
        s1 = scores(1, j)
        st0 = _flash_update_biased(st0, s_ref[...], vst_ref[0, j])
        s_ref[...] = scores(0, j + 1)
        st1 = _flash_update_biased(st1, s1, vst_ref[1, j])
        return st0, st1

    init = tuple((jnp.full((1, cols), NEG, F32), jnp.zeros((V_AUG, cols), F32)) for _ in heads)
    j_diag = (i * Q_BLOCK) // SEL_TILE
    s_ref[...] = scores(0, 0)
    states = lax.fori_loop(0, j_diag, sel_body, init)
    key_pos = j_diag * SEL_TILE + lax.broadcasted_iota(jnp.int32, (SEL_TILE, 1), 0)
    causal = tile4(jnp.where(key_pos <= pos_q, 0.0, NEG))
    last = (s_ref[...], scores(1, j_diag))
    o_s = []
    for h in heads:
        _, acc = _flash_update_biased(states[h], last[h] + causal, vst_ref[h, j_diag])
        o_s.append(acc[:HEAD_DIM] / jnp.maximum(acc[HEAD_DIM:HEAD_DIM + 1], TINY))

    w_pos = i * Q_BLOCK - WINDOW + lax.broadcasted_iota(jnp.int32, (WIN_KEYS, 1), 0)
    mask_w = tile4((w_pos <= pos_q) & (w_pos > pos_q - WINDOW) & (w_pos >= 0))
    o_w = []
    for h in heads:
        p_w, denom = _masked_exp0(s_win[h], mask_w)
        p_w = p_w.astype(BF16)
        acc = jnp.zeros((HEAD_DIM, cols), F32)
        for w in range(WIN_KEYS // Q_BLOCK):
            acc = acc + _dot(vwt_ref[h, i + w], p_w[w * Q_BLOCK:(w + 1) * Q_BLOCK, :])
        o_w.append(acc / denom)

    for h in heads:
        gt = jax.nn.sigmoid(gt_ref[h])
        outs = []
        for g in range(NSA_GROUP):
            sl = slice(g * Q_BLOCK, (g + 1) * Q_BLOCK)
            outs.append(gt[3 * g:3 * g + 1, :] * o_c[h][:, sl] + gt[3 * g + 1:3 * g + 2, :] * o_s[h][:, sl]
                        + gt[3 * g + 2:3 * g + 3, :] * o_w[h][:, sl])
        for pair in range(NSA_GROUP // 2):
            both = jnp.concatenate([outs[2 * pair], outs[2 * pair + 1]], axis=0)
            lo = (h * NSA_GROUP // 2 + pair) * LANE
            o_ref[:, lo:lo + LANE] = both.T


def nsa_prompt(qt, kc, vct, ks, vst, kw, vwt, gt, mt, t):
    nq = t // Q_BLOCK
    head = lambda a: pl.BlockSpec((P_HEADS,) + a.shape[1:], lambda h, i: (h,) + (0,) * (a.ndim - 1),
                                  pipeline_mode=pl.Buffered(1))
    width = P_HEADS * NSA_GROUP * HEAD_DIM
    return pl.pallas_call(
        _nsa_prompt_kernel,
        grid=(NSA_KV_HEADS // P_HEADS, nq),
        in_specs=[pl.BlockSpec((width, Q_BLOCK), lambda h, i: (h, i)),
                  head(kc), head(vct), head(ks), head(vst), head(kw), head(vwt),
                  pl.BlockSpec((P_HEADS, 16, Q_BLOCK), lambda h, i: (h, 0, i)),
                  pl.BlockSpec(mt.shape, lambda h, i: (0, 0))],
        out_specs=pl.BlockSpec((Q_BLOCK, width), lambda h, i: (i, h)),
        out_shape=jax.ShapeDtypeStruct((t, NSA_WIDTH), F32),
        scratch_shapes=[pltpu.VMEM((P_HEADS, mt.shape[0], Q_BLOCK), F32),
                        pltpu.VMEM((SEL_TILE, NSA_GROUP * Q_BLOCK), F32)],
        compiler_params=_cparams(("parallel", "arbitrary")),
        name="nsa_p",
    )(qt, kc, vct, ks, vst, kw, vwt, gt, mt)


def _cmp_to_sel_t(n_rows, n_blk, n_blk_pad):
    cs = (np.arange(n_rows)[None, :] - 1) * CMP_STRIDE
    js = np.arange(n_blk_pad)[:, None] * SEL_BLOCK
    m = (cs < js + SEL_BLOCK) & (cs + CMP_LEN > js) & (np.arange(n_rows)[None, :] >= 1) & (np.arange(n_blk_pad)[:, None] < n_blk)
    return jnp.asarray(m.astype(np.float32), dtype=BF16)


S_PAGES = 16
S_COLS = NSA_HEADS * 8


def _softmax_rows(s, mask):
    s = jnp.where(mask, s, NEG)
    m = jnp.max(s, axis=1, keepdims=True)
    p = jnp.where(mask, jnp.exp2(s - m), 0.0)
    return p / jnp.maximum(jnp.sum(p, axis=1, keepdims=True), TINY)


def _flash_rows(state, s, v_t):
    m, l, acc = state
    m_new = jnp.maximum(m, jnp.max(s, axis=1, keepdims=True))
    p = jnp.exp2(s - m_new)
    alpha = jnp.exp2(m - m_new)
    return m_new, alpha * l + jnp.sum(p, axis=1, keepdims=True), alpha * acc + _dot_nt(p.astype(BF16), v_t)


def _nsa_sample_kernel(n_steps, pt_ref, *refs):
    pages = refs[:S_PAGES]
    (qb_ref, kct_ref, vct_ref, cw_ref, nw_ref, ns_ref, g_ref, mt_ref, gsum_ref, emat_ref, o_ref,
     sel_ref, m_ref, l_ref, acc_ref, oc_ref, ow_ref) = refs[S_PAGES:]
    s_id = pl.program_id(1)
    past = n_steps * S_PAGES * PAGE
    qb = qb_ref[0]
    pos_q = past + lax.broadcasted_iota(jnp.int32, (S_COLS, 1), 0) % 8
    blocks_per_step = S_PAGES * PAGE // SEL_BLOCK

    def block_bias(grp, n_keys):
        flags = jnp.concatenate([grp, jnp.zeros((LANE - grp.shape[0], S_COLS), F32)], axis=0).T
        return _dot(jnp.where(flags > 0.5, 0.0, NEG).astype(BF16), emat_ref[:, :n_keys])

    @pl.when(s_id == 0)
    def _():
        n_ent = kct_ref.shape[2]
        s = _dot(qb, kct_ref[0].astype(BF16))
        ent = lax.broadcasted_iota(jnp.int32, (1, n_ent), 1)
        p_c = _softmax_rows(s, (ent >= 1) & (ent * CMP_STRIDE + (CMP_LEN - CMP_STRIDE - 1) <= pos_q))
        oc_ref[...] = _dot_nt(p_c.astype(BF16), vct_ref[0].astype(BF16))
        imp = _split_dot(mt_ref[...], p_c.T)
        imp = _split_dot_r(imp, gsum_ref[...])
        pos_row = past + lax.broadcasted_iota(jnp.int32, (1, S_COLS), 1) % 8
        sel_ref[...] = _top_blocks([imp], pos_row // SEL_BLOCK)[0]

        wb = cw_ref.shape[2]
        k_t = jnp.concatenate([cw_ref[0, :HALF_ROW, :], nw_ref[0, :HALF_ROW, :]], axis=1).astype(BF16)
        v_t = jnp.concatenate([cw_ref[0, HALF_ROW:, :], nw_ref[0, HALF_ROW:, :]], axis=1).astype(BF16)
        w_pos = past - wb + lax.broadcasted_iota(jnp.int32, (1, wb + PAGE), 1)
        p_w = _softmax_rows(_dot(qb, k_t), (w_pos <= pos_q) & (w_pos > pos_q - WINDOW) & (w_pos >= 0))
        ow_ref[...] = _dot_nt(p_w.astype(BF16), v_t)

        nblk0 = past // SEL_BLOCK
        key_pos = past + lax.broadcasted_iota(jnp.int32, (1, PAGE), 1)
        s = (_dot(qb, ns_ref[0, :HALF_ROW, :].astype(BF16)) + block_bias(sel_ref[nblk0:nblk0 + 8, :], PAGE)
             + jnp.where(key_pos <= pos_q, 0.0, NEG))
        init = (jnp.full((S_COLS, 1), NEG, F32), jnp.zeros((S_COLS, 1), F32), jnp.zeros((S_COLS, HALF_ROW), F32))
        m_ref[...], l_ref[...], acc_ref[...] = _flash_rows(init, s, ns_ref[0, HALF_ROW:, :].astype(BF16))

    grp = sel_ref[pl.ds(pl.multiple_of(s_id * blocks_per_step, blocks_per_step), blocks_per_step), :]
    k_t = jnp.concatenate([pg[0, :HALF_ROW, :] for pg in pages], axis=1).astype(BF16)
    v_t = jnp.concatenate([pg[0, HALF_ROW:, :] for pg in pages], axis=1).astype(BF16)
    s = _dot(qb, k_t) + block_bias(grp, S_PAGES * PAGE)
    st = _flash_rows((m_ref[...], l_ref[...], acc_ref[...]), s, v_t)
    m_ref[...], l_ref[...], acc_ref[...] = st

    @pl.when(s_id == n_steps - 1)
    def _():
        g = jax.nn.sigmoid(g_ref[0])
        o_s = st[2] / jnp.maximum(st[1], TINY)
        o_ref[0] = g[:, 0:1] * oc_ref[...] + g[:, 1:2] * o_s + g[:, 2:3] * ow_ref[...]


def nsa_sample(pool_t, page_table, qb, kct, vct, cache_wt, new_w, new_s, g, mt, gsum, emat):
    b, n_pages = page_table.shape
    steps = n_pages // S_PAGES
    page_spec = lambda k: pl.BlockSpec((1, KV_ROW, PAGE), lambda bi, si, pt, k=k: (pt[bi, si * S_PAGES + k], 0, 0))
    per_b = lambda a: pl.BlockSpec((1,) + a.shape[1:], lambda bi, si, pt: (bi,) + (0,) * (a.ndim - 1))
    const = lambda a: pl.BlockSpec(a.shape, lambda bi, si, pt: (0,) * a.ndim)
    grid_spec = pltpu.PrefetchScalarGridSpec(
        num_scalar_prefetch=1,
        grid=(b, steps),
        in_specs=[page_spec(k) for k in range(S_PAGES)] + [
            per_b(qb), per_b(kct), per_b(vct), per_b(cache_wt), per_b(new_w), per_b(new_s), per_b(g),
            const(mt), const(gsum), const(emat)],
        out_specs=pl.BlockSpec((1, S_COLS, HALF_ROW), lambda bi, si, pt: (bi, 0, 0)),
        scratch_shapes=[pltpu.VMEM((mt.shape[0], S_COLS), F32), pltpu.VMEM((S_COLS, 1), F32), pltpu.VMEM((S_COLS, 1), F32),
                        pltpu.VMEM((S_COLS, HALF_ROW), F32), pltpu.VMEM((S_COLS, HALF_ROW), F32),
                        pltpu.VMEM((S_COLS, HALF_ROW), F32)],
    )
    return pl.pallas_call(
        functools.partial(_nsa_sample_kernel, steps),
        grid_spec=grid_spec,
        out_shape=jax.ShapeDtypeStruct((b, S_COLS, HALF_ROW), F32),
        compiler_params=_cparams(("parallel", "arbitrary")),
        name="nsa_s",
    )(page_table, *([pool_t] * S_PAGES), qb, kct, vct, cache_wt, new_w, new_s, g, mt, gsum, emat)


GLA_SUB = 16


def _gla_head(q, k, v, cum, state):
    c = q.shape[0]
    sub = min(GLA_SUB, c)
    lane = lax.broadcasted_iota(jnp.int32, (1, LANE), 1)
    t_sub = lax.broadcasted_iota(jnp.int32, (sub, 1), 0)
    row_pad = lambda a: jnp.concatenate([a, jnp.zeros((LANE - c, a.shape[1]), F32)], axis=0).astype(BF16)
    v_pad = row_pad(v)
    o = _dot((q * jnp.exp(cum)).astype(BF16), state.astype(BF16))
    blocks = []
    for r0 in range(0, c, sub):
        q_i, cum_i = q[r0:r0 + sub], cum[r0:r0 + sub]
        if r0 == 0:
            att_i = jnp.zeros((sub, LANE), F32)
        else:
            base = cum[r0 - 1:r0]
            q_dec = (q_i * jnp.exp(cum_i - base)).astype(BF16)
            k_dec = row_pad(k * jnp.exp(jnp.minimum(base - cum, 0.0)))
            att_i = jnp.where(lane < r0, _dot_nt(q_dec, k_dec), 0.0)
        for s in range(r0, r0 + sub):
            decay = jnp.exp(jnp.where(t_sub >= s - r0, cum_i - cum[s:s + 1], NEG))
            column = jnp.sum(q_i * k[s:s + 1] * decay, axis=-1, keepdims=True)
            att_i = jnp.where(lane == s, column, att_i)
        blocks.append(att_i)
    att = jnp.concatenate(blocks, axis=0)
    o = o + _dot(att.astype(BF16), v_pad)

    c_last = cum[c - 1:c]
    k_end = jnp.concatenate([k * jnp.exp(c_last - cum), jnp.zeros((LANE - c, GLA_DK), F32)], axis=0)
    eye = lax.broadcasted_iota(jnp.int32, (GLA_DK, GLA_DK), 0) == lax.broadcasted_iota(jnp.int32, (GLA_DK, GLA_DK), 1)
    decay_col = jnp.sum(jnp.where(eye, jnp.exp(c_last), 0.0), axis=1, keepdims=True)
    return o, decay_col * state + _dot(k_end.T.astype(BF16), v_pad)


def _gla_kernel(q_ref, k_ref, v_ref, a_ref, z_ref, wa_ref, ba_ref, nw_ref, s0_ref, o_ref, so_ref, s_ref):
    c = q_ref.shape[0]
    ci = pl.program_id(1)

    @pl.when(ci == 0)
    def _():
        s_ref[...] = s0_ref[0]

    pre = _dot(a_ref[...].astype(BF16), wa_ref[...].astype(BF16)) + ba_ref[...]
    log_a = (jnp.minimum(pre, 0.0) - jnp.log1p(jnp.exp(-jnp.abs(pre)))) / GLA_TAU
    t_idx = lax.broadcasted_iota(jnp.int32, (c, 1), 0)
    cum = log_a
    sh = 1
    while sh < c:
        cum = cum + jnp.where(t_idx >= sh, pltpu.roll(cum, sh, 0), 0.0)
        sh *= 2
    for h in range(GLA_HEADS):
        ks = slice(h * GLA_DK, (h + 1) * GLA_DK)
        vs = slice(h * GLA_DV, (h + 1) * GLA_DV)
        o, new_state = _gla_head(q_ref[:, ks] * (GLA_DK ** -0.5), k_ref[:, ks], v_ref[:, vs], cum[:, ks], s_ref[h])
        s_ref[h] = new_state
        y = o * lax.rsqrt(jnp.mean(o * o, axis=-1, keepdims=True) + EPS) * nw_ref[...]
        o_ref[:, vs] = y * _silu(z_ref[:, vs])

    @pl.when(ci == pl.num_programs(1) - 1)
    def _():
        so_ref[0] = s_ref[...]


def gla(proj, w_a2p, b_a, gla_norm_w, s0, n_seq, chunk):
    rows = proj.shape[0]
    n_chunk = rows // (n_seq * chunk)
    kw, vw = GLA_HEADS * GLA_DK, GLA_HEADS * GLA_DV
    rows_at = lambda width, col: pl.BlockSpec((chunk, width), lambda b, c: (b * n_chunk + c, col // width))
    state_spec = pl.BlockSpec((1, GLA_HEADS, GLA_DK, GLA_DV), lambda b, c: (b, 0, 0, 0))
    return pl.pallas_call(
        _gla_kernel,
        grid=(n_seq, n_chunk),
        in_specs=[rows_at(kw, C_QG), rows_at(kw, C_KG), rows_at(vw, C_VG), rows_at(LANE, C_AG), rows_at(vw, C_ZG),
                  pl.BlockSpec((LANE, kw), lambda b, c: (0, 0)),
                  pl.BlockSpec((1, kw), lambda b, c: (0, 0)),
                  pl.BlockSpec((1, GLA_DV), lambda b, c: (0, 0)),
                  state_spec],
        out_specs=[pl.BlockSpec((chunk, vw), lambda b, c: (b * n_chunk + c, 0)), state_spec],
        out_shape=[jax.ShapeDtypeStruct((rows, vw), F32),
                   jax.ShapeDtypeStruct((n_seq, GLA_HEADS, GLA_DK, GLA_DV), F32)],
        scratch_shapes=[pltpu.VMEM((GLA_HEADS, GLA_DK, GLA_DV), F32)],
        compiler_params=_cparams(("parallel", "arbitrary")),
        name="gla",
    )(proj, proj, proj, proj, proj, w_a2p, b_a.reshape(1, -1), gla_norm_w.reshape(1, -1), s0)


def _out_kernel(on_ref, zn_ref, og_ref, mn_ref, mg_ref, x_ref, gate_ref, wn_ref, wg_ref, wo_ref, fw_ref, y_ref):
    o_nsa = (on_ref[...] * _silu(zn_ref[...])).astype(BF16)
    merged = (jax.nn.sigmoid(mn_ref[...]) * _dot(o_nsa, wn_ref[...])
              + jax.nn.sigmoid(mg_ref[...]) * _dot(og_ref[...].astype(BF16), wg_ref[...]))
    y = x_ref[...] + gate_ref[...] * _dot(merged.astype(BF16), wo_ref[...])
    y_ref[...] = y * lax.rsqrt(jnp.mean(y * y, axis=-1, keepdims=True) + EPS) * fw_ref[...]


def out_proj(o_nsa, o_gla, proj, x, gate, w_o_nsa, w_o_gla, w_out, final_norm_w, tm):
    rows = x.shape[0]
    per_row = gate.shape[0] != 1
    gate_spec = pl.BlockSpec((tm, D_MODEL), lambda i: (i, 0)) if per_row else pl.BlockSpec((1, D_MODEL), lambda i: (0, 0))
    resident = lambda a: pl.BlockSpec(a.shape, lambda i: (0, 0), pipeline_mode=pl.Buffered(1))
    return pl.pallas_call(
        _out_kernel,
        grid=(rows // tm,),
        in_specs=[pl.BlockSpec((tm, NSA_WIDTH), lambda i: (i, 0)),
                  pl.BlockSpec((tm, NSA_WIDTH), lambda i: (i, C_ZN // NSA_WIDTH)),
                  pl.BlockSpec((tm, NSA_WIDTH), lambda i: (i, 0)),
                  pl.BlockSpec((tm, D_MODEL), lambda i: (i, C_MN // D_MODEL)),
                  pl.BlockSpec((tm, D_MODEL), lambda i: (i, C_MG // D_MODEL)),
                  pl.BlockSpec((tm, D_MODEL), lambda i: (i, 0)),
                  gate_spec, resident(w_o_nsa), resident(w_o_gla), resident(w_out),
                  pl.BlockSpec((1, D_MODEL), lambda i: (0, 0))],
        out_specs=pl.BlockSpec((tm, D_MODEL), lambda i: (i, 0)),
        out_shape=jax.ShapeDtypeStruct((rows, D_MODEL), F32),
        compiler_params=_cparams(("parallel",)),
        name="outproj",
    )(o_nsa, proj, o_gla, proj, proj, x, gate, w_o_nsa, w_o_gla, w_out, final_norm_w.reshape(1, D_MODEL))


def _feature_major(a):
    lead = a.shape[:-4]
    n = len(lead)
    a = jnp.transpose(a, tuple(range(n)) + (n + 1, n + 2, n + 3, n))
    return a.reshape(lead + (KV_ROW, a.shape[-1]))


def _token_major(a_t, lead):
    rows = a_t.shape[-1]
    a = a_t.reshape(a_t.shape[:-2] + (2, NSA_KV_HEADS, HEAD_DIM, rows))
    n = a.ndim - 4
    a = jnp.transpose(a, tuple(range(n)) + (n + 3, n, n + 1, n + 2))
    return a.reshape(lead + (rows, 2, NSA_KV_HEADS, HEAD_DIM))


def kernel(x_prompt, x_sample, cache_kv_cmp, cache_kv_sel, cache_kv_win, state_gla, page_table, c_prompt, c_sample, norm_w, w_ada, b_ada, w_in, cmp_pos, cmp_w1, cmp_b1, cmp_w2, cmp_b2, w_a2, b_a, gla_norm_w, w_o_nsa, w_o_gla, w_out, final_norm_w):
    assert x_prompt.shape[0] == 1 and norm_w.shape[0] == 1, "one prompt sequence, one layer"
    t_p = x_prompt.shape[1]
    b_s, t_s = x_sample.shape[:2]
    past = page_table.shape[1] * PAGE
    wb = cache_kv_win.shape[2]
    assert t_s == 8 and wb == WINDOW and past % (S_PAGES * PAGE) == 0 and t_p % SEL_TILE == 0
    rows_s = b_s * t_s

    c_rows = jnp.zeros((40, D_MODEL), F32).at[0:1].set(c_prompt).at[1:1 + b_s].set(c_sample)
    mod = ada_mod(c_rows, w_ada[0], b_ada[0])
    shift, scale, gate = mod[:, :D_MODEL], mod[:, D_MODEL:2 * D_MODEL], mod[:, 2 * D_MODEL:]
    per_row = lambda a: jnp.repeat(a[1:1 + b_s], t_s, axis=0)

    w_t = jnp.transpose(w_in[0])
    xp = x_prompt.reshape(t_p, D_MODEL)
    xs = x_sample.reshape(rows_s, D_MODEL)
    h_p = modulated_norm(xp, scale[0:1], shift[0:1], norm_w[0], 512)
    h_s = modulated_norm(xs, per_row(scale), per_row(shift), norm_w[0], rows_s)
    proj_p = in_proj(h_p, w_t, RM_OFFSETS, 1024, RM_TILE, False)
    projt_p = in_proj(h_p, w_t, FM_OFFSETS, 512, FM_TILE, True)
    proj_s = in_proj(h_s, w_t, RM_OFFSETS, rows_s, RM_TILE, False)
    projt_s = in_proj(h_s, w_t, FM_OFFSETS, rows_s, FM_TILE, True)

    cos_p, sin_p = _rope_tables(jnp.arange(t_p, dtype=jnp.int32))
    cos_s, sin_s = _rope_tables(jnp.tile(past + jnp.arange(t_s, dtype=jnp.int32), b_s))
    qt_p, kvc_p, kvs_p, kvw_p, ks_p, kw_p, vst_p, vwt_p = rope_stage(projt_p, cos_p, sin_p, 512, True)
    qt_s, kvc_s, kvs_s, kvw_s = rope_stage(projt_s, cos_s, sin_s, rows_s, False)

    pb = pos_bias(cmp_pos[0], cmp_w1[0], cmp_b1[0])
    cmp_consts = _compress_weights(cmp_w1[0], cmp_w2[0], cmp_b2[0])
    perm, w1t, w2k, w2t, b2k, b2c = cmp_consts
    ident = jnp.arange(t_p // PAGE, dtype=jnp.int32)[None, :]
    pages_p = jnp.transpose(kvc_p.reshape(KV_ROW, t_p // PAGE, PAGE), (1, 0, 2))
    kc_p, _, vct_p = compress(pages_p, ident, perm, w1t, pb, w2k, w2t, b2k, b2c)
    pool_c = _feature_major(cache_kv_cmp[0])
    _, kct_s, vct_s = compress(pool_c, page_table, perm, w1t, pb, w2k, w2t, b2k, b2c)

    n_ent = kc_p.shape[1]
    kc_h = jnp.transpose(kc_p[0].reshape(n_ent, NSA_KV_HEADS, HEAD_DIM), (1, 0, 2)).astype(BF16)
    vct_h = vct_p[0].reshape(NSA_KV_HEADS, HEAD_DIM, n_ent).astype(BF16)
    kw_h = jnp.pad(kw_p, ((0, 0), (WINDOW, 0), (0, 0)))
    vwt_h = jnp.pad(vwt_p, ((0, 0), (WINDOW // Q_BLOCK, 0), (0, 0), (0, 0)))
    g_p = projt_p[R_GN:R_GN + 48].reshape(NSA_KV_HEADS, 12, t_p)
    g_p = jnp.pad(g_p, ((0, 0), (0, 4), (0, 0)))
    mt_p = _cmp_to_sel_t(n_ent, t_p // SEL_BLOCK, t_p // SEL_BLOCK)
    o_nsa_p = nsa_prompt(qt_p, kc_h, vct_h, ks_p, vst_p, kw_h, vwt_h, g_p, mt_p, t_p)

    q5 = qt_s.reshape(NSA_KV_HEADS, NSA_GROUP, HEAD_DIM, b_s, t_s)
    q_c = jnp.transpose(q5, (3, 0, 1, 4, 2)).reshape(b_s, NSA_KV_HEADS, NSA_GROUP * t_s, HEAD_DIM)
    eye = jnp.eye(NSA_KV_HEADS, dtype=BF16)
    qb = jnp.einsum('bhcd,hk->bhckd', q_c, eye).reshape(b_s, S_COLS, HALF_ROW)
    new_keys = lambda a_t: jnp.pad(jnp.transpose(a_t.reshape(KV_ROW, b_s, t_s), (1, 0, 2)), ((0, 0), (0, 0), (0, PAGE - t_s)))
    g_s = projt_s[R_GN:R_GN + 48].reshape(NSA_KV_HEADS, NSA_GROUP, 3, b_s, t_s)
    g_s = jnp.transpose(g_s, (3, 0, 1, 4, 2)).reshape(b_s, S_COLS, 3)
    g_s = jnp.pad(g_s, ((0, 0), (0, 0), (0, 5)))
    n_blk_s = -(-(past + t_s) // SEL_BLOCK)
    mt_s = _cmp_to_sel_t(kct_s.shape[2], n_blk_s, -(-n_blk_s // 8) * 8)
    col = np.arange(S_COLS)
    gsum = jnp.asarray(((col[:, None] // 32 == col[None, :] // 32) & (col[:, None] % 8 == col[None, :] % 8)).astype(np.float32), dtype=BF16)
    emat = jnp.asarray((np.arange(S_PAGES * PAGE)[None, :] // SEL_BLOCK == np.arange(LANE)[:, None]).astype(np.float32), dtype=BF16)
    cache_wt = _feature_major(cache_kv_win[0])
    o_all = nsa_sample(_feature_major(cache_kv_sel[0]), page_table, qb, kct_s, vct_s, cache_wt,
                       new_keys(kvw_s), new_keys(kvs_s), g_s, mt_s, gsum, emat)
    o6 = o_all.reshape(b_s, NSA_KV_HEADS, NSA_GROUP, t_s, NSA_KV_HEADS, HEAD_DIM)
    o_nsa_s = jnp.stack([o6[:, h, :, :, h, :] for h in range(NSA_KV_HEADS)], axis=1)
    o_nsa_s = jnp.transpose(o_nsa_s, (0, 3, 1, 2, 4)).reshape(rows_s, NSA_WIDTH)

    w_a2p = jnp.zeros((LANE, GLA_HEADS * GLA_DK), F32).at[:GLA_RANK].set(w_a2[0])
    s0_p = jnp.zeros((1, GLA_HEADS, GLA_DK, GLA_DV), F32)
    o_gla_p, st_p = gla(proj_p, w_a2p, b_a[0], gla_norm_w[0], s0_p, 1, GLA_CHUNK)
    o_gla_s, st_s = gla(proj_s, w_a2p, b_a[0], gla_norm_w[0], state_gla[0], b_s, t_s)

    wn, wg, wo = w_o_nsa[0].astype(BF16), w_o_gla[0].astype(BF16), w_out[0].astype(BF16)
    y_p = out_proj(o_nsa_p, o_gla_p, proj_p, xp, gate[0:1], wn, wg, wo, final_norm_w, 256)
    y_s = out_proj(o_nsa_s, o_gla_s, proj_s, xs, per_row(gate), wn, wg, wo, final_norm_w, rows_s)

    sample_rows = lambda a_t: _token_major(jnp.transpose(a_t.reshape(KV_ROW, b_s, t_s), (1, 0, 2)), (1, b_s))
    win_t = jnp.concatenate([cache_wt, jnp.transpose(kvw_s.reshape(KV_ROW, b_s, t_s), (1, 0, 2))], axis=2)[:, :, t_s:]
    n_win = min(WINDOW, t_p)
    return (y_p.reshape(x_prompt.shape), y_s.reshape(x_sample.shape),
            _token_major(kvc_p, (1, 1)), sample_rows(kvc_s), _token_major(kvs_p, (1, 1)), sample_rows(kvs_s),
            _token_major(kvw_p[:, t_p - n_win:], (1, 1)), _token_major(win_t, (1, b_s)),
            st_p[None], st_s[None])
```

```python
import functools

import jax
import jax.numpy as jnp
import numpy as np
from jax import lax
from jax.experimental import pallas as pl
from jax.experimental.pallas import tpu as pltpu

F32 = jnp.float32
BF16 = jnp.bfloat16

D_MODEL = 2048
HEAD_DIM = 64
NSA_HEADS = 16
NSA_KV_HEADS = 4
NSA_GROUP = 4
NSA_WIDTH = 1024
HALF_ROW = NSA_KV_HEADS * HEAD_DIM
KV_ROW = 2 * HALF_ROW
CMP_LEN = 32
CMP_STRIDE = 16
CMP_HIDDEN = 128
SEL_BLOCK = 64
N_SELECT = 16
N_LOCAL = 2
WINDOW = 512
Q_BLOCK = 128
PAGE = 128
GLA_HEADS = 4
GLA_DK = 128
GLA_DV = 256
GLA_RANK = 16
GLA_TAU = 16.0
GLA_CHUNK = 64
ROPE_THETA = 10000.0
EPS = 1e-6
NEG = -1e30
BIG = 1e30
TINY = 1e-30
REMOVED = -3e38
LOG2E = 1.4426950408889634
SEL_TILE = 512
V_AUG = HEAD_DIM + 16

LANE = 128
VMEM_LIMIT = 48 * 1024 * 1024

(W_Q, W_KV, W_GN, W_ZN, W_QG, W_KG, W_VG, W_AG, W_ZG, W_MN, W_MG) = (
    0, 1024, 2560, 2608, 3632, 4144, 4656, 5680, 5696, 6720, 8768)
RM_TILE = 1024
W_ALIGN = 16
C_MN, C_MG, C_ZN, C_VG, C_ZG, C_QG, C_KG, C_AG = 0, 2048, 4096, 5120, 6144, 7168, 7680, 8192
assert W_KG == W_QG + RM_TILE // 2
RM_SOURCES = ((W_MN, 2), (W_MG, 2), (W_ZN, 1), (W_VG, 1), (W_ZG, 1), (W_QG, 1), (W_AG, 1))
RM_OFFSETS = tuple(start + RM_TILE * k for start, tiles in RM_SOURCES for k in range(tiles))
RM_COLS = RM_TILE * len(RM_OFFSETS)
R_Q, R_KV, R_GN = W_Q, W_KV, W_GN
FM_TILE = 1344
FM_ROWS = 2 * FM_TILE
FM_OFFSETS = (0, FM_TILE)


def _cparams(sem):
    return pltpu.CompilerParams(dimension_semantics=sem, vmem_limit_bytes=VMEM_LIMIT)


def _dot(a, b):
    return jnp.dot(a, b, preferred_element_type=F32)


def _dot_nt(a, b):
    return lax.dot_general(a, b, (((1,), (1,)), ((), ())), preferred_element_type=F32)


def _silu(x):
    return x * jax.nn.sigmoid(x)


def _ada_kernel(c_ref, w_ref, b_ref, o_ref):
    o_ref[...] = _dot(c_ref[...].astype(BF16), w_ref[...].astype(BF16)) + b_ref[...]


def ada_mod(c_rows, w_ada, b_ada):
    rows, tn = c_rows.shape[0], 512
    n = w_ada.shape[1]
    return pl.pallas_call(
        _ada_kernel,
        grid=(n // tn,),
        in_specs=[pl.BlockSpec((rows, D_MODEL), lambda j: (0, 0)),
                  pl.BlockSpec((D_MODEL, tn), lambda j: (0, j)),
                  pl.BlockSpec((1, tn), lambda j: (0, j))],
        out_specs=pl.BlockSpec((rows, tn), lambda j: (0, j)),
        out_shape=jax.ShapeDtypeStruct((rows, n), F32),
        compiler_params=_cparams(("parallel",)),
        name="ada",
    )(c_rows, w_ada, b_ada.reshape(1, n))


def _norm_kernel(x_ref, sc_ref, sh_ref, nw_ref, h_ref):
    x = x_ref[...]
    y = x * lax.rsqrt(jnp.mean(x * x, axis=-1, keepdims=True) + EPS) * nw_ref[...]
    h_ref[...] = (y * (1.0 + sc_ref[...]) + sh_ref[...]).astype(BF16)


def modulated_norm(x, scale, shift, norm_w, tm):
    rows = x.shape[0]
    per_row = scale.shape[0] != 1
    mod_spec = pl.BlockSpec((tm, D_MODEL), lambda i: (i, 0)) if per_row else pl.BlockSpec((1, D_MODEL), lambda i: (0, 0))
    return pl.pallas_call(
        _norm_kernel,
        grid=(rows // tm,),
        in_specs=[pl.BlockSpec((tm, D_MODEL), lambda i: (i, 0)), mod_spec, mod_spec,
                  pl.BlockSpec((1, D_MODEL), lambda i: (0, 0))],
        out_specs=pl.BlockSpec((tm, D_MODEL), lambda i: (i, 0)),
        out_shape=jax.ShapeDtypeStruct((rows, D_MODEL), BF16),
        compiler_params=_cparams(("parallel",)),
        name="norm",
    )(x, scale, shift, norm_w.reshape(1, D_MODEL))


def _inproj_rm_kernel(off_ref, h_ref, w_ref, o_ref, wb_ref):
    @pl.when(pl.program_id(1) == 0)
    def _():
        wb_ref[...] = w_ref[...].astype(BF16)

    o_ref[...] = _dot_nt(h_ref[...], wb_ref[...])


def _inproj_fm_kernel(off_ref, h_ref, w_ref, o_ref, wb_ref):
    @pl.when(pl.program_id(1) == 0)
    def _():
        wb_ref[...] = w_ref[...].astype(BF16)

    o_ref[...] = _dot_nt(wb_ref[...], h_ref[...])


def in_proj(h, w_t, row_offsets, tm, tn, feature_major):
    rows, n = h.shape[0], len(row_offsets) * tn
    if feature_major:
        body, out_spec, out_shape = _inproj_fm_kernel, pl.BlockSpec((tn, tm), lambda j, i, off: (j, i)), (n, rows)
    else:
        body, out_spec, out_shape = _inproj_rm_kernel, pl.BlockSpec((tm, tn), lambda j, i, off: (i, j)), (rows, n)
    grid_spec = pltpu.PrefetchScalarGridSpec(
        num_scalar_prefetch=1,
        grid=(len(row_offsets), rows // tm),
        in_specs=[pl.BlockSpec((tm, D_MODEL), lambda j, i, off: (i, 0)),
                  pl.BlockSpec((pl.Element(tn), pl.Element(D_MODEL)), lambda j, i, off: (off[j] * W_ALIGN, 0))],
        out_specs=out_spec,
        scratch_shapes=[pltpu.VMEM((tn, D_MODEL), BF16)],
    )
    return pl.pallas_call(
        body,
        grid_spec=grid_spec,
        out_shape=jax.ShapeDtypeStruct(out_shape, F32),
        compiler_params=_cparams(("parallel", "arbitrary")),
        name="inproj_fm" if feature_major else "inproj_rm",
    )(jnp.asarray([o // W_ALIGN for o in row_offsets], jnp.int32), h, w_t)


def _rope_kernel(q_ref, c_ref, s_ref, w_ref, cos_ref, sin_ref, qo_ref, co_ref, so_ref, wo_ref, *tile_refs):
    cos, sin = cos_ref[...], sin_ref[...]
    hh = HEAD_DIM // 2
    tr = cos.shape[1]

    def rot(src, head):
        x1 = src[head * HEAD_DIM:head * HEAD_DIM + hh, :]
        x2 = src[head * HEAD_DIM + hh:(head + 1) * HEAD_DIM, :]
        return x1 * cos - x2 * sin, x2 * cos + x1 * sin

    q_scale = HEAD_DIM ** -0.5 * LOG2E
    for head in range(NSA_HEADS):
        o1, o2 = rot(q_ref, head)
        qo_ref[head * HEAD_DIM:head * HEAD_DIM + hh, :] = (o1 * q_scale).astype(BF16)
        qo_ref[head * HEAD_DIM + hh:(head + 1) * HEAD_DIM, :] = (o2 * q_scale).astype(BF16)
    for src, dst in ((c_ref, co_ref), (s_ref, so_ref), (w_ref, wo_ref)):
        for head in range(NSA_KV_HEADS):
            o1, o2 = rot(src, head)
            dst[head * HEAD_DIM:head * HEAD_DIM + hh, :] = o1
            dst[head * HEAD_DIM + hh:(head + 1) * HEAD_DIM, :] = o2
        dst[HALF_ROW:, :] = src[HALF_ROW:, :]
    if tile_refs:
        ks_ref, kw_ref, vs_ref, vw_ref = tile_refs
        lane = lax.broadcasted_iota(jnp.int32, (1, LANE), 1)
        r = lax.broadcasted_iota(jnp.int32, (tr, 1), 0)
        onehot = jnp.where(lane - HEAD_DIM == (r // SEL_BLOCK) % (SEL_TILE // SEL_BLOCK), 1.0, 0.0)
        ones_row = jnp.where(lax.broadcasted_iota(jnp.int32, (V_AUG - HEAD_DIM, SEL_TILE), 0) == 0, 1.0, 0.0)
        for pair in range(NSA_KV_HEADS // 2):
            k_pair = so_ref[pair * LANE:(pair + 1) * LANE, :].T
            ks_ref[2 * pair] = jnp.where(lane < HEAD_DIM, k_pair, onehot).astype(BF16)
            ks_ref[2 * pair + 1] = jnp.where(lane < HEAD_DIM, pltpu.roll(k_pair, HEAD_DIM, 1), onehot).astype(BF16)
            kw_pair = wo_ref[pair * LANE:(pair + 1) * LANE, :].T.astype(BF16)
            kw_ref[2 * pair] = kw_pair[:, :HEAD_DIM]
            kw_ref[2 * pair + 1] = kw_pair[:, HEAD_DIM:]
        for head in range(NSA_KV_HEADS):
            rows = slice(HALF_ROW + head * HEAD_DIM, HALF_ROW + (head + 1) * HEAD_DIM)
            v = so_ref[rows, :]
            for w in range(tr // SEL_TILE):
                vs_ref[head, w] = jnp.concatenate([v[:, w * SEL_TILE:(w + 1) * SEL_TILE], ones_row], axis=0).astype(BF16)
            v = wo_ref[rows, :].astype(BF16)
            for w in range(tr // Q_BLOCK):
                vw_ref[head, w] = v[:, w * Q_BLOCK:(w + 1) * Q_BLOCK]


def rope_stage(proj_t, cos_t, sin_t, tr, with_tiles):
    tok = proj_t.shape[1]
    kv_spec = lambda k: pl.BlockSpec((KV_ROW, tr), lambda i, k=k: (R_KV // KV_ROW + k, i))
    out_kv = jax.ShapeDtypeStruct((KV_ROW, tok), F32)
    tab = pl.BlockSpec((HEAD_DIM // 2, tr), lambda i: (0, i))
    out_specs = [pl.BlockSpec((NSA_WIDTH, tr), lambda i: (0, i))] + [pl.BlockSpec((KV_ROW, tr), lambda i: (0, i))] * 3
    out_shape = [jax.ShapeDtypeStruct((NSA_WIDTH, tok), BF16), out_kv, out_kv, out_kv]
    if with_tiles:
        k_rows = lambda width: jax.ShapeDtypeStruct((NSA_KV_HEADS, tok, width), BF16)
        k_spec = lambda width: pl.BlockSpec((NSA_KV_HEADS, tr, width), lambda i: (0, i, 0))
        v_tiles = lambda rows, tile: jax.ShapeDtypeStruct((NSA_KV_HEADS, tok // tile, rows, tile), BF16)
        v_spec = lambda rows, tile: pl.BlockSpec((NSA_KV_HEADS, tr // tile, rows, tile), lambda i: (0, i, 0, 0))
        out_specs += [k_spec(LANE), k_spec(HEAD_DIM), v_spec(V_AUG, SEL_TILE), v_spec(HEAD_DIM, Q_BLOCK)]
        out_shape += [k_rows(LANE), k_rows(HEAD_DIM), v_tiles(V_AUG, SEL_TILE), v_tiles(HEAD_DIM, Q_BLOCK)]
    return pl.pallas_call(
        _rope_kernel,
        grid=(tok // tr,),
        in_specs=[pl.BlockSpec((NSA_WIDTH, tr), lambda i: (R_Q // NSA_WIDTH, i)), kv_spec(0), kv_spec(1), kv_spec(2), tab, tab],
        out_specs=out_specs,
        out_shape=out_shape,
        compiler_params=_cparams(("parallel",)),
        name="rope",
    )(proj_t, proj_t, proj_t, proj_t, cos_t, sin_t)


def _rope_tables(pos):
    half = HEAD_DIM // 2
    inv = ROPE_THETA ** (-jnp.arange(half, dtype=F32) / half)
    ang = inv[:, None] * pos.astype(F32)[None, :]
    return jnp.cos(ang), jnp.sin(ang)


def _posbias_kernel(p_ref, w_ref, b_ref, o_ref):
    for x in range(2):
        o_ref[x] = _dot(p_ref[x], w_ref[x]) + b_ref[x]


def pos_bias(cmp_pos, cmp_w1, cmp_b1):
    k = CMP_LEN * HEAD_DIM
    pos = jnp.zeros((2, 8, k), F32).at[:, 0].set(cmp_pos.reshape(2, k))
    out = pl.pallas_call(
        _posbias_kernel,
        out_shape=jax.ShapeDtypeStruct((2, 8, CMP_HIDDEN), F32),
        compiler_params=pltpu.CompilerParams(vmem_limit_bytes=VMEM_LIMIT),
        name="posbias",
    )(pos, cmp_w1.reshape(2, k, CMP_HIDDEN), cmp_b1.reshape(2, 1, CMP_HIDDEN))
    return out[:, 0]


CMP_PAGES = 16
CMP_CHUNKS = CMP_PAGES * PAGE // CMP_STRIDE
CHUNKS_PER_PAGE = PAGE // CMP_STRIDE


def _compress_kernel(pt_ref, *refs):
    pages = refs[:CMP_PAGES]
    perm_ref, w1_ref, pb_ref, w2_ref, w2t_ref, b2_ref, b2c_ref, k_ref, kt_ref, vt_ref, carry_ref = refs[CMP_PAGES:]
    s = pl.program_id(1)

    @pl.when(s == 0)
    def _():
        carry_ref[...] = jnp.zeros_like(carry_ref)

    n = CMP_CHUNKS
    perm = perm_ref[...]
    rows_by_p = [_dot_nt(perm, pg[0].astype(BF16)) for pg in pages]
    row0 = lax.broadcasted_iota(jnp.int32, (n, 1), 0) == 0
    accs = []
    for t in range(KV_ROW // LANE):
        x = t // 2
        sl = slice(t * LANE, (t + 1) * LANE)
        acc = jnp.zeros((n, 4 * CMP_HIDDEN), F32)
        for pp in range(CMP_STRIDE // 2):
            parts = []
            for p in (2 * pp, 2 * pp + 1):
                parts.append(jnp.concatenate(
                    [r[p * CHUNKS_PER_PAGE:(p + 1) * CHUNKS_PER_PAGE, sl] for r in rows_by_p], axis=0))
            lhs = jnp.concatenate(parts, axis=1).astype(BF16)
            acc = acc + _dot(lhs, w1_ref[x, pp])
        accs.append(acc)
    for t, acc in enumerate(accs):
        x = t // 2
        sl = slice(t * LANE, (t + 1) * LANE)
        hid = []
        for hh in range(2):
            part0 = acc[:, hh * 256:hh * 256 + CMP_HIDDEN]
            part1 = acc[:, hh * 256 + CMP_HIDDEN:(hh + 1) * 256]
            csl = slice((t * 2 + hh) * CMP_HIDDEN, (t * 2 + hh + 1) * CMP_HIDDEN)
            prev = jnp.where(row0, carry_ref[0:1, csl], pltpu.roll(part0, 1, 0))
            carry_ref[0:1, csl] = part0[n - 1:n, :]
            hid.append(_silu(prev + part1 + pb_ref[x:x + 1, :]))
        hid = jnp.concatenate(hid, axis=1).astype(BF16)
        out_t = _dot_nt(w2t_ref[x], hid) + b2c_ref[x]
        rows = slice((t % 2) * LANE, (t % 2 + 1) * LANE)
        if x == 0:
            k_ref[0, :, sl] = _dot(hid, w2_ref[...]) + b2_ref[...]
            kt_ref[0, rows, :] = out_t
        else:
            vt_ref[0, rows, :] = out_t


def compress(pool_t, page_table, perm, w1t, pb, w2k, w2t, b2k, b2c):
    b, n_pages = page_table.shape
    steps = n_pages // CMP_PAGES
    n_blk = n_pages * CHUNKS_PER_PAGE
    page_spec = lambda k: pl.BlockSpec((1, KV_ROW, PAGE), lambda bi, si, pt, k=k: (pt[bi, si * CMP_PAGES + k], 0, 0))
    const = lambda a: pl.BlockSpec(a.shape, lambda bi, si, pt: (0,) * a.ndim)
    consts = (perm, w1t, pb, w2k, w2t, b2k, b2c)
    fm_spec = pl.BlockSpec((1, HALF_ROW, CMP_CHUNKS), lambda bi, si, pt: (bi, 0, si))
    fm_shape = jax.ShapeDtypeStruct((b, HALF_ROW, n_blk), F32)
    grid_spec = pltpu.PrefetchScalarGridSpec(
        num_scalar_prefetch=1,
        grid=(b, steps),
        in_specs=[page_spec(k) for k in range(CMP_PAGES)] + [const(a) for a in consts],
        out_specs=[pl.BlockSpec((1, CMP_CHUNKS, HALF_ROW), lambda bi, si, pt: (bi, si, 0)), fm_spec, fm_spec],
        scratch_shapes=[pltpu.VMEM((8, 8 * CMP_HIDDEN), F32)],
    )
    return pl.pallas_call(
        _compress_kernel,
        grid_spec=grid_spec,
        out_shape=[jax.ShapeDtypeStruct((b, n_blk, HALF_ROW), F32), fm_shape, fm_shape],
        compiler_params=_cparams(("parallel", "arbitrary")),
        name="compress",
    )(page_table, *([pool_t] * CMP_PAGES), *consts)


def _compress_weights(cmp_w1, cmp_w2, cmp_b2):
    w1 = cmp_w1.reshape(2, 2, CMP_STRIDE // 2, 2, HEAD_DIM, CMP_HIDDEN)
    w1 = jnp.transpose(w1, (0, 2, 3, 4, 1, 5))
    eye = jnp.eye(2, dtype=F32)
    w1t = jnp.einsum('xqpdje,hk->xqphdkje', w1, eye).reshape(2, CMP_STRIDE // 2, 256, 512).astype(BF16)
    w2bd = jnp.einsum('xed,hk->xhekd', cmp_w2, eye).reshape(2, 256, LANE).astype(BF16)
    b2t = jnp.concatenate([cmp_b2, cmp_b2], axis=1)
    r = np.arange(PAGE)
    perm = np.zeros((PAGE, PAGE), np.float32)
    perm[(r % CMP_STRIDE) * CHUNKS_PER_PAGE + r // CMP_STRIDE, r] = 1.0
    return (jnp.asarray(perm, dtype=BF16), w1t, w2bd[0], jnp.transpose(w2bd, (0, 2, 1)), b2t[0:1], b2t.reshape(2, LANE, 1))


def _masked_exp0(s, mask):
    s = jnp.where(mask, s, NEG)
    m = jnp.max(s, axis=0, keepdims=True)
    p = jnp.where(mask, jnp.exp2(s - m), 0.0)
    return p, jnp.maximum(jnp.sum(p, axis=0, keepdims=True), TINY)


def _softmax0(s, mask):
    p, denom = _masked_exp0(s, mask)
    return p / denom


def _split_dot(a, x):
    hi = x.astype(BF16)
    lo = (x - hi.astype(F32)).astype(BF16)
    return _dot(a, hi) + _dot(a, lo)


def _split_dot_r(x, a):
    hi = x.astype(BF16)
    lo = (x - hi.astype(F32)).astype(BF16)
    return _dot(hi, a) + _dot(lo, a)


def _top_blocks(imps, cur):
    blk = lax.broadcasted_iota(jnp.int32, (imps[0].shape[0], 1), 0)
    forced = (blk == 0) | ((blk <= cur) & (blk > cur - N_LOCAL))
    imps = tuple(jnp.where(forced, REMOVED, jnp.where(blk > cur, -BIG, imp)) for imp in imps)
    picked = jnp.where(forced, 1.0, 0.0)
    blk_f = blk.astype(F32)

    def pick(_, carry):
        out = []
        for imp, sel in carry:
            mx = jnp.max(imp, axis=0, keepdims=True)
            first = jnp.min(jnp.where(imp == mx, blk_f, 1e9), axis=0, keepdims=True)
            hit = blk_f == first
            out.append((jnp.where(hit, REMOVED, imp), jnp.where(hit, 1.0, sel)))
        return tuple(out)

    final = lax.fori_loop(0, N_SELECT - 1 - N_LOCAL, pick, tuple((imp, picked) for imp in imps), unroll=True)
    return [sel for _, sel in final]


def _flash_update_biased(state, s, v_aug):
    m, acc = state
    m_new = jnp.maximum(m, jnp.max(s, axis=0, keepdims=True))
    p = jnp.exp2(s - m_new)
    acc = jnp.exp2(m - m_new) * acc + _dot(v_aug, p.astype(BF16))
    return m_new, acc


def _cmp_mask(n_rows, pos_q):
    r = lax.broadcasted_iota(jnp.int32, (n_rows, 1), 0)
    return (r >= 1) & (r * CMP_STRIDE + (CMP_LEN - CMP_STRIDE - 1) <= pos_q)


WIN_KEYS = WINDOW + Q_BLOCK
P_HEADS = 2
BLOCKS_PER_TILE = SEL_TILE // SEL_BLOCK


def _nsa_prompt_kernel(qt_ref, kc_ref, vct_ref, ks_ref, vst_ref, kw_ref, vwt_ref, gt_ref, mt_ref, o_ref, sel_ref, s_ref):
    i = pl.program_id(1)
    cols = NSA_GROUP * Q_BLOCK
    lane = lax.broadcasted_iota(jnp.int32, (1, Q_BLOCK), 1)
    pos_q = i * Q_BLOCK + lane
    tile4 = lambda a: jnp.concatenate([a] * NSA_GROUP, axis=1)
    heads = range(P_HEADS)
    q_ts = []
    for h in heads:
        q_blk = qt_ref[h * NSA_GROUP * HEAD_DIM:(h + 1) * NSA_GROUP * HEAD_DIM, :]
        q_ts.append(jnp.concatenate([q_blk[g * HEAD_DIM:(g + 1) * HEAD_DIM, :] for g in range(NSA_GROUP)], axis=1))

    nc = kc_ref.shape[1]
    s_cmp = [_dot(kc_ref[h], q_ts[h]) for h in heads]
    s_win = [_dot(kw_ref[h, pl.ds(pl.multiple_of(i * Q_BLOCK, Q_BLOCK), WIN_KEYS), :], q_ts[h]) for h in heads]

    mask_c = tile4(_cmp_mask(nc, pos_q))
    o_c, imps = [], []
    for h in heads:
        p_c = _softmax0(s_cmp[h], mask_c)
        o_c.append(_dot(vct_ref[h], p_c.astype(BF16)))
        pg = p_c[:, 0:Q_BLOCK]
        for g in range(1, NSA_GROUP):
            pg = pg + p_c[:, g * Q_BLOCK:(g + 1) * Q_BLOCK]
        imps.append(_split_dot(mt_ref[...], pg))
    for h, sel in enumerate(_top_blocks(imps, pos_q // SEL_BLOCK)):
        sel_ref[h] = sel

    zeros_q = jnp.zeros((LANE - HEAD_DIM - 16, cols), BF16)

    def q_aug(h, j):
        grp = sel_ref[h, pl.ds(pl.multiple_of(j * BLOCKS_PER_TILE, BLOCKS_PER_TILE), BLOCKS_PER_TILE), :]
        bias = jnp.concatenate([jnp.where(grp > 0.5, 0.0, NEG), jnp.zeros_like(grp)], axis=0)
        return jnp.concatenate([q_ts[h], tile4(bias).astype(BF16), zeros_q], axis=0)

    def scores(h, j):
        return _dot(ks_ref[h, pl.ds(pl.multiple_of(j * SEL_TILE, SEL_TILE), SEL_TILE), :], q_aug(h, j))

    def sel_body(j, states):
        st0, st1 = states
        s1 = scores(1, j)
        st0 = _flash_update_biased(st0, s_ref[...], vst_ref[0, j])
        s_ref[...] = scores(0, j + 1)
        st1 = _flash_update_biased(st1, s1, vst_ref[1, j])
        return st0, st1

    init = tuple((jnp.full((1, cols), NEG, F32), jnp.zeros((V_AUG, cols), F32)) for _ in heads)
    j_diag = (i * Q_BLOCK) // SEL_TILE
    s_ref[...] = scores(0, 0)
    states = lax.fori_loop(0, j_diag, sel_body, init)
    key_pos = j_diag * SEL_TILE + lax.broadcasted_iota(jnp.int32, (SEL_TILE, 1), 0)
    causal = tile4(jnp.where(key_pos <= pos_q, 0.0, NEG))
    last = (s_ref[...], scores(1, j_diag))
    o_s = []
    for h in heads:
        _, acc = _flash_update_biased(states[h], last[h] + causal, vst_ref[h, j_diag])
        o_s.append(acc[:HEAD_DIM] / jnp.maximum(acc[HEAD_DIM:HEAD_DIM + 1], TINY))

    w_pos = i * Q_BLOCK - WINDOW + lax.broadcasted_iota(jnp.int32, (WIN_KEYS, 1), 0)
    mask_w = tile4((w_pos <= pos_q) & (w_pos > pos_q - WINDOW) & (w_pos >= 0))
    o_w = []
    for h in heads:
        p_w, denom = _masked_exp0(s_win[h], mask_w)
        p_w = p_w.astype(BF16)
        acc = jnp.zeros((HEAD_DIM, cols), F32)
        for w in range(WIN_KEYS // Q_BLOCK):
            acc = acc + _dot(vwt_ref[h, i + w], p_w[w * Q_BLOCK:(w + 1) * Q_BLOCK, :])
        o_w.append(acc / denom)

    for h in heads:
        gt = jax.nn.sigmoid(gt_ref[h])
        outs = []
        for g in range(NSA_GROUP):
            sl = slice(g * Q_BLOCK, (g + 1) * Q_BLOCK)
            outs.append(gt[3 * g:3 * g + 1, :] * o_c[h][:, sl] + gt[3 * g + 1:3 * g + 2, :] * o_s[h][:, sl]
                        + gt[3 * g + 2:3 * g + 3, :] * o_w[h][:, sl])
        for pair in range(NSA_GROUP // 2):
            both = jnp.concatenate([outs[2 * pair], outs[2 * pair + 1]], axis=0)
            lo = (h * NSA_GROUP // 2 + pair) * LANE
            o_ref[:, lo:lo + LANE] = both.T


def nsa_prompt(qt, kc, vct, ks, vst, kw, vwt, gt, mt, t):
    nq = t // Q_BLOCK
    head = lambda a: pl.BlockSpec((P_HEADS,) + a.shape[1:], lambda h, i: (h,) + (0,) * (a.ndim - 1),
                                  pipeline_mode=pl.Buffered(1))
    width = P_HEADS * NSA_GROUP * HEAD_DIM
    return pl.pallas_call(
        _nsa_prompt_kernel,
        grid=(NSA_KV_HEADS // P_HEADS, nq),
        in_specs=[pl.BlockSpec((width, Q_BLOCK), lambda h, i: (h, i)),
                  head(kc), head(vct), head(ks), head(vst), head(kw), head(vwt),
                  pl.BlockSpec((P_HEADS, 16, Q_BLOCK), lambda h, i: (h, 0, i)),
                  pl.BlockSpec(mt.shape, lambda h, i: (0, 0))],
        out_specs=pl.BlockSpec((Q_BLOCK, width), lambda h, i: (i, h)),
        out_shape=jax.ShapeDtypeStruct((t, NSA_WIDTH), F32),
        scratch_shapes=[pltpu.VMEM((P_HEADS, mt.shape[0], Q_BLOCK), F32),
                        pltpu.VMEM((SEL_TILE, NSA_GROUP * Q_BLOCK), F32)],
        compiler_params=_cparams(("parallel", "arbitrary")),
        name="nsa_p",
    )(qt, kc, vct, ks, vst, kw, vwt, gt, mt)


def _cmp_to_sel_t(n_rows, n_blk, n_blk_pad):
    cs = (np.arange(n_rows)[None, :] - 1) * CMP_STRIDE
    js = np.arange(n_blk_pad)[:, None] * SEL_BLOCK
    m = (cs < js + SEL_BLOCK) & (cs + CMP_LEN > js) & (np.arange(n_rows)[None, :] >= 1) & (np.arange(n_blk_pad)[:, None] < n_blk)
    return jnp.asarray(m.astype(np.float32), dtype=BF16)


S_PAGES = 16
S_COLS = NSA_HEADS * 8


def _softmax_rows(s, mask):
    s = jnp.where(mask, s, NEG)
    m = jnp.max(s, axis=1, keepdims=True)
    p = jnp.where(mask, jnp.exp2(s - m), 0.0)
    return p / jnp.maximum(jnp.sum(p, axis=1, keepdims=True), TINY)


def _flash_rows(state, s, v_t):
    m, l, acc = state
    m_new = jnp.maximum(m, jnp.max(s, axis=1, keepdims=True))
    p = jnp.exp2(s - m_new)
    alpha = jnp.exp2(m - m_new)
    return m_new, alpha * l + jnp.sum(p, axis=1, keepdims=True), alpha * acc + _dot_nt(p.astype(BF16), v_t)


def _nsa_sample_kernel(n_steps, pt_ref, *refs):
    pages = refs[:S_PAGES]
    (qb_ref, kct_ref, vct_ref, cw_ref, nw_ref, ns_ref, g_ref, mt_ref, gsum_ref, emat_ref, o_ref,
     sel_ref, m_ref, l_ref, acc_ref, oc_ref, ow_ref) = refs[S_PAGES:]
    s_id = pl.program_id(1)
    past = n_steps * S_PAGES * PAGE
    qb = qb_ref[0]
    pos_q = past + lax.broadcasted_iota(jnp.int32, (S_COLS, 1), 0) % 8
    blocks_per_step = S_PAGES * PAGE // SEL_BLOCK

    def block_bias(grp, n_keys):
        flags = jnp.concatenate([grp, jnp.zeros((LANE - grp.shape[0], S_COLS), F32)], axis=0).T
        return _dot(jnp.where(flags > 0.5, 0.0, NEG).astype(BF16), emat_ref[:, :n_keys])

    @pl.when(s_id == 0)
    def _():
        n_ent = kct_ref.shape[2]
        s = _dot(qb, kct_ref[0].astype(BF16))
        ent = lax.broadcasted_iota(jnp.int32, (1, n_ent), 1)
        p_c = _softmax_rows(s, (ent >= 1) & (ent * CMP_STRIDE + (CMP_LEN - CMP_STRIDE - 1) <= pos_q))
        oc_ref[...] = _dot_nt(p_c.astype(BF16), vct_ref[0].astype(BF16))
        imp = _split_dot(mt_ref[...], p_c.T)
        imp = _split_dot_r(imp, gsum_ref[...])
        pos_row = past + lax.broadcasted_iota(jnp.int32, (1, S_COLS), 1) % 8
        sel_ref[...] = _top_blocks([imp], pos_row // SEL_BLOCK)[0]

        wb = cw_ref.shape[2]
        k_t = jnp.concatenate([cw_ref[0, :HALF_ROW, :], nw_ref[0, :HALF_ROW, :]], axis=1).astype(BF16)
        v_t = jnp.concatenate([cw_ref[0, HALF_ROW:, :], nw_ref[0, HALF_ROW:, :]], axis=1).astype(BF16)
        w_pos = past - wb + lax.broadcasted_iota(jnp.int32, (1, wb + PAGE), 1)
        p_w = _softmax_rows(_dot(qb, k_t), (w_pos <= pos_q) & (w_pos > pos_q - WINDOW) & (w_pos >= 0))
        ow_ref[...] = _dot_nt(p_w.astype(BF16), v_t)

        nblk0 = past // SEL_BLOCK
        key_pos = past + lax.broadcasted_iota(jnp.int32, (1, PAGE), 1)
        s = (_dot(qb, ns_ref[0, :HALF_ROW, :].astype(BF16)) + block_bias(sel_ref[nblk0:nblk0 + 8, :], PAGE)
             + jnp.where(key_pos <= pos_q, 0.0, NEG))
        init = (jnp.full((S_COLS, 1), NEG, F32), jnp.zeros((S_COLS, 1), F32), jnp.zeros((S_COLS, HALF_ROW), F32))
        m_ref[...], l_ref[...], acc_ref[...] = _flash_rows(init, s, ns_ref[0, HALF_ROW:, :].astype(BF16))

    grp = sel_ref[pl.ds(pl.multiple_of(s_id * blocks_per_step, blocks_per_step), blocks_per_step), :]
    k_t = jnp.concatenate([pg[0, :HALF_ROW, :] for pg in pages], axis=1).astype(BF16)
    v_t = jnp.concatenate([pg[0, HALF_ROW:, :] for pg in pages], axis=1).astype(BF16)
    s = _dot(qb, k_t) + block_bias(grp, S_PAGES * PAGE)
    st = _flash_rows((m_ref[...], l_ref[...], acc_ref[...]), s, v_t)
    m_ref[...], l_ref[...], acc_ref[...] = st

    @pl.when(s_id == n_steps - 1)
    def _():
        g = jax.nn.sigmoid(g_ref[0])
        o_s = st[2] / jnp.maximum(st[1], TINY)
        o_ref[0] = g[:, 0:1] * oc_ref[...] + g[:, 1:2] * o_s + g[:, 2:3] * ow_ref[...]


def nsa_sample(pool_t, page_table, qb, kct, vct, cache_wt, new_w, new_s, g, mt, gsum, emat):
    b, n_pages = page_table.shape
    steps = n_pages // S_PAGES
    page_spec = lambda k: pl.BlockSpec((1, KV_ROW, PAGE), lambda bi, si, pt, k=k: (pt[bi, si * S_PAGES + k], 0, 0))
    per_b = lambda a: pl.BlockSpec((1,) + a.shape[1:], lambda bi, si, pt: (bi,) + (0,) * (a.ndim - 1))
    const = lambda a: pl.BlockSpec(a.shape, lambda bi, si, pt: (0,) * a.ndim)
    grid_spec = pltpu.PrefetchScalarGridSpec(
        num_scalar_prefetch=1,
        grid=(b, steps),
        in_specs=[page_spec(k) for k in range(S_PAGES)] + [
            per_b(qb), per_b(kct), per_b(vct), per_b(cache_wt), per_b(new_w), per_b(new_s), per_b(g),
            const(mt), const(gsum), const(emat)],
        out_specs=pl.BlockSpec((1, S_COLS, HALF_ROW), lambda bi, si, pt: (bi, 0, 0)),
        scratch_shapes=[pltpu.VMEM((mt.shape[0], S_COLS), F32), pltpu.VMEM((S_COLS, 1), F32), pltpu.VMEM((S_COLS, 1), F32),
                        pltpu.VMEM((S_COLS, HALF_ROW), F32), pltpu.VMEM((S_COLS, HALF_ROW), F32),
                        pltpu.VMEM((S_COLS, HALF_ROW), F32)],
    )
    return pl.pallas_call(
        functools.partial(_nsa_sample_kernel, steps),
        grid_spec=grid_spec,
        out_shape=jax.ShapeDtypeStruct((b, S_COLS, HALF_ROW), F32),
        compiler_params=_cparams(("parallel", "arbitrary")),
        name="nsa_s",
    )(page_table, *([pool_t] * S_PAGES), qb, kct, vct, cache_wt, new_w, new_s, g, mt, gsum, emat)


GLA_SUB = 16


def _gla_head(q, k, v, cum, state):
    c = q.shape[0]
    sub = min(GLA_SUB, c)
    lane = lax.broadcasted_iota(jnp.int32, (1, LANE), 1)
    t_sub = lax.broadcasted_iota(jnp.int32, (sub, 1), 0)
    row_pad = lambda a: jnp.concatenate([a, jnp.zeros((LANE - c, a.shape[1]), F32)], axis=0).astype(BF16)
    v_pad = row_pad(v)
    o = _dot((q * jnp.exp(cum)).astype(BF16), state.astype(BF16))
    blocks = []
    for r0 in range(0, c, sub):
        q_i, cum_i = q[r0:r0 + sub], cum[r0:r0 + sub]
        if r0 == 0:
            att_i = jnp.zeros((sub, LANE), F32)
        else:
            base = cum[r0 - 1:r0]
            q_dec = (q_i * jnp.exp(cum_i - base)).astype(BF16)
            k_dec = row_pad(k * jnp.exp(jnp.minimum(base - cum, 0.0)))
            att_i = jnp.where(lane < r0, _dot_nt(q_dec, k_dec), 0.0)
        for s in range(r0, r0 + sub):
            decay = jnp.exp(jnp.where(t_sub >= s - r0, cum_i - cum[s:s + 1], NEG))
            column = jnp.sum(q_i * k[s:s + 1] * decay, axis=-1, keepdims=True)
            att_i = jnp.where(lane == s, column, att_i)
        blocks.append(att_i)
    att = jnp.concatenate(blocks, axis=0)
    o = o + _dot(att.astype(BF16), v_pad)

    c_last = cum[c - 1:c]
    k_end = jnp.concatenate([k * jnp.exp(c_last - cum), jnp.zeros((LANE - c, GLA_DK), F32)], axis=0)
    eye = lax.broadcasted_iota(jnp.int32, (GLA_DK, GLA_DK), 0) == lax.broadcasted_iota(jnp.int32, (GLA_DK, GLA_DK), 1)
    decay_col = jnp.sum(jnp.where(eye, jnp.exp(c_last), 0.0), axis=1, keepdims=True)
    return o, decay_col * state + _dot(k_end.T.astype(BF16), v_pad)


def _gla_kernel(q_ref, k_ref, v_ref, a_ref, z_ref, wa_ref, ba_ref, nw_ref, s0_ref, o_ref, so_ref, s_ref):
    c = q_ref.shape[0]
    ci = pl.program_id(1)

    @pl.when(ci == 0)
    def _():
        s_ref[...] = s0_ref[0]

    pre = _dot(a_ref[...].astype(BF16), wa_ref[...].astype(BF16)) + ba_ref[...]
    log_a = (jnp.minimum(pre, 0.0) - jnp.log1p(jnp.exp(-jnp.abs(pre)))) / GLA_TAU
    t_idx = lax.broadcasted_iota(jnp.int32, (c, 1), 0)
    cum = log_a
    sh = 1
    while sh < c:
        cum = cum + jnp.where(t_idx >= sh, pltpu.roll(cum, sh, 0), 0.0)
        sh *= 2
    for h in range(GLA_HEADS):
        ks = slice(h * GLA_DK, (h + 1) * GLA_DK)
        vs = slice(h * GLA_DV, (h + 1) * GLA_DV)
        o, new_state = _gla_head(q_ref[:, ks] * (GLA_DK ** -0.5), k_ref[:, ks], v_ref[:, vs], cum[:, ks], s_ref[h])
        s_ref[h] = new_state
        y = o * lax.rsqrt(jnp.mean(o * o, axis=-1, keepdims=True) + EPS) * nw_ref[...]
        o_ref[:, vs] = y * _silu(z_ref[:, vs])

    @pl.when(ci == pl.num_programs(1) - 1)
    def _():
        so_ref[0] = s_ref[...]


def gla(proj, w_a2p, b_a, gla_norm_w, s0, n_seq, chunk):
    rows = proj.shape[0]
    n_chunk = rows // (n_seq * chunk)
    kw, vw = GLA_HEADS * GLA_DK, GLA_HEADS * GLA_DV
    rows_at = lambda width, col: pl.BlockSpec((chunk, width), lambda b, c: (b * n_chunk + c, col // width))
    state_spec = pl.BlockSpec((1, GLA_HEADS, GLA_DK, GLA_DV), lambda b, c: (b, 0, 0, 0))
    return pl.pallas_call(
        _gla_kernel,
        grid=(n_seq, n_chunk),
        in_specs=[rows_at(kw, C_QG), rows_at(kw, C_KG), rows_at(vw, C_VG), rows_at(LANE, C_AG), rows_at(vw, C_ZG),
                  pl.BlockSpec((LANE, kw), lambda b, c: (0, 0)),
                  pl.BlockSpec((1, kw), lambda b, c: (0, 0)),
                  pl.BlockSpec((1, GLA_DV), lambda b, c: (0, 0)),
                  state_spec],
        out_specs=[pl.BlockSpec((chunk, vw), lambda b, c: (b * n_chunk + c, 0)), state_spec],
        out_shape=[jax.ShapeDtypeStruct((rows, vw), F32),
                   jax.ShapeDtypeStruct((n_seq, GLA_HEADS, GLA_DK, GLA_DV), F32)],
        scratch_shapes=[pltpu.VMEM((GLA_HEADS, GLA_DK, GLA_DV), F32)],
        compiler_params=_cparams(("parallel", "arbitrary")),
        name="gla",
    )(proj, proj, proj, proj, proj, w_a2p, b_a.reshape(1, -1), gla_norm_w.reshape(1, -1), s0)


def _out_kernel(on_ref, zn_ref, og_ref, mn_ref, mg_ref, x_ref, gate_ref, wn_ref, wg_ref, wo_ref, fw_ref, y_ref):
    o_nsa = (on_ref[...] * _silu(zn_ref[...])).astype(BF16)
    merged = (jax.nn.sigmoid(mn_ref[...]) * _dot(o_nsa, wn_ref[...])
              + jax.nn.sigmoid(mg_ref[...]) * _dot(og_ref[...].astype(BF16), wg_ref[...]))
    y = x_ref[...] + gate_ref[...] * _dot(merged.astype(BF16), wo_ref[...])
    y_ref[...] = y * lax.rsqrt(jnp.mean(y * y, axis=-1, keepdims=True) + EPS) * fw_ref[...]


def out_proj(o_nsa, o_gla, proj, x, gate, w_o_nsa, w_o_gla, w_out, final_norm_w, tm):
    rows = x.shape[0]
    per_row = gate.shape[0] != 1
    gate_spec = pl.BlockSpec((tm, D_MODEL), lambda i: (i, 0)) if per_row else pl.BlockSpec((1, D_MODEL), lambda i: (0, 0))
    resident = lambda a: pl.BlockSpec(a.shape, lambda i: (0, 0), pipeline_mode=pl.Buffered(1))
    return pl.pallas_call(
        _out_kernel,
        grid=(rows // tm,),
        in_specs=[pl.BlockSpec((tm, NSA_WIDTH), lambda i: (i, 0)),
                  pl.BlockSpec((tm, NSA_WIDTH), lambda i: (i, C_ZN // NSA_WIDTH)),
                  pl.BlockSpec((tm, NSA_WIDTH), lambda i: (i, 0)),
                  pl.BlockSpec((tm, D_MODEL), lambda i: (i, C_MN // D_MODEL)),
                  pl.BlockSpec((tm, D_MODEL), lambda i: (i, C_MG // D_MODEL)),
                  pl.BlockSpec((tm, D_MODEL), lambda i: (i, 0)),
                  gate_spec, resident(w_o_nsa), resident(w_o_gla), resident(w_out),
                  pl.BlockSpec((1, D_MODEL), lambda i: (0, 0))],
        out_specs=pl.BlockSpec((tm, D_MODEL), lambda i: (i, 0)),
        out_shape=jax.ShapeDtypeStruct((rows, D_MODEL), F32),
        compiler_params=_cparams(("parallel",)),
        name="outproj",
    )(o_nsa, proj, o_gla, proj, proj, x, gate, w_o_nsa, w_o_gla, w_out, final_norm_w.reshape(1, D_MODEL))


def _feature_major(a):
    lead = a.shape[:-4]
    n = len(lead)
    a = jnp.transpose(a, tuple(range(n)) + (n + 1, n + 2, n + 3, n))
    return a.reshape(lead + (KV_ROW, a.shape[-1]))


def _token_major(a_t, lead):
    rows = a_t.shape[-1]
    a = a_t.reshape(a_t.shape[:-2] + (2, NSA_KV_HEADS, HEAD_DIM, rows))
    n = a.ndim - 4
    a = jnp.transpose(a, tuple(range(n)) + (n + 3, n, n + 1, n + 2))
    return a.reshape(lead + (rows, 2, NSA_KV_HEADS, HEAD_DIM))


def kernel(x_prompt, x_sample, cache_kv_cmp, cache_kv_sel, cache_kv_win, state_gla, page_table, c_prompt, c_sample, norm_w, w_ada, b_ada, w_in, cmp_pos, cmp_w1, cmp_b1, cmp_w2, cmp_b2, w_a2, b_a, gla_norm_w, w_o_nsa, w_o_gla, w_out, final_norm_w):
    assert x_prompt.shape[0] == 1 and norm_w.shape[0] == 1, "one prompt sequence, one layer"
    t_p = x_prompt.shape[1]
    b_s, t_s = x_sample.shape[:2]
    past = page_table.shape[1] * PAGE
    wb = cache_kv_win.shape[2]
    assert t_s == 8 and wb == WINDOW and past % (S_PAGES * PAGE) == 0 and t_p % SEL_TILE == 0
    rows_s = b_s * t_s

    c_rows = jnp.zeros((40, D_MODEL), F32).at[0:1].set(c_prompt).at[1:1 + b_s].set(c_sample)
    mod = ada_mod(c_rows, w_ada[0], b_ada[0])
    shift, scale, gate = mod[:, :D_MODEL], mod[:, D_MODEL:2 * D_MODEL], mod[:, 2 * D_MODEL:]
    per_row = lambda a: jnp.repeat(a[1:1 + b_s], t_s, axis=0)

    w_t = jnp.transpose(w_in[0])
    xp = x_prompt.reshape(t_p, D_MODEL)
    xs = x_sample.reshape(rows_s, D_MODEL)
    h_p = modulated_norm(xp, scale[0:1], shift[0:1], norm_w[0], 512)
    h_s = modulated_norm(xs, per_row(scale), per_row(shift), norm_w[0], rows_s)
    proj_p = in_proj(h_p, w_t, RM_OFFSETS, 1024, RM_TILE, False)
    projt_p = in_proj(h_p, w_t, FM_OFFSETS, 512, FM_TILE, True)
    proj_s = in_proj(h_s, w_t, RM_OFFSETS, rows_s, RM_TILE, False)
    projt_s = in_proj(h_s, w_t, FM_OFFSETS, rows_s, FM_TILE, True)

    cos_p, sin_p = _rope_tables(jnp.arange(t_p, dtype=jnp.int32))
    cos_s, sin_s = _rope_tables(jnp.tile(past + jnp.arange(t_s, dtype=jnp.int32), b_s))
    qt_p, kvc_p, kvs_p, kvw_p, ks_p, kw_p, vst_p, vwt_p = rope_stage(projt_p, cos_p, sin_p, 512, True)
    qt_s, kvc_s, kvs_s, kvw_s = rope_stage(projt_s, cos_s, sin_s, rows_s, False)

    pb = pos_bias(cmp_pos[0], cmp_w1[0], cmp_b1[0])
    cmp_consts = _compress_weights(cmp_w1[0], cmp_w2[0], cmp_b2[0])
    perm, w1t, w2k, w2t, b2k, b2c = cmp_consts
    ident = jnp.arange(t_p // PAGE, dtype=jnp.int32)[None, :]
    pages_p = jnp.transpose(kvc_p.reshape(KV_ROW, t_p // PAGE, PAGE), (1, 0, 2))
    kc_p, _, vct_p = compress(pages_p, ident, perm, w1t, pb, w2k, w2t, b2k, b2c)
    pool_c = _feature_major(cache_kv_cmp[0])
    _, kct_s, vct_s = compress(pool_c, page_table, perm, w1t, pb, w2k, w2t, b2k, b2c)

    n_ent = kc_p.shape[1]
    kc_h = jnp.transpose(kc_p[0].reshape(n_ent, NSA_KV_HEADS, HEAD_DIM), (1, 0, 2)).astype(BF16)
    vct_h = vct_p[0].reshape(NSA_KV_HEADS, HEAD_DIM, n_ent).astype(BF16)
    kw_h = jnp.pad(kw_p, ((0, 0), (WINDOW, 0), (0, 0)))
    vwt_h = jnp.pad(vwt_p, ((0, 0), (WINDOW // Q_BLOCK, 0), (0, 0), (0, 0)))
    g_p = projt_p[R_GN:R_GN + 48].reshape(NSA_KV_HEADS, 12, t_p)
    g_p = jnp.pad(g_p, ((0, 0), (0, 4), (0, 0)))
    mt_p = _cmp_to_sel_t(n_ent, t_p // SEL_BLOCK, t_p // SEL_BLOCK)
    o_nsa_p = nsa_prompt(qt_p, kc_h, vct_h, ks_p, vst_p, kw_h, vwt_h, g_p, mt_p, t_p)

    q5 = qt_s.reshape(NSA_KV_HEADS, NSA_GROUP, HEAD_DIM, b_s, t_s)
    q_c = jnp.transpose(q5, (3, 0, 1, 4, 2)).reshape(b_s, NSA_KV_HEADS, NSA_GROUP * t_s, HEAD_DIM)
    eye = jnp.eye(NSA_KV_HEADS, dtype=BF16)
    qb = jnp.einsum('bhcd,hk->bhckd', q_c, eye).reshape(b_s, S_COLS, HALF_ROW)
    new_keys = lambda a_t: jnp.pad(jnp.transpose(a_t.reshape(KV_ROW, b_s, t_s), (1, 0, 2)), ((0, 0), (0, 0), (0, PAGE - t_s)))
    g_s = projt_s[R_GN:R_GN + 48].reshape(NSA_KV_HEADS, NSA_GROUP, 3, b_s, t_s)
    g_s = jnp.transpose(g_s, (3, 0, 1, 4, 2)).reshape(b_s, S_COLS, 3)
    g_s = jnp.pad(g_s, ((0, 0), (0, 0), (0, 5)))
    n_blk_s = -(-(past + t_s) // SEL_BLOCK)
    mt_s = _cmp_to_sel_t(kct_s.shape[2], n_blk_s, -(-n_blk_s // 8) * 8)
    col = np.arange(S_COLS)
    gsum = jnp.asarray(((col[:, None] // 32 == col[None, :] // 32) & (col[:, None] % 8 == col[None, :] % 8)).astype(np.float32), dtype=BF16)
    emat = jnp.asarray((np.arange(S_PAGES * PAGE)[None, :] // SEL_BLOCK == np.arange(LANE)[:, None]).astype(np.float32), dtype=BF16)
    cache_wt = _feature_major(cache_kv_win[0])
    o_all = nsa_sample(_feature_major(cache_kv_sel[0]), page_table, qb, kct_s, vct_s, cache_wt,
                       new_keys(kvw_s), new_keys(kvs_s), g_s, mt_s, gsum, emat)
    o6 = o_all.reshape(b_s, NSA_KV_HEADS, NSA_GROUP, t_s, NSA_KV_HEADS, HEAD_DIM)
    o_nsa_s = jnp.stack([o6[:, h, :, :, h, :] for h in range(NSA_KV_HEADS)], axis=1)
    o_nsa_s = jnp.transpose(o_nsa_s, (0, 3, 1, 2, 4)).reshape(rows_s, NSA_WIDTH)

    w_a2p = jnp.zeros((LANE, GLA_HEADS * GLA_DK), F32).at[:GLA_RANK].set(w_a2[0])
    s0_p = jnp.zeros((1, GLA_HEADS, GLA_DK, GLA_DV), F32)
    o_gla_p, st_p = gla(proj_p, w_a2p, b_a[0], gla_norm_w[0], s0_p, 1, GLA_CHUNK)
    o_gla_s, st_s = gla(proj_s, w_a2p, b_a[0], gla_norm_w[0], state_gla[0], b_s, t_s)

    wn, wg, wo = w_o_nsa[0].astype(BF16), w_o_gla[0].astype(BF16), w_out[0].astype(BF16)
    y_p = out_proj(o_nsa_p, o_gla_p, proj_p, xp, gate[0:1], wn, wg, wo, final_norm_w, 256)
    y_s = out_proj(o_nsa_s, o_gla_s, proj_s, xs, per_row(gate), wn, wg, wo, final_norm_w, rows_s)

    sample_rows = lambda a_t: _token_major(jnp.transpose(a_t.reshape(KV_ROW, b_s, t_s), (1, 0, 2)), (1, b_s))
    win_t = jnp.concatenate([cache_wt, jnp.transpose(kvw_s.reshape(KV_ROW, b_s, t_s), (1, 0, 2))], axis=2)[:, :, t_s:]
    n_win = min(WINDOW, t_p)
    return (y_p.reshape(x_prompt.shape), y_s.reshape(x_sample.shape),
            _token_major(kvc_p, (1, 1)), sample_rows(kvc_s), _token_major(kvs_p, (1, 1)), sample_rows(kvs_s),
            _token_major(kvw_p[:, t_p - n_win:], (1, 1)), _token_major(win_t, (1, b_s)),
            st_p[None], st_s[None])
```

```python
import functools

import jax
import jax.numpy as jnp
import numpy as np
from jax import lax
from jax.experimental import pallas as pl
from jax.experimental.pallas import tpu as pltpu

F32 = jnp.float32
BF16 = jnp.bfloat16

D_MODEL = 2048
HEAD_DIM = 64
NSA_HEADS = 16
NSA_KV_HEADS = 4
NSA_GROUP = 4
NSA_WIDTH = 1024
HALF_ROW = NSA_KV_HEADS * HEAD_DIM
KV_ROW = 2 * HALF_ROW
CMP_LEN = 32
CMP_STRIDE = 16
CMP_HIDDEN = 128
SEL_BLOCK = 64
N_SELECT = 16
N_LOCAL = 2
WINDOW = 512
Q_BLOCK = 128
PAGE = 128
GLA_HEADS = 4
GLA_DK = 128
GLA_DV = 256
GLA_RANK = 16
GLA_TAU = 16.0
GLA_CHUNK = 64
ROPE_THETA = 10000.0
EPS = 1e-6
NEG = -1e30
BIG = 1e30
TINY = 1e-30
REMOVED = -3e38
LOG2E = 1.4426950408889634
SEL_TILE = 512
V_AUG = HEAD_DIM + 16

LANE = 128
VMEM_LIMIT = 48 * 1024 * 1024

(W_Q, W_KV, W_GN, W_ZN, W_QG, W_KG, W_VG, W_AG, W_ZG, W_MN, W_MG) = (
    0, 1024, 2560, 2608, 3632, 4144, 4656, 5680, 5696, 6720, 8768)
RM_TILE = 1024
W_ALIGN = 16
C_MN, C_MG, C_ZN, C_VG, C_ZG, C_QG, C_KG, C_AG = 0, 2048, 4096, 5120, 6144, 7168, 7680, 8192
assert W_KG == W_QG + RM_TILE // 2
RM_SOURCES = ((W_MN, 2), (W_MG, 2), (W_ZN, 1), (W_VG, 1), (W_ZG, 1), (W_QG, 1), (W_AG, 1))
RM_OFFSETS = tuple(start + RM_TILE * k for start, tiles in RM_SOURCES for k in range(tiles))
RM_COLS = RM_TILE * len(RM_OFFSETS)
R_Q, R_KV, R_GN = W_Q, W_KV, W_GN
FM_TILE = 1344
FM_ROWS = 2 * FM_TILE
FM_OFFSETS = (0, FM_TILE)


def _cparams(sem):
    return pltpu.CompilerParams(dimension_semantics=sem, vmem_limit_bytes=VMEM_LIMIT)


def _dot(a, b):
    return jnp.dot(a, b, preferred_element_type=F32)


def _dot_nt(a, b):
    return lax.dot_general(a, b, (((1,), (1,)), ((), ())), preferred_element_type=F32)


def _silu(x):
    return x * jax.nn.sigmoid(x)


def _ada_kernel(c_ref, w_ref, b_ref, o_ref):
    o_ref[...] = _dot(c_ref[...].astype(BF16), w_ref[...].astype(BF16)) + b_ref[...]


def ada_mod(c_rows, w_ada, b_ada):
    rows, tn = c_rows.shape[0], 512
    n = w_ada.shape[1]
    return pl.pallas_call(
        _ada_kernel,
        grid=(n // tn,),
        in_specs=[pl.BlockSpec((rows, D_MODEL), lambda j: (0, 0)),
                  pl.BlockSpec((D_MODEL, tn), lambda j: (0, j)),
                  pl.BlockSpec((1, tn), lambda j: (0, j))],
        out_specs=pl.BlockSpec((rows, tn), lambda j: (0, j)),
        out_shape=jax.ShapeDtypeStruct((rows, n), F32),
        compiler_params=_cparams(("parallel",)),
        name="ada",
    )(c_rows, w_ada, b_ada.reshape(1, n))


def _norm_kernel(x_ref, sc_ref, sh_ref, nw_ref, h_ref):
    x = x_ref[...]
    y = x * lax.rsqrt(jnp.mean(x * x, axis=-1, keepdims=True) + EPS) * nw_ref[...]
    h_ref[...] = (y * (1.0 + sc_ref[...]) + sh_ref[...]).astype(BF16)


def modulated_norm(x, scale, shift, norm_w, tm):
    rows = x.shape[0]
    per_row = scale.shape[0] != 1
    mod_spec = pl.BlockSpec((tm, D_MODEL), lambda i: (i, 0)) if per_row else pl.BlockSpec((1, D_MODEL), lambda i: (0, 0))
    return pl.pallas_call(
        _norm_kernel,
        grid=(rows // tm,),
        in_specs=[pl.BlockSpec((tm, D_MODEL), lambda i: (i, 0)), mod_spec, mod_spec,
                  pl.BlockSpec((1, D_MODEL), lambda i: (0, 0))],
        out_specs=pl.BlockSpec((tm, D_MODEL), lambda i: (i, 0)),
        out_shape=jax.ShapeDtypeStruct((rows, D_MODEL), BF16),
        compiler_params=_cparams(("parallel",)),
        name="norm",
    )(x, scale, shift, norm_w.reshape(1, D_MODEL))


def _inproj_rm_kernel(off_ref, h_ref, w_ref, o_ref, wb_ref):
    @pl.when(pl.program_id(1) == 0)
    def _():
        wb_ref[...] = w_ref[...].astype(BF16)

    o_ref[...] = _dot_nt(h_ref[...], wb_ref[...])


def _inproj_fm_kernel(off_ref, h_ref, w_ref, o_ref, wb_ref):
    @pl.when(pl.program_id(1) == 0)
    def _():
        wb_ref[...] = w_ref[...].astype(BF16)

    o_ref[...] = _dot_nt(wb_ref[...], h_ref[...])


def in_proj(h, w_t, row_offsets, tm, tn, feature_major):
    rows, n = h.shape[0], len(row_offsets) * tn
    if feature_major:
        body, out_spec, out_shape = _inproj_fm_kernel, pl.BlockSpec((tn, tm), lambda j, i, off: (j, i)), (n, rows)
    else:
        body, out_spec, out_shape = _inproj_rm_kernel, pl.BlockSpec((tm, tn), lambda j, i, off: (i, j)), (rows, n)
    grid_spec = pltpu.PrefetchScalarGridSpec(
        num_scalar_prefetch=1,
        grid=(len(row_offsets), rows // tm),
        in_specs=[pl.BlockSpec((tm, D_MODEL), lambda j, i, off: (i, 0)),
                  pl.BlockSpec((pl.Element(tn), pl.Element(D_MODEL)), lambda j, i, off: (off[j] * W_ALIGN, 0))],
        out_specs=out_spec,
        scratch_shapes=[pltpu.VMEM((tn, D_MODEL), BF16)],
    )
    return pl.pallas_call(
        body,
        grid_spec=grid_spec,
        out_shape=jax.ShapeDtypeStruct(out_shape, F32),
        compiler_params=_cparams(("parallel", "arbitrary")),
        name="inproj_fm" if feature_major else "inproj_rm",
    )(jnp.asarray([o // W_ALIGN for o in row_offsets], jnp.int32), h, w_t)


def _rope_kernel(q_ref, c_ref, s_ref, w_ref, cos_ref, sin_ref, qo_ref, co_ref, so_ref, wo_ref, *tile_refs):
    cos, sin = cos_ref[...], sin_ref[...]
    hh = HEAD_DIM // 2
    tr = cos.shape[1]

    def rot(src, head):
        x1 = src[head * HEAD_DIM:head * HEAD_DIM + hh, :]
        x2 = src[head * HEAD_DIM + hh:(head + 1) * HEAD_DIM, :]
        return x1 * cos - x2 * sin, x2 * cos + x1 * sin

    q_scale = HEAD_DIM ** -0.5 * LOG2E
    for head in range(NSA_HEADS):
        o1, o2 = rot(q_ref, head)
        qo_ref[head * HEAD_DIM:head * HEAD_DIM + hh, :] = (o1 * q_scale).astype(BF16)
        qo_ref[head * HEAD_DIM + hh:(head + 1) * HEAD_DIM, :] = (o2 * q_scale).astype(BF16)
    for src, dst in ((c_ref, co_ref), (s_ref, so_ref), (w_ref, wo_ref)):
        for head in range(NSA_KV_HEADS):
            o1, o2 = rot(src, head)
            dst[head * HEAD_DIM:head * HEAD_DIM + hh, :] = o1
            dst[head * HEAD_DIM + hh:(head + 1) * HEAD_DIM, :] = o2
        dst[HALF_ROW:, :] = src[HALF_ROW:, :]
    if tile_refs:
        ks_ref, kw_ref, vs_ref, vw_ref = tile_refs
        lane = lax.broadcasted_iota(jnp.int32, (1, LANE), 1)
        r = lax.broadcasted_iota(jnp.int32, (tr, 1), 0)
        onehot = jnp.where(lane - HEAD_DIM == (r // SEL_BLOCK) % (SEL_TILE // SEL_BLOCK), 1.0, 0.0)
        ones_row = jnp.where(lax.broadcasted_iota(jnp.int32, (V_AUG - HEAD_DIM, SEL_TILE), 0) == 0, 1.0, 0.0)
        for pair in range(NSA_KV_HEADS // 2):
            k_pair = so_ref[pair * LANE:(pair + 1) * LANE, :].T
            ks_ref[2 * pair] = jnp.where(lane < HEAD_DIM, k_pair, onehot).astype(BF16)
            ks_ref[2 * pair + 1] = jnp.where(lane < HEAD_DIM, pltpu.roll(k_pair, HEAD_DIM, 1), onehot).astype(BF16)
            kw_pair = wo_ref[pair * LANE:(pair + 1) * LANE, :].T.astype(BF16)
            kw_ref[2 * pair] = kw_pair[:, :HEAD_DIM]
            kw_ref[2 * pair + 1] = kw_pair[:, HEAD_DIM:]
        for head in range(NSA_KV_HEADS):
            rows = slice(HALF_ROW + head * HEAD_DIM, HALF_ROW + (head + 1) * HEAD_DIM)
            v = so_ref[rows, :]
            for w in range(tr // SEL_TILE):
                vs_ref[head, w] = jnp.concatenate([v[:, w * SEL_TILE:(w + 1) * SEL_TILE], ones_row], axis=0).astype(BF16)
            v = wo_ref[rows, :].astype(BF16)
            for w in range(tr // Q_BLOCK):
                vw_ref[head, w] = v[:, w * Q_BLOCK:(w + 1) * Q_BLOCK]


def rope_stage(proj_t, cos_t, sin_t, tr, with_tiles):
    tok = proj_t.shape[1]
    kv_spec = lambda k: pl.BlockSpec((KV_ROW, tr), lambda i, k=k: (R_KV // KV_ROW + k, i))
    out_kv = jax.ShapeDtypeStruct((KV_ROW, tok), F32)
    tab = pl.BlockSpec((HEAD_DIM // 2, tr), lambda i: (0, i))
    out_specs = [pl.BlockSpec((NSA_WIDTH, tr), lambda i: (0, i))] + [pl.BlockSpec((KV_ROW, tr), lambda i: (0, i))] * 3
    out_shape = [jax.ShapeDtypeStruct((NSA_WIDTH, tok), BF16), out_kv, out_kv, out_kv]
    if with_tiles:
        k_rows = lambda width: jax.ShapeDtypeStruct((NSA_KV_HEADS, tok, width), BF16)
        k_spec = lambda width: pl.BlockSpec((NSA_KV_HEADS, tr, width), lambda i: (0, i, 0))
        v_tiles = lambda rows, tile: jax.ShapeDtypeStruct((NSA_KV_HEADS, tok // tile, rows, tile), BF16)
        v_spec = lambda rows, tile: pl.BlockSpec((NSA_KV_HEADS, tr // tile, rows, tile), lambda i: (0, i, 0, 0))
        out_specs += [k_spec(LANE), k_spec(HEAD_DIM), v_spec(V_AUG, SEL_TILE), v_spec(HEAD_DIM, Q_BLOCK)]
        out_shape += [k_rows(LANE), k_rows(HEAD_DIM), v_tiles(V_AUG, SEL_TILE), v_tiles(HEAD_DIM, Q_BLOCK)]
    return pl.pallas_call(
        _rope_kernel,
        grid=(tok // tr,),
        in_specs=[pl.BlockSpec((NSA_WIDTH, tr), lambda i: (R_Q // NSA_WIDTH, i)), kv_spec(0), kv_spec(1), kv_spec(2), tab, tab],
        out_specs=out_specs,
        out_shape=out_shape,
        compiler_params=_cparams(("parallel",)),
        name="rope",
    )(proj_t, proj_t, proj_t, proj_t, cos_t, sin_t)


def _rope_tables(pos):
    half = HEAD_DIM // 2
    inv = ROPE_THETA ** (-jnp.arange(half, dtype=F32) / half)
    ang = inv[:, None] * pos.astype(F32)[None, :]
    return jnp.cos(ang), jnp.sin(ang)


def _posbias_kernel(p_ref, w_ref, b_ref, o_ref):
    for x in range(2):
        o_ref[x] = _dot(p_ref[x], w_ref[x]) + b_ref[x]


def pos_bias(cmp_pos, cmp_w1, cmp_b1):
    k = CMP_LEN * HEAD_DIM
    pos = jnp.zeros((2, 8, k), F32).at[:, 0].set(cmp_pos.reshape(2, k))
    out = pl.pallas_call(
        _posbias_kernel,
        out_shape=jax.ShapeDtypeStruct((2, 8, CMP_HIDDEN), F32),
        compiler_params=pltpu.CompilerParams(vmem_limit_bytes=VMEM_LIMIT),
        name="posbias",
    )(pos, cmp_w1.reshape(2, k, CMP_HIDDEN), cmp_b1.reshape(2, 1, CMP_HIDDEN))
    return out[:, 0]


CMP_PAGES = 16
CMP_CHUNKS = CMP_PAGES * PAGE // CMP_STRIDE
CHUNKS_PER_PAGE = PAGE // CMP_STRIDE


def _compress_kernel(pt_ref, *refs):
    pages = refs[:CMP_PAGES]
    perm_ref, w1_ref, pb_ref, w2_ref, w2t_ref, b2_ref, b2c_ref, k_ref, kt_ref, vt_ref, carry_ref = refs[CMP_PAGES:]
    s = pl.program_id(1)

    @pl.when(s == 0)
    def _():
        carry_ref[...] = jnp.zeros_like(carry_ref)

    n = CMP_CHUNKS
    perm = perm_ref[...]
    rows_by_p = [_dot_nt(perm, pg[0].astype(BF16)) for pg in pages]
    row0 = lax.broadcasted_iota(jnp.int32, (n, 1), 0) == 0
    accs = []
    for t in range(KV_ROW // LANE):
        x = t // 2
        sl = slice(t * LANE, (t + 1) * LANE)
        acc = jnp.zeros((n, 4 * CMP_HIDDEN), F32)
        for pp in range(CMP_STRIDE // 2):
            parts = []
            for p in (2 * pp, 2 * pp + 1):
                parts.append(jnp.concatenate(
                    [r[p * CHUNKS_PER_PAGE:(p + 1) * CHUNKS_PER_PAGE, sl] for r in rows_by_p], axis=0))
            lhs = jnp.concatenate(parts, axis=1).astype(BF16)
            acc = acc + _dot(lhs, w1_ref[x, pp])
        accs.append(acc)
    for t, acc in enumerate(accs):
        x = t // 2
        sl = slice(t * LANE, (t + 1) * LANE)
        hid = []
        for hh in range(2):
            part0 = acc[:, hh * 256:hh * 256 + CMP_HIDDEN]
            part1 = acc[:, hh * 256 + CMP_HIDDEN:(hh + 1) * 256]
            csl = slice((t * 2 + hh) * CMP_HIDDEN, (t * 2 + hh + 1) * CMP_HIDDEN)
            prev = jnp.where(row0, carry_ref[0:1, csl], pltpu.roll(part0, 1, 0))
            carry_ref[0:1, csl] = part0[n - 1:n, :]
            hid.append(_silu(prev + part1 + pb_ref[x:x + 1, :]))
        hid = jnp.concatenate(hid, axis=1).astype(BF16)
        out_t = _dot_nt(w2t_ref[x], hid) + b2c_ref[x]
        rows = slice((t % 2) * LANE, (t % 2 + 1) * LANE)
        if x == 0:
            k_ref[0, :, sl] = _dot(hid, w2_ref[...]) + b2_ref[...]
            kt_ref[0, rows, :] = out_t
        else:
            vt_ref[0, rows, :] = out_t


def compress(pool_t, page_table, perm, w1t, pb, w2k, w2t, b2k, b2c):
    b, n_pages = page_table.shape
    steps = n_pages // CMP_PAGES
    n_blk = n_pages * CHUNKS_PER_PAGE
    page_spec = lambda k: pl.BlockSpec((1, KV_ROW, PAGE), lambda bi, si, pt, k=k: (pt[bi, si * CMP_PAGES + k], 0, 0))
    const = lambda a: pl.BlockSpec(a.shape, lambda bi, si, pt: (0,) * a.ndim)
    consts = (perm, w1t, pb, w2k, w2t, b2k, b2c)
    fm_spec = pl.BlockSpec((1, HALF_ROW, CMP_CHUNKS), lambda bi, si, pt: (bi, 0, si))
    fm_shape = jax.ShapeDtypeStruct((b, HALF_ROW, n_blk), F32)
    grid_spec = pltpu.PrefetchScalarGridSpec(
        num_scalar_prefetch=1,
        grid=(b, steps),
        in_specs=[page_spec(k) for k in range(CMP_PAGES)] + [const(a) for a in consts],
        out_specs=[pl.BlockSpec((1, CMP_CHUNKS, HALF_ROW), lambda bi, si, pt: (bi, si, 0)), fm_spec, fm_spec],
        scratch_shapes=[pltpu.VMEM((8, 8 * CMP_HIDDEN), F32)],
    )
    return pl.pallas_call(
        _compress_kernel,
        grid_spec=grid_spec,
        out_shape=[jax.ShapeDtypeStruct((b, n_blk, HALF_ROW), F32), fm_shape, fm_shape],
        compiler_params=_cparams(("parallel", "arbitrary")),
        name="compress",
    )(page_table, *([pool_t] * CMP_PAGES), *consts)


def _compress_weights(cmp_w1, cmp_w2, cmp_b2):
    w1 = cmp_w1.reshape(2, 2, CMP_STRIDE // 2, 2, HEAD_DIM, CMP_HIDDEN)
    w1 = jnp.transpose(w1, (0, 2, 3, 4, 1, 5))
    eye = jnp.eye(2, dtype=F32)
    w1t = jnp.einsum('xqpdje,hk->xqphdkje', w1, eye).reshape(2, CMP_STRIDE // 2, 256, 512).astype(BF16)
    w2bd = jnp.einsum('xed,hk->xhekd', cmp_w2, eye).reshape(2, 256, LANE).astype(BF16)
    b2t = jnp.concatenate([cmp_b2, cmp_b2], axis=1)
    r = np.arange(PAGE)
    perm = np.zeros((PAGE, PAGE), np.float32)
    perm[(r % CMP_STRIDE) * CHUNKS_PER_PAGE + r // CMP_STRIDE, r] = 1.0
    return (jnp.asarray(perm, dtype=BF16), w1t, w2bd[0], jnp.transpose(w2bd, (0, 2, 1)), b2t[0:1], b2t.reshape(2, LANE, 1))


def _masked_exp0(s, mask):
    s = jnp.where(mask, s, NEG)
    m = jnp.max(s, axis=0, keepdims=True)
    p = jnp.where(mask, jnp.exp2(s - m), 0.0)
    return p, jnp.maximum(jnp.sum(p, axis=0, keepdims=True), TINY)


def _softmax0(s, mask):
    p, denom = _masked_exp0(s, mask)
    return p / denom


def _split_dot(a, x):
    hi = x.astype(BF16)
    lo = (x - hi.astype(F32)).astype(BF16)
    return _dot(a, hi) + _dot(a, lo)


def _split_dot_r(x, a):
    hi = x.astype(BF16)
    lo = (x - hi.astype(F32)).astype(BF16)
    return _dot(hi, a) + _dot(lo, a)


def _top_blocks(imps, cur):
    blk = lax.broadcasted_iota(jnp.int32, (imps[0].shape[0], 1), 0)
    forced = (blk == 0) | ((blk <= cur) & (blk > cur - N_LOCAL))
    imps = tuple(jnp.where(forced, REMOVED, jnp.where(blk > cur, -BIG, imp)) for imp in imps)
    picked = jnp.where(forced, 1.0, 0.0)
    blk_f = blk.astype(F32)

    def pick(_, carry):
        out = []
        for imp, sel in carry:
            mx = jnp.max(imp, axis=0, keepdims=True)
            first = jnp.min(jnp.where(imp == mx, blk_f, 1e9), axis=0, keepdims=True)
            hit = blk_f == first
            out.append((jnp.where(hit, REMOVED, imp), jnp.where(hit, 1.0, sel)))
        return tuple(out)

    final = lax.fori_loop(0, N_SELECT - 1 - N_LOCAL, pick, tuple((imp, picked) for imp in imps), unroll=True)
    return [sel for _, sel in final]


def _flash_update_biased(state, s, v_aug):
    m, acc = state
    m_new = jnp.maximum(m, jnp.max(s, axis=0, keepdims=True))
    p = jnp.exp2(s - m_new)
    acc = jnp.exp2(m - m_new) * acc + _dot(v_aug, p.astype(BF16))
    return m_new, acc


def _cmp_mask(n_rows, pos_q):
    r = lax.broadcasted_iota(jnp.int32, (n_rows, 1), 0)
    return (r >= 1) & (r * CMP_STRIDE + (CMP_LEN - CMP_STRIDE - 1) <= pos_q)


WIN_KEYS = WINDOW + Q_BLOCK
P_HEADS = 4
BLOCKS_PER_TILE = SEL_TILE // SEL_BLOCK


def _nsa_prompt_kernel(qt_ref, kc_ref, vct_ref, ks_ref, vst_ref, kw_ref, vwt_ref, gt_ref, mt_ref, o_ref, sel_ref, s_ref):
    i = pl.program_id(1)
    cols = NSA_GROUP * Q_BLOCK
    lane = lax.broadcasted_iota(jnp.int32, (1, Q_BLOCK), 1)
    pos_q = i * Q_BLOCK + lane
    tile4 = lambda a: jnp.concatenate([a] * NSA_GROUP, axis=1)
    heads = range(P_HEADS)
    q_ts = []
    for h in heads:
        q_blk = qt_ref[h * NSA_GROUP * HEAD_DIM:(h + 1) * NSA_GROUP * HEAD_DIM, :]
        q_ts.append(jnp.concatenate([q_blk[g * HEAD_DIM:(g + 1) * HEAD_DIM, :] for g in range(NSA_GROUP)], axis=1))

    nc = kc_ref.shape[1]
    s_cmp = [_dot(kc_ref[h], q_ts[h]) for h in heads]
    s_win = [_dot(kw_ref[h, pl.ds(pl.multiple_of(i * Q_BLOCK, Q_BLOCK), WIN_KEYS), :], q_ts[h]) for h in heads]

    mask_c = tile4(_cmp_mask(nc, pos_q))
    o_c, imps = [], []
    for h in heads:
        p_c = _softmax0(s_cmp[h], mask_c)
        o_c.append(_dot(vct_ref[h], p_c.astype(BF16)))
        pg = p_c[:, 0:Q_BLOCK]
        for g in range(1, NSA_GROUP):
            pg = pg + p_c[:, g * Q_BLOCK:(g + 1) * Q_BLOCK]
        imps.append(_split_dot(mt_ref[...], pg))
    for h, sel in enumerate(_top_blocks(imps, pos_q // SEL_BLOCK)):
        sel_ref[h] = sel

    zeros_q = jnp.zeros((LANE - HEAD_DIM - 16, cols), BF16)

    def q_aug(h, j):
        grp = sel_ref[h, pl.ds(pl.multiple_of(j * BLOCKS_PER_TILE, BLOCKS_PER_TILE), BLOCKS_PER_TILE), :]
        bias = jnp.concatenate([jnp.where(grp > 0.5, 0.0, NEG), jnp.zeros_like(grp)], axis=0)
        return jnp.concatenate([q_ts[h], tile4(bias).astype(BF16), zeros_q], axis=0)

    def scores(h, j):
        return _dot(ks_ref[h, pl.ds(pl.multiple_of(j * SEL_TILE, SEL_TILE), SEL_TILE), :], q_aug(h, j))

    def sel_body(j, states):
        states = list(states)
        s_cur = s_ref[...]
        for h in heads:
            s_next = scores(h + 1, j) if h + 1 < P_HEADS else scores(0, j + 1)
            states[h] = _flash_update_biased(states[h], s_cur, vst_ref[h, j])
            s_cur = s_next
        s_ref[...] = s_cur
        return tuple(states)

    init = tuple((jnp.full((1, cols), NEG, F32), jnp.zeros((V_AUG, cols), F32)) for _ in heads)
    j_diag = (i * Q_BLOCK) // SEL_TILE
    s_ref[...] = scores(0, 0)
    states = lax.fori_loop(0, j_diag, sel_body, init)
    key_pos = j_diag * SEL_TILE + lax.broadcasted_iota(jnp.int32, (SEL_TILE, 1), 0)
    causal = tile4(jnp.where(key_pos <= pos_q, 0.0, NEG))
    last = [s_ref[...]] + [scores(h, j_diag) for h in heads[1:]]
    o_s = []
    for h in heads:
        _, acc = _flash_update_biased(states[h], last[h] + causal, vst_ref[h, j_diag])
        o_s.append(acc[:HEAD_DIM] / jnp.maximum(acc[HEAD_DIM:HEAD_DIM + 1], TINY))

    w_pos = i * Q_BLOCK - WINDOW + lax.broadcasted_iota(jnp.int32, (WIN_KEYS, 1), 0)
    mask_w = tile4((w_pos <= pos_q) & (w_pos > pos_q - WINDOW) & (w_pos >= 0))
    o_w = []
    for h in heads:
        p_w, denom = _masked_exp0(s_win[h], mask_w)
        p_w = p_w.astype(BF16)
        acc = jnp.zeros((HEAD_DIM, cols), F32)
        for w in range(WIN_KEYS // Q_BLOCK):
            acc = acc + _dot(vwt_ref[h, i + w], p_w[w * Q_BLOCK:(w + 1) * Q_BLOCK, :])
        o_w.append(acc / denom)

    for h in heads:
        gt = jax.nn.sigmoid(gt_ref[h])
        outs = []
        for g in range(NSA_GROUP):
            sl = slice(g * Q_BLOCK, (g + 1) * Q_BLOCK)
            outs.append(gt[3 * g:3 * g + 1, :] * o_c[h][:, sl] + gt[3 * g + 1:3 * g + 2, :] * o_s[h][:, sl]
                        + gt[3 * g + 2:3 * g + 3, :] * o_w[h][:, sl])
        for pair in range(NSA_GROUP // 2):
            both = jnp.concatenate([outs[2 * pair], outs[2 * pair + 1]], axis=0)
            lo = (h * NSA_GROUP // 2 + pair) * LANE
            o_ref[:, lo:lo + LANE] = both.T


def nsa_prompt(qt, kc, vct, ks, vst, kw, vwt, gt, mt, t):
    nq = t // Q_BLOCK
    head = lambda a: pl.BlockSpec((P_HEADS,) + a.shape[1:], lambda h, i: (h,) + (0,) * (a.ndim - 1),
                                  pipeline_mode=pl.Buffered(1))
    width = P_HEADS * NSA_GROUP * HEAD_DIM
    return pl.pallas_call(
        _nsa_prompt_kernel,
        grid=(NSA_KV_HEADS // P_HEADS, nq),
        in_specs=[pl.BlockSpec((width, Q_BLOCK), lambda h, i: (h, i)),
                  head(kc), head(vct), head(ks), head(vst), head(kw), head(vwt),
                  pl.BlockSpec((P_HEADS, 16, Q_BLOCK), lambda h, i: (h, 0, i)),
                  pl.BlockSpec(mt.shape, lambda h, i: (0, 0))],
        out_specs=pl.BlockSpec((Q_BLOCK, width), lambda h, i: (i, h)),
        out_shape=jax.ShapeDtypeStruct((t, NSA_WIDTH), F32),
        scratch_shapes=[pltpu.VMEM((P_HEADS, mt.shape[0], Q_BLOCK), F32),
                        pltpu.VMEM((SEL_TILE, NSA_GROUP * Q_BLOCK), F32)],
        compiler_params=_cparams(("parallel", "arbitrary")),
        name="nsa_p",
    )(qt, kc, vct, ks, vst, kw, vwt, gt, mt)


def _cmp_to_sel_t(n_rows, n_blk, n_blk_pad):
    cs = (np.arange(n_rows)[None, :] - 1) * CMP_STRIDE
    js = np.arange(n_blk_pad)[:, None] * SEL_BLOCK
    m = (cs < js + SEL_BLOCK) & (cs + CMP_LEN > js) & (np.arange(n_rows)[None, :] >= 1) & (np.arange(n_blk_pad)[:, None] < n_blk)
    return jnp.asarray(m.astype(np.float32), dtype=BF16)


S_PAGES = 16
S_COLS = NSA_HEADS * 8


def _softmax_rows(s, mask):
    s = jnp.where(mask, s, NEG)
    m = jnp.max(s, axis=1, keepdims=True)
    p = jnp.where(mask, jnp.exp2(s - m), 0.0)
    return p / jnp.maximum(jnp.sum(p, axis=1, keepdims=True), TINY)


def _flash_rows(state, s, v_t):
    m, l, acc = state
    m_new = jnp.maximum(m, jnp.max(s, axis=1, keepdims=True))
    p = jnp.exp2(s - m_new)
    alpha = jnp.exp2(m - m_new)
    return m_new, alpha * l + jnp.sum(p, axis=1, keepdims=True), alpha * acc + _dot_nt(p.astype(BF16), v_t)


def _nsa_sample_kernel(n_steps, pt_ref, *refs):
    pages = refs[:S_PAGES]
    (qb_ref, kct_ref, vct_ref, cw_ref, nw_ref, ns_ref, g_ref, mt_ref, gsum_ref, emat_ref, o_ref,
     sel_ref, m_ref, l_ref, acc_ref, oc_ref, ow_ref) = refs[S_PAGES:]
    s_id = pl.program_id(1)
    past = n_steps * S_PAGES * PAGE
    qb = qb_ref[0]
    pos_q = past + lax.broadcasted_iota(jnp.int32, (S_COLS, 1), 0) % 8
    blocks_per_step = S_PAGES * PAGE // SEL_BLOCK

    def block_bias(grp, n_keys):
        flags = jnp.concatenate([grp, jnp.zeros((LANE - grp.shape[0], S_COLS), F32)], axis=0).T
        return _dot(jnp.where(flags > 0.5, 0.0, NEG).astype(BF16), emat_ref[:, :n_keys])

    @pl.when(s_id == 0)
    def _():
        n_ent = kct_ref.shape[2]
        wb = cw_ref.shape[2]
        k_win = jnp.concatenate([cw_ref[0, :HALF_ROW, :], nw_ref[0, :HALF_ROW, :]], axis=1).astype(BF16)
        s = _dot(qb, kct_ref[0].astype(BF16))
        s_win = _dot(qb, k_win)
        ent = lax.broadcasted_iota(jnp.int32, (1, n_ent), 1)
        p_c = _softmax_rows(s, (ent >= 1) & (ent * CMP_STRIDE + (CMP_LEN - CMP_STRIDE - 1) <= pos_q))
        oc_ref[...] = _dot_nt(p_c.astype(BF16), vct_ref[0].astype(BF16))
        imp = _split_dot(mt_ref[...], p_c.T)
        imp = _split_dot_r(imp, gsum_ref[...])
        pos_row = past + lax.broadcasted_iota(jnp.int32, (1, S_COLS), 1) % 8
        sel_ref[...] = _top_blocks([imp], pos_row // SEL_BLOCK)[0]

        v_t = jnp.concatenate([cw_ref[0, HALF_ROW:, :], nw_ref[0, HALF_ROW:, :]], axis=1).astype(BF16)
        w_pos = past - wb + lax.broadcasted_iota(jnp.int32, (1, wb + PAGE), 1)
        p_w = _softmax_rows(s_win, (w_pos <= pos_q) & (w_pos > pos_q - WINDOW) & (w_pos >= 0))
        ow_ref[...] = _dot_nt(p_w.astype(BF16), v_t)

        nblk0 = past // SEL_BLOCK
        key_pos = past + lax.broadcasted_iota(jnp.int32, (1, PAGE), 1)
        s = (_dot(qb, ns_ref[0, :HALF_ROW, :].astype(BF16)) + block_bias(sel_ref[nblk0:nblk0 + 8, :], PAGE)
             + jnp.where(key_pos <= pos_q, 0.0, NEG))
        init = (jnp.full((S_COLS, 1), NEG, F32), jnp.zeros((S_COLS, 1), F32), jnp.zeros((S_COLS, HALF_ROW), F32))
        m_ref[...], l_ref[...], acc_ref[...] = _flash_rows(init, s, ns_ref[0, HALF_ROW:, :].astype(BF16))

    grp = sel_ref[pl.ds(pl.multiple_of(s_id * blocks_per_step, blocks_per_step), blocks_per_step), :]
    k_t = jnp.concatenate([pg[0, :HALF_ROW, :] for pg in pages], axis=1).astype(BF16)
    v_t = jnp.concatenate([pg[0, HALF_ROW:, :] for pg in pages], axis=1).astype(BF16)
    s = _dot(qb, k_t) + block_bias(grp, S_PAGES * PAGE)
    st = _flash_rows((m_ref[...], l_ref[...], acc_ref[...]), s, v_t)
    m_ref[...], l_ref[...], acc_ref[...] = st

    @pl.when(s_id == n_steps - 1)
    def _():
        g = jax.nn.sigmoid(g_ref[0])
        o_s = st[2] / jnp.maximum(st[1], TINY)
        o_ref[0] = g[:, 0:1] * oc_ref[...] + g[:, 1:2] * o_s + g[:, 2:3] * ow_ref[...]


def nsa_sample(pool_t, page_table, qb, kct, vct, cache_wt, new_w, new_s, g, mt, gsum, emat):
    b, n_pages = page_table.shape
    steps = n_pages // S_PAGES
    page_spec = lambda k: pl.BlockSpec((1, KV_ROW, PAGE), lambda bi, si, pt, k=k: (pt[bi, si * S_PAGES + k], 0, 0))
    per_b = lambda a: pl.BlockSpec((1,) + a.shape[1:], lambda bi, si, pt: (bi,) + (0,) * (a.ndim - 1))
    const = lambda a: pl.BlockSpec(a.shape, lambda bi, si, pt: (0,) * a.ndim)
    grid_spec = pltpu.PrefetchScalarGridSpec(
        num_scalar_prefetch=1,
        grid=(b, steps),
        in_specs=[page_spec(k) for k in range(S_PAGES)] + [
            per_b(qb), per_b(kct), per_b(vct), per_b(cache_wt), per_b(new_w), per_b(new_s), per_b(g),
            const(mt), const(gsum), const(emat)],
        out_specs=pl.BlockSpec((1, S_COLS, HALF_ROW), lambda bi, si, pt: (bi, 0, 0)),
        scratch_shapes=[pltpu.VMEM((mt.shape[0], S_COLS), F32), pltpu.VMEM((S_COLS, 1), F32), pltpu.VMEM((S_COLS, 1), F32),
                        pltpu.VMEM((S_COLS, HALF_ROW), F32), pltpu.VMEM((S_COLS, HALF_ROW), F32),
                        pltpu.VMEM((S_COLS, HALF_ROW), F32)],
    )
    return pl.pallas_call(
        functools.partial(_nsa_sample_kernel, steps),
        grid_spec=grid_spec,
        out_shape=jax.ShapeDtypeStruct((b, S_COLS, HALF_ROW), F32),
        compiler_params=_cparams(("parallel", "arbitrary")),
        name="nsa_s",
    )(page_table, *([pool_t] * S_PAGES), qb, kct, vct, cache_wt, new_w, new_s, g, mt, gsum, emat)


GLA_SUB = 16


def _gla_head(q, k, v, cum, state):
    c = q.shape[0]
    sub = min(GLA_SUB, c)
    lane = lax.broadcasted_iota(jnp.int32, (1, LANE), 1)
    t_sub = lax.broadcasted_iota(jnp.int32, (sub, 1), 0)
    row_pad = lambda a: jnp.concatenate([a, jnp.zeros((LANE - c, a.shape[1]), F32)], axis=0).astype(BF16)
    v_pad = row_pad(v)
    o = _dot((q * jnp.exp(cum)).astype(BF16), state.astype(BF16))
    blocks = []
    for r0 in range(0, c, sub):
        q_i, cum_i = q[r0:r0 + sub], cum[r0:r0 + sub]
        if r0 == 0:
            att_i = jnp.zeros((sub, LANE), F32)
        else:
            base = cum[r0 - 1:r0]
            q_dec = (q_i * jnp.exp(cum_i - base)).astype(BF16)
            k_dec = row_pad(k * jnp.exp(jnp.minimum(base - cum, 0.0)))
            att_i = jnp.where(lane < r0, _dot_nt(q_dec, k_dec), 0.0)
        for s in range(r0, r0 + sub):
            decay = jnp.exp(jnp.where(t_sub >= s - r0, cum_i - cum[s:s + 1], NEG))
            column = jnp.sum(q_i * k[s:s + 1] * decay, axis=-1, keepdims=True)
            att_i = jnp.where(lane == s, column, att_i)
        blocks.append(att_i)
    att = jnp.concatenate(blocks, axis=0)
    o = o + _dot(att.astype(BF16), v_pad)

    c_last = cum[c - 1:c]
    k_end = jnp.concatenate([k * jnp.exp(c_last - cum), jnp.zeros((LANE - c, GLA_DK), F32)], axis=0)
    eye = lax.broadcasted_iota(jnp.int32, (GLA_DK, GLA_DK), 0) == lax.broadcasted_iota(jnp.int32, (GLA_DK, GLA_DK), 1)
    decay_col = jnp.sum(jnp.where(eye, jnp.exp(c_last), 0.0), axis=1, keepdims=True)
    return o, decay_col * state + _dot(k_end.T.astype(BF16), v_pad)


def _gla_kernel(q_ref, k_ref, v_ref, a_ref, z_ref, wa_ref, ba_ref, nw_ref, s0_ref, o_ref, so_ref, s_ref):
    c = q_ref.shape[0]
    ci = pl.program_id(1)

    @pl.when(ci == 0)
    def _():
        s_ref[...] = s0_ref[0]

    pre = _dot(a_ref[...].astype(BF16), wa_ref[...].astype(BF16)) + ba_ref[...]
    log_a = (jnp.minimum(pre, 0.0) - jnp.log1p(jnp.exp(-jnp.abs(pre)))) / GLA_TAU
    t_idx = lax.broadcasted_iota(jnp.int32, (c, 1), 0)
    cum = log_a
    sh = 1
    while sh < c:
        cum = cum + jnp.where(t_idx >= sh, pltpu.roll(cum, sh, 0), 0.0)
        sh *= 2
    for h in range(GLA_HEADS):
        ks = slice(h * GLA_DK, (h + 1) * GLA_DK)
        vs = slice(h * GLA_DV, (h + 1) * GLA_DV)
        o, new_state = _gla_head(q_ref[:, ks] * (GLA_DK ** -0.5), k_ref[:, ks], v_ref[:, vs], cum[:, ks], s_ref[h])
        s_ref[h] = new_state
        y = o * lax.rsqrt(jnp.mean(o * o, axis=-1, keepdims=True) + EPS) * nw_ref[...]
        o_ref[:, vs] = y * _silu(z_ref[:, vs])

    @pl.when(ci == pl.num_programs(1) - 1)
    def _():
        so_ref[0] = s_ref[...]


def gla(proj, w_a2p, b_a, gla_norm_w, s0, n_seq, chunk):
    rows = proj.shape[0]
    n_chunk = rows // (n_seq * chunk)
    kw, vw = GLA_HEADS * GLA_DK, GLA_HEADS * GLA_DV
    rows_at = lambda width, col: pl.BlockSpec((chunk, width), lambda b, c: (b * n_chunk + c, col // width))
    state_spec = pl.BlockSpec((1, GLA_HEADS, GLA_DK, GLA_DV), lambda b, c: (b, 0, 0, 0))
    return pl.pallas_call(
        _gla_kernel,
        grid=(n_seq, n_chunk),
        in_specs=[rows_at(kw, C_QG), rows_at(kw, C_KG), rows_at(vw, C_VG), rows_at(LANE, C_AG), rows_at(vw, C_ZG),
                  pl.BlockSpec((LANE, kw), lambda b, c: (0, 0)),
                  pl.BlockSpec((1, kw), lambda b, c: (0, 0)),
                  pl.BlockSpec((1, GLA_DV), lambda b, c: (0, 0)),
                  state_spec],
        out_specs=[pl.BlockSpec((chunk, vw), lambda b, c: (b * n_chunk + c, 0)), state_spec],
        out_shape=[jax.ShapeDtypeStruct((rows, vw), F32),
                   jax.ShapeDtypeStruct((n_seq, GLA_HEADS, GLA_DK, GLA_DV), F32)],
        scratch_shapes=[pltpu.VMEM((GLA_HEADS, GLA_DK, GLA_DV), F32)],
        compiler_params=_cparams(("parallel", "arbitrary")),
        name="gla",
    )(proj, proj, proj, proj, proj, w_a2p, b_a.reshape(1, -1), gla_norm_w.reshape(1, -1), s0)


def _out_kernel(on_ref, zn_ref, og_ref, mn_ref, mg_ref, x_ref, gate_ref, wn_ref, wg_ref, wo_ref, fw_ref, y_ref):
    o_nsa = (on_ref[...] * _silu(zn_ref[...])).astype(BF16)
    merged = (jax.nn.sigmoid(mn_ref[...]) * _dot(o_nsa, wn_ref[...])
              + jax.nn.sigmoid(mg_ref[...]) * _dot(og_ref[...].astype(BF16), wg_ref[...]))
    y = x_ref[...] + gate_ref[...] * _dot(merged.astype(BF16), wo_ref[...])
    y_ref[...] = y * lax.rsqrt(jnp.mean(y * y, axis=-1, keepdims=True) + EPS) * fw_ref[...]


def out_proj(o_nsa, o_gla, proj, x, gate, w_o_nsa, w_o_gla, w_out, final_norm_w, tm):
    rows = x.shape[0]
    per_row = gate.shape[0] != 1
    gate_spec = pl.BlockSpec((tm, D_MODEL), lambda i: (i, 0)) if per_row else pl.BlockSpec((1, D_MODEL), lambda i: (0, 0))
    resident = lambda a: pl.BlockSpec(a.shape, lambda i: (0, 0), pipeline_mode=pl.Buffered(1))
    return pl.pallas_call(
        _out_kernel,
        grid=(rows // tm,),
        in_specs=[pl.BlockSpec((tm, NSA_WIDTH), lambda i: (i, 0)),
                  pl.BlockSpec((tm, NSA_WIDTH), lambda i: (i, C_ZN // NSA_WIDTH)),
                  pl.BlockSpec((tm, NSA_WIDTH), lambda i: (i, 0)),
                  pl.BlockSpec((tm, D_MODEL), lambda i: (i, C_MN // D_MODEL)),
                  pl.BlockSpec((tm, D_MODEL), lambda i: (i, C_MG // D_MODEL)),
                  pl.BlockSpec((tm, D_MODEL), lambda i: (i, 0)),
                  gate_spec, resident(w_o_nsa), resident(w_o_gla), resident(w_out),
                  pl.BlockSpec((1, D_MODEL), lambda i: (0, 0))],
        out_specs=pl.BlockSpec((tm, D_MODEL), lambda i: (i, 0)),
        out_shape=jax.ShapeDtypeStruct((rows, D_MODEL), F32),
        compiler_params=_cparams(("parallel",)),
        name="outproj",
    )(o_nsa, proj, o_gla, proj, proj, x, gate, w_o_nsa, w_o_gla, w_out, final_norm_w.reshape(1, D_MODEL))


def _feature_major(a):
    lead = a.shape[:-4]
    n = len(lead)
    a = jnp.transpose(a, tuple(range(n)) + (n + 1, n + 2, n + 3, n))
    return a.reshape(lead + (KV_ROW, a.shape[-1]))


def _token_major(a_t, lead):
    rows = a_t.shape[-1]
    a = a_t.reshape(a_t.shape[:-2] + (2, NSA_KV_HEADS, HEAD_DIM, rows))
    n = a.ndim - 4
    a = jnp.transpose(a, tuple(range(n)) + (n + 3, n, n + 1, n + 2))
    return a.reshape(lead + (rows, 2, NSA_KV_HEADS, HEAD_DIM))


def kernel(x_prompt, x_sample, cache_kv_cmp, cache_kv_sel, cache_kv_win, state_gla, page_table, c_prompt, c_sample, norm_w, w_ada, b_ada, w_in, cmp_pos, cmp_w1, cmp_b1, cmp_w2, cmp_b2, w_a2, b_a, gla_norm_w, w_o_nsa, w_o_gla, w_out, final_norm_w):
    assert x_prompt.shape[0] == 1 and norm_w.shape[0] == 1, "one prompt sequence, one layer"
    t_p = x_prompt.shape[1]
    b_s, t_s = x_sample.shape[:2]
    past = page_table.shape[1] * PAGE
    wb = cache_kv_win.shape[2]
    assert t_s == 8 and wb == WINDOW and past % (S_PAGES * PAGE) == 0 and t_p % SEL_TILE == 0
    rows_s = b_s * t_s

    c_rows = jnp.zeros((40, D_MODEL), F32).at[0:1].set(c_prompt).at[1:1 + b_s].set(c_sample)
    mod = ada_mod(c_rows, w_ada[0], b_ada[0])
    shift, scale, gate = mod[:, :D_MODEL], mod[:, D_MODEL:2 * D_MODEL], mod[:, 2 * D_MODEL:]
    per_row = lambda a: jnp.repeat(a[1:1 + b_s], t_s, axis=0)

    w_t = jnp.transpose(w_in[0])
    xp = x_prompt.reshape(t_p, D_MODEL)
    xs = x_sample.reshape(rows_s, D_MODEL)
    h_p = modulated_norm(xp, scale[0:1], shift[0:1], norm_w[0], 512)
    h_s = modulated_norm(xs, per_row(scale), per_row(shift), norm_w[0], rows_s)
    proj_p = in_proj(h_p, w_t, RM_OFFSETS, 1024, RM_TILE, False)
    projt_p = in_proj(h_p, w_t, FM_OFFSETS, 512, FM_TILE, True)
    proj_s = in_proj(h_s, w_t, RM_OFFSETS, rows_s, RM_TILE, False)
    projt_s = in_proj(h_s, w_t, FM_OFFSETS, rows_s, FM_TILE, True)

    cos_p, sin_p = _rope_tables(jnp.arange(t_p, dtype=jnp.int32))
    cos_s, sin_s = _rope_tables(jnp.tile(past + jnp.arange(t_s, dtype=jnp.int32), b_s))
    qt_p, kvc_p, kvs_p, kvw_p, ks_p, kw_p, vst_p, vwt_p = rope_stage(projt_p, cos_p, sin_p, 512, True)
    qt_s, kvc_s, kvs_s, kvw_s = rope_stage(projt_s, cos_s, sin_s, rows_s, False)

    pb = pos_bias(cmp_pos[0], cmp_w1[0], cmp_b1[0])
    cmp_consts = _compress_weights(cmp_w1[0], cmp_w2[0], cmp_b2[0])
    perm, w1t, w2k, w2t, b2k, b2c = cmp_consts
    ident = jnp.arange(t_p // PAGE, dtype=jnp.int32)[None, :]
    pages_p = jnp.transpose(kvc_p.reshape(KV_ROW, t_p // PAGE, PAGE), (1, 0, 2))
    kc_p, _, vct_p = compress(pages_p, ident, perm, w1t, pb, w2k, w2t, b2k, b2c)
    pool_c = _feature_major(cache_kv_cmp[0])
    _, kct_s, vct_s = compress(pool_c, page_table, perm, w1t, pb, w2k, w2t, b2k, b2c)

    n_ent = kc_p.shape[1]
    kc_h = jnp.transpose(kc_p[0].reshape(n_ent, NSA_KV_HEADS, HEAD_DIM), (1, 0, 2)).astype(BF16)
    vct_h = vct_p[0].reshape(NSA_KV_HEADS, HEAD_DIM, n_ent).astype(BF16)
    kw_h = jnp.pad(kw_p, ((0, 0), (WINDOW, 0), (0, 0)))
    vwt_h = jnp.pad(vwt_p, ((0, 0), (WINDOW // Q_BLOCK, 0), (0, 0), (0, 0)))
    g_p = projt_p[R_GN:R_GN + 48].reshape(NSA_KV_HEADS, 12, t_p)
    g_p = jnp.pad(g_p, ((0, 0), (0, 4), (0, 0)))
    mt_p = _cmp_to_sel_t(n_ent, t_p // SEL_BLOCK, t_p // SEL_BLOCK)
    o_nsa_p = nsa_prompt(qt_p, kc_h, vct_h, ks_p, vst_p, kw_h, vwt_h, g_p, mt_p, t_p)

    q5 = qt_s.reshape(NSA_KV_HEADS, NSA_GROUP, HEAD_DIM, b_s, t_s)
    q_c = jnp.transpose(q5, (3, 0, 1, 4, 2)).reshape(b_s, NSA_KV_HEADS, NSA_GROUP * t_s, HEAD_DIM)
    eye = jnp.eye(NSA_KV_HEADS, dtype=BF16)
    qb = jnp.einsum('bhcd,hk->bhckd', q_c, eye).reshape(b_s, S_COLS, HALF_ROW)
    new_keys = lambda a_t: jnp.pad(jnp.transpose(a_t.reshape(KV_ROW, b_s, t_s), (1, 0, 2)), ((0, 0), (0, 0), (0, PAGE - t_s)))
    g_s = projt_s[R_GN:R_GN + 48].reshape(NSA_KV_HEADS, NSA_GROUP, 3, b_s, t_s)
    g_s = jnp.transpose(g_s, (3, 0, 1, 4, 2)).reshape(b_s, S_COLS, 3)
    g_s = jnp.pad(g_s, ((0, 0), (0, 0), (0, 5)))
    n_blk_s = -(-(past + t_s) // SEL_BLOCK)
    mt_s = _cmp_to_sel_t(kct_s.shape[2], n_blk_s, -(-n_blk_s // 8) * 8)
    col = np.arange(S_COLS)
    gsum = jnp.asarray(((col[:, None] // 32 == col[None, :] // 32) & (col[:, None] % 8 == col[None, :] % 8)).astype(np.float32), dtype=BF16)
    emat = jnp.asarray((np.arange(S_PAGES * PAGE)[None, :] // SEL_BLOCK == np.arange(LANE)[:, None]).astype(np.float32), dtype=BF16)
    cache_wt = _feature_major(cache_kv_win[0])
    o_all = nsa_sample(_feature_major(cache_kv_sel[0]), page_table, qb, kct_s, vct_s, cache_wt,
                       new_keys(kvw_s), new_keys(kvs_s), g_s, mt_s, gsum, emat)
    o6 = o_all.reshape(b_s, NSA_KV_HEADS, NSA_GROUP, t_s, NSA_KV_HEADS, HEAD_DIM)
    o_nsa_s = jnp.stack([o6[:, h, :, :, h, :] for h in range(NSA_KV_HEADS)], axis=1)
    o_nsa_s = jnp.transpose(o_nsa_s, (0, 3, 1, 2, 4)).reshape(rows_s, NSA_WIDTH)

    w_a2p = jnp.zeros((LANE, GLA_HEADS * GLA_DK), F32).at[:GLA_RANK].set(w_a2[0])
    s0_p = jnp.zeros((1, GLA_HEADS, GLA_DK, GLA_DV), F32)
    o_gla_p, st_p = gla(proj_p, w_a2p, b_a[0], gla_norm_w[0], s0_p, 1, GLA_CHUNK)
    o_gla_s, st_s = gla(proj_s, w_a2p, b_a[0], gla_norm_w[0], state_gla[0], b_s, t_s)

    wn, wg, wo = w_o_nsa[0].astype(BF16), w_o_gla[0].astype(BF16), w_out[0].astype(BF16)
    y_p = out_proj(o_nsa_p, o_gla_p, proj_p, xp, gate[0:1], wn, wg, wo, final_norm_w, 256)
    y_s = out_proj(o_nsa_s, o_gla_s, proj_s, xs, per_row(gate), wn, wg, wo, final_norm_w, rows_s)

    sample_rows = lambda a_t: _token_major(jnp.transpose(a_t.reshape(KV_ROW, b_s, t_s), (1, 0, 2)), (1, b_s))
    win_t = jnp.concatenate([cache_wt, jnp.transpose(kvw_s.reshape(KV_ROW, b_s, t_s), (1, 0, 2))], axis=2)[:, :, t_s:]
    n_win = min(WINDOW, t_p)
    return (y_p.reshape(x_prompt.shape), y_s.reshape(x_sample.shape),
            _token_major(kvc_p, (1, 1)), sample_rows(kvc_s), _token_major(kvs_p, (1, 1)), sample_rows(kvs_s),
            _token_major(kvw_p[:, t_p - n_win:], (1, 1)), _token_major(win_t, (1, b_s)),
            st_p[None], st_s[None])
```

```python
import functools

import jax
import jax.numpy as jnp
import numpy as np
from jax import lax
from jax.experimental import pallas as pl
from jax.experimental.pallas import tpu as pltpu

F32 = jnp.float32
BF16 = jnp.bfloat16

D_MODEL = 2048
HEAD_DIM = 64
NSA_HEADS = 16
NSA_KV_HEADS = 4
NSA_GROUP = 4
NSA_WIDTH = 1024
HALF_ROW = NSA_KV_HEADS * HEAD_DIM
KV_ROW = 2 * HALF_ROW
CMP_LEN = 32
CMP_STRIDE = 16
CMP_HIDDEN = 128
SEL_BLOCK = 64
N_SELECT = 16
N_LOCAL = 2
WINDOW = 512
Q_BLOCK = 128
PAGE = 128
GLA_HEADS = 4
GLA_DK = 128
GLA_DV = 256
GLA_RANK = 16
GLA_TAU = 16.0
GLA_CHUNK = 64
ROPE_THETA = 10000.0
EPS = 1e-6
NEG = -1e30
BIG = 1e30
TINY = 1e-30
REMOVED = -3e38
LOG2E = 1.4426950408889634
SEL_TILE = 512
V_AUG = HEAD_DIM + 16

LANE = 128
VMEM_LIMIT = 48 * 1024 * 1024

(W_Q, W_KV, W_GN, W_ZN, W_QG, W_KG, W_VG, W_AG, W_ZG, W_MN, W_MG) = (
    0, 1024, 2560, 2608, 3632, 4144, 4656, 5680, 5696, 6720, 8768)
RM_TILE = 1024
W_ALIGN = 16
C_MN, C_MG, C_ZN, C_VG, C_ZG, C_QG, C_KG, C_AG = 0, 2048, 4096, 5120, 6144, 7168, 7680, 8192
assert W_KG == W_QG + RM_TILE // 2
RM_SOURCES = ((W_MN, 2), (W_MG, 2), (W_ZN, 1), (W_VG, 1), (W_ZG, 1), (W_QG, 1), (W_AG, 1))
RM_OFFSETS = tuple(start + RM_TILE * k for start, tiles in RM_SOURCES for k in range(tiles))
RM_COLS = RM_TILE * len(RM_OFFSETS)
R_Q, R_KV, R_GN = W_Q, W_KV, W_GN
FM_TILE = 1344
FM_ROWS = 2 * FM_TILE
FM_OFFSETS = (0, FM_TILE)


def _row_tiles(rows):
    big = rows >= 1024
    return dict(norm=512 if big else rows, proj_rm=1024 if big else rows, proj_fm=512 if big else rows,
                rope=SEL_TILE if big else rows, out=256 if big else rows)


def _cparams(sem):
    return pltpu.CompilerParams(dimension_semantics=sem, vmem_limit_bytes=VMEM_LIMIT)


def _dot(a, b):
    return jnp.dot(a, b, preferred_element_type=F32)


def _dot_nt(a, b):
    return lax.dot_general(a, b, (((1,), (1,)), ((), ())), preferred_element_type=F32)


def _silu(x):
    return x * jax.nn.sigmoid(x)


def _ada_kernel(c_ref, w_ref, b_ref, o_ref):
    o_ref[...] = _dot(c_ref[...].astype(BF16), w_ref[...].astype(BF16)) + b_ref[...]


def ada_mod(c_rows, w_ada, b_ada):
    rows, tn = c_rows.shape[0], 512
    n = w_ada.shape[1]
    return pl.pallas_call(
        _ada_kernel,
        grid=(n // tn,),
        in_specs=[pl.BlockSpec((rows, D_MODEL), lambda j: (0, 0)),
                  pl.BlockSpec((D_MODEL, tn), lambda j: (0, j)),
                  pl.BlockSpec((1, tn), lambda j: (0, j))],
        out_specs=pl.BlockSpec((rows, tn), lambda j: (0, j)),
        out_shape=jax.ShapeDtypeStruct((rows, n), F32),
        compiler_params=_cparams(("parallel",)),
        name="ada",
    )(c_rows, w_ada, b_ada.reshape(1, n))


def _norm_kernel(x_ref, sc_ref, sh_ref, nw_ref, h_ref):
    x = x_ref[...]
    y = x * lax.rsqrt(jnp.mean(x * x, axis=-1, keepdims=True) + EPS) * nw_ref[...]
    h_ref[...] = (y * (1.0 + sc_ref[...]) + sh_ref[...]).astype(BF16)


def modulated_norm(x, scale, shift, norm_w, tm):
    rows = x.shape[0]
    per_row = scale.shape[0] != 1
    mod_spec = pl.BlockSpec((tm, D_MODEL), lambda i: (i, 0)) if per_row else pl.BlockSpec((1, D_MODEL), lambda i: (0, 0))
    return pl.pallas_call(
        _norm_kernel,
        grid=(rows // tm,),
        in_specs=[pl.BlockSpec((tm, D_MODEL), lambda i: (i, 0)), mod_spec, mod_spec,
                  pl.BlockSpec((1, D_MODEL), lambda i: (0, 0))],
        out_specs=pl.BlockSpec((tm, D_MODEL), lambda i: (i, 0)),
        out_shape=jax.ShapeDtypeStruct((rows, D_MODEL), BF16),
        compiler_params=_cparams(("parallel",)),
        name="norm",
    )(x, scale, shift, norm_w.reshape(1, D_MODEL))


def _inproj_rm_kernel(off_ref, h_ref, w_ref, o_ref, wb_ref):
    @pl.when(pl.program_id(1) == 0)
    def _():
        wb_ref[...] = w_ref[...].astype(BF16)

    o_ref[...] = _dot_nt(h_ref[...], wb_ref[...])


def _inproj_fm_kernel(off_ref, h_ref, w_ref, o_ref, wb_ref):
    @pl.when(pl.program_id(1) == 0)
    def _():
        wb_ref[...] = w_ref[...].astype(BF16)

    o_ref[...] = _dot_nt(wb_ref[...], h_ref[...])


def in_proj(h, w_t, row_offsets, tm, tn, feature_major):
    rows, n = h.shape[0], len(row_offsets) * tn
    if feature_major:
        body, out_spec, out_shape = _inproj_fm_kernel, pl.BlockSpec((tn, tm), lambda j, i, off: (j, i)), (n, rows)
    else:
        body, out_spec, out_shape = _inproj_rm_kernel, pl.BlockSpec((tm, tn), lambda j, i, off: (i, j)), (rows, n)
    grid_spec = pltpu.PrefetchScalarGridSpec(
        num_scalar_prefetch=1,
        grid=(len(row_offsets), rows // tm),
        in_specs=[pl.BlockSpec((tm, D_MODEL), lambda j, i, off: (i, 0)),
                  pl.BlockSpec((pl.Element(tn), pl.Element(D_MODEL)), lambda j, i, off: (off[j] * W_ALIGN, 0))],
        out_specs=out_spec,
        scratch_shapes=[pltpu.VMEM((tn, D_MODEL), BF16)],
    )
    return pl.pallas_call(
        body,
        grid_spec=grid_spec,
        out_shape=jax.ShapeDtypeStruct(out_shape, F32),
        compiler_params=_cparams(("parallel", "arbitrary")),
        name="inproj_fm" if feature_major else "inproj_rm",
    )(jnp.asarray([o // W_ALIGN for o in row_offsets], jnp.int32), h, w_t)


def _rope_kernel(q_ref, c_ref, s_ref, w_ref, cos_ref, sin_ref, qo_ref, co_ref, so_ref, wo_ref, *tile_refs):
    cos, sin = cos_ref[...], sin_ref[...]
    hh = HEAD_DIM // 2
    tr = cos.shape[1]

    def rot(src, head):
        x1 = src[head * HEAD_DIM:head * HEAD_DIM + hh, :]
        x2 = src[head * HEAD_DIM + hh:(head + 1) * HEAD_DIM, :]
        return x1 * cos - x2 * sin, x2 * cos + x1 * sin

    q_scale = HEAD_DIM ** -0.5 * LOG2E
    for head in range(NSA_HEADS):
        o1, o2 = rot(q_ref, head)
        qo_ref[head * HEAD_DIM:head * HEAD_DIM + hh, :] = (o1 * q_scale).astype(BF16)
        qo_ref[head * HEAD_DIM + hh:(head + 1) * HEAD_DIM, :] = (o2 * q_scale).astype(BF16)
    for src, dst in ((c_ref, co_ref), (s_ref, so_ref), (w_ref, wo_ref)):
        for head in range(NSA_KV_HEADS):
            o1, o2 = rot(src, head)
            dst[head * HEAD_DIM:head * HEAD_DIM + hh, :] = o1
            dst[head * HEAD_DIM + hh:(head + 1) * HEAD_DIM, :] = o2
        dst[HALF_ROW:, :] = src[HALF_ROW:, :]
    if tile_refs:
        ks_ref, kw_ref, vs_ref, vw_ref = tile_refs
        lane = lax.broadcasted_iota(jnp.int32, (1, LANE), 1)
        r = lax.broadcasted_iota(jnp.int32, (tr, 1), 0)
        onehot = jnp.where(lane - HEAD_DIM == (r // SEL_BLOCK) % (SEL_TILE // SEL_BLOCK), 1.0, 0.0)
        ones_row = jnp.where(lax.broadcasted_iota(jnp.int32, (V_AUG - HEAD_DIM, SEL_TILE), 0) == 0, 1.0, 0.0)
        for pair in range(NSA_KV_HEADS // 2):
            k_pair = so_ref[pair * LANE:(pair + 1) * LANE, :].T
            ks_ref[2 * pair] = jnp.where(lane < HEAD_DIM, k_pair, onehot).astype(BF16)
            ks_ref[2 * pair + 1] = jnp.where(lane < HEAD_DIM, pltpu.roll(k_pair, HEAD_DIM, 1), onehot).astype(BF16)
            kw_pair = wo_ref[pair * LANE:(pair + 1) * LANE, :].T.astype(BF16)
            kw_ref[2 * pair] = kw_pair[:, :HEAD_DIM]
            kw_ref[2 * pair + 1] = kw_pair[:, HEAD_DIM:]
        for head in range(NSA_KV_HEADS):
            rows = slice(HALF_ROW + head * HEAD_DIM, HALF_ROW + (head + 1) * HEAD_DIM)
            v = so_ref[rows, :]
            for w in range(tr // SEL_TILE):
                vs_ref[head, w] = jnp.concatenate([v[:, w * SEL_TILE:(w + 1) * SEL_TILE], ones_row], axis=0).astype(BF16)
            v = wo_ref[rows, :].astype(BF16)
            for w in range(tr // Q_BLOCK):
                vw_ref[head, w] = v[:, w * Q_BLOCK:(w + 1) * Q_BLOCK]


def rope_stage(proj_t, cos_t, sin_t, tr, with_tiles):
    tok = proj_t.shape[1]
    kv_spec = lambda k: pl.BlockSpec((KV_ROW, tr), lambda i, k=k: (R_KV // KV_ROW + k, i))
    out_kv = jax.ShapeDtypeStruct((KV_ROW, tok), F32)
    tab = pl.BlockSpec((HEAD_DIM // 2, tr), lambda i: (0, i))
    out_specs = [pl.BlockSpec((NSA_WIDTH, tr), lambda i: (0, i))] + [pl.BlockSpec((KV_ROW, tr), lambda i: (0, i))] * 3
    out_shape = [jax.ShapeDtypeStruct((NSA_WIDTH, tok), BF16), out_kv, out_kv, out_kv]
    if with_tiles:
        k_rows = lambda width: jax.ShapeDtypeStruct((NSA_KV_HEADS, tok, width), BF16)
        k_spec = lambda width: pl.BlockSpec((NSA_KV_HEADS, tr, width), lambda i: (0, i, 0))
        v_tiles = lambda rows, tile: jax.ShapeDtypeStruct((NSA_KV_HEADS, tok // tile, rows, tile), BF16)
        v_spec = lambda rows, tile: pl.BlockSpec((NSA_KV_HEADS, tr // tile, rows, tile), lambda i: (0, i, 0, 0))
        out_specs += [k_spec(LANE), k_spec(HEAD_DIM), v_spec(V_AUG, SEL_TILE), v_spec(HEAD_DIM, Q_BLOCK)]
        out_shape += [k_rows(LANE), k_rows(HEAD_DIM), v_tiles(V_AUG, SEL_TILE), v_tiles(HEAD_DIM, Q_BLOCK)]
    return pl.pallas_call(
        _rope_kernel,
        grid=(tok // tr,),
        in_specs=[pl.BlockSpec((NSA_WIDTH, tr), lambda i: (R_Q // NSA_WIDTH, i)), kv_spec(0), kv_spec(1), kv_spec(2), tab, tab],
        out_specs=out_specs,
        out_shape=out_shape,
        compiler_params=_cparams(("parallel",)),
        name="rope",
    )(proj_t, proj_t, proj_t, proj_t, cos_t, sin_t)


def _rope_tables(pos):
    half = HEAD_DIM // 2
    inv = ROPE_THETA ** (-jnp.arange(half, dtype=F32) / half)
    ang = inv[:, None] * pos.astype(F32)[None, :]
    return jnp.cos(ang), jnp.sin(ang)


def _posbias_kernel(p_ref, w_ref, b_ref, o_ref):
    for x in range(2):
        o_ref[x] = _dot(p_ref[x], w_ref[x]) + b_ref[x]


def pos_bias(cmp_pos, cmp_w1, cmp_b1):
    k = CMP_LEN * HEAD_DIM
    pos = jnp.zeros((2, 8, k), F32).at[:, 0].set(cmp_pos.reshape(2, k))
    out = pl.pallas_call(
        _posbias_kernel,
        out_shape=jax.ShapeDtypeStruct((2, 8, CMP_HIDDEN), F32),
        compiler_params=pltpu.CompilerParams(vmem_limit_bytes=VMEM_LIMIT),
        name="posbias",
    )(pos, cmp_w1.reshape(2, k, CMP_HIDDEN), cmp_b1.reshape(2, 1, CMP_HIDDEN))
    return out[:, 0]


CMP_PAGES = 16
CMP_CHUNKS = CMP_PAGES * PAGE // CMP_STRIDE
CHUNKS_PER_PAGE = PAGE // CMP_STRIDE


def _compress_kernel(pt_ref, *refs):
    pages = refs[:CMP_PAGES]
    perm_ref, w1_ref, pb_ref, w2_ref, w2t_ref, b2_ref, b2c_ref, k_ref, kt_ref, vt_ref, carry_ref = refs[CMP_PAGES:]
    s = pl.program_id(1)

    @pl.when(s == 0)
    def _():
        carry_ref[...] = jnp.zeros_like(carry_ref)

    n = CMP_CHUNKS
    perm = perm_ref[...]
    page = lambda pg: pg[...].reshape(KV_ROW, PAGE)
    rows_by_p = [_dot_nt(perm, page(pg).astype(BF16)) for pg in pages]
    row0 = lax.broadcasted_iota(jnp.int32, (n, 1), 0) == 0
    accs = []
    for t in range(KV_ROW // LANE):
        x = t // 2
        sl = slice(t * LANE, (t + 1) * LANE)
        acc = jnp.zeros((n, 4 * CMP_HIDDEN), F32)
        for pp in range(CMP_STRIDE // 2):
            parts = []
            for p in (2 * pp, 2 * pp + 1):
                parts.append(jnp.concatenate(
                    [r[p * CHUNKS_PER_PAGE:(p + 1) * CHUNKS_PER_PAGE, sl] for r in rows_by_p], axis=0))
            lhs = jnp.concatenate(parts, axis=1).astype(BF16)
            acc = acc + _dot(lhs, w1_ref[x, pp])
        accs.append(acc)
    for t, acc in enumerate(accs):
        x = t // 2
        sl = slice(t * LANE, (t + 1) * LANE)
        hid = []
        for hh in range(2):
            part0 = acc[:, hh * 256:hh * 256 + CMP_HIDDEN]
            part1 = acc[:, hh * 256 + CMP_HIDDEN:(hh + 1) * 256]
            csl = slice((t * 2 + hh) * CMP_HIDDEN, (t * 2 + hh + 1) * CMP_HIDDEN)
            prev = jnp.where(row0, carry_ref[0:1, csl], pltpu.roll(part0, 1, 0))
            carry_ref[0:1, csl] = part0[n - 1:n, :]
            hid.append(_silu(prev + part1 + pb_ref[x:x + 1, :]))
        hid = jnp.concatenate(hid, axis=1).astype(BF16)
        out_t = _dot_nt(w2t_ref[x], hid) + b2c_ref[x]
        rows = slice((t % 2) * LANE, (t % 2 + 1) * LANE)
        if x == 0:
            k_ref[0, :, sl] = _dot(hid, w2_ref[...]) + b2_ref[...]
            kt_ref[0, rows, :] = out_t
        else:
            vt_ref[0, rows, :] = out_t


def compress(pool_t, page_table, perm, w1t, pb, w2k, w2t, b2k, b2c):
    b, n_pages = page_table.shape
    steps = n_pages // CMP_PAGES
    n_blk = n_pages * CHUNKS_PER_PAGE
    if pool_t.ndim == 3:
        page_spec = lambda k: pl.BlockSpec((1, KV_ROW, PAGE), lambda bi, si, pt, k=k: (pt[bi, si * CMP_PAGES + k], 0, 0))
    else:
        page_spec = lambda k: pl.BlockSpec((KV_ROW, PAGE), lambda bi, si, pt, k=k: (0, pt[bi, si * CMP_PAGES + k]))
    const = lambda a: pl.BlockSpec(a.shape, lambda bi, si, pt: (0,) * a.ndim)
    consts = (perm, w1t, pb, w2k, w2t, b2k, b2c)
    fm_spec = pl.BlockSpec((1, HALF_ROW, CMP_CHUNKS), lambda bi, si, pt: (bi, 0, si))
    fm_shape = jax.ShapeDtypeStruct((b, HALF_ROW, n_blk), F32)
    grid_spec = pltpu.PrefetchScalarGridSpec(
        num_scalar_prefetch=1,
        grid=(b, steps),
        in_specs=[page_spec(k) for k in range(CMP_PAGES)] + [const(a) for a in consts],
        out_specs=[pl.BlockSpec((1, CMP_CHUNKS, HALF_ROW), lambda bi, si, pt: (bi, si, 0)), fm_spec, fm_spec],
        scratch_shapes=[pltpu.VMEM((8, 8 * CMP_HIDDEN), F32)],
    )
    return pl.pallas_call(
        _compress_kernel,
        grid_spec=grid_spec,
        out_shape=[jax.ShapeDtypeStruct((b, n_blk, HALF_ROW), F32), fm_shape, fm_shape],
        compiler_params=_cparams(("parallel", "arbitrary")),
        name="compress",
    )(page_table, *([pool_t] * CMP_PAGES), *consts)


def _compress_weights(cmp_w1, cmp_w2, cmp_b2):
    w1 = cmp_w1.reshape(2, 2, CMP_STRIDE // 2, 2, HEAD_DIM, CMP_HIDDEN)
    w1 = jnp.transpose(w1, (0, 2, 3, 4, 1, 5))
    eye = jnp.eye(2, dtype=F32)
    w1t = jnp.einsum('xqpdje,hk->xqphdkje', w1, eye).reshape(2, CMP_STRIDE // 2, 256, 512).astype(BF16)
    w2bd = jnp.einsum('xed,hk->xhekd', cmp_w2, eye).reshape(2, 256, LANE).astype(BF16)
    b2t = jnp.concatenate([cmp_b2, cmp_b2], axis=1)
    r = np.arange(PAGE)
    perm = np.zeros((PAGE, PAGE), np.float32)
    perm[(r % CMP_STRIDE) * CHUNKS_PER_PAGE + r // CMP_STRIDE, r] = 1.0
    return (jnp.asarray(perm, dtype=BF16), w1t, w2bd[0], jnp.transpose(w2bd, (0, 2, 1)), b2t[0:1], b2t.reshape(2, LANE, 1))


def _masked_exp0(s, mask):
    s = jnp.where(mask, s, NEG)
    m = jnp.max(s, axis=0, keepdims=True)
    p = jnp.where(mask, jnp.exp2(s - m), 0.0)
    return p, jnp.maximum(jnp.sum(p, axis=0, keepdims=True), TINY)


def _softmax0(s, mask):
    p, denom = _masked_exp0(s, mask)
    return p / denom


def _split_dot(a, x):
    hi = x.astype(BF16)
    lo = (x - hi.astype(F32)).astype(BF16)
    return _dot(a, hi) + _dot(a, lo)


def _split_dot_r(x, a):
    hi = x.astype(BF16)
    lo = (x - hi.astype(F32)).astype(BF16)
    return _dot(hi, a) + _dot(lo, a)


def _top_blocks(imps, cur):
    blk = lax.broadcasted_iota(jnp.int32, (imps[0].shape[0], 1), 0)
    forced = (blk == 0) | ((blk <= cur) & (blk > cur - N_LOCAL))
    imps = tuple(jnp.where(forced, REMOVED, jnp.where(blk > cur, -BIG, imp)) for imp in imps)
    picked = jnp.where(forced, 1.0, 0.0)
    blk_f = blk.astype(F32)

    def pick(_, carry):
        out = []
        for imp, sel in carry:
            mx = jnp.max(imp, axis=0, keepdims=True)
            first = jnp.min(jnp.where(imp == mx, blk_f, 1e9), axis=0, keepdims=True)
            hit = blk_f == first
            out.append((jnp.where(hit, REMOVED, imp), jnp.where(hit, 1.0, sel)))
        return tuple(out)

    final = lax.fori_loop(0, N_SELECT - 1 - N_LOCAL, pick, tuple((imp, picked) for imp in imps), unroll=True)
    return [sel for _, sel in final]


def _flash_update_biased(state, s, v_aug):
    m, acc = state
    m_new = jnp.maximum(m, jnp.max(s, axis=0, keepdims=True))
    p = jnp.exp2(s - m_new)
    acc = jnp.exp2(m - m_new) * acc + _dot(v_aug, p.astype(BF16))
    return m_new, acc


def _cmp_mask(n_rows, pos_q):
    r = lax.broadcasted_iota(jnp.int32, (n_rows, 1), 0)
    return (r >= 1) & (r * CMP_STRIDE + (CMP_LEN - CMP_STRIDE - 1) <= pos_q)


WIN_KEYS = WINDOW + Q_BLOCK
P_HEADS = 4
BLOCKS_PER_TILE = SEL_TILE // SEL_BLOCK


def _nsa_prompt_kernel(qt_ref, kc_ref, vct_ref, ks_ref, vst_ref, kw_ref, vwt_ref, gt_ref, mt_ref, o_ref, sel_ref, s_ref):
    i = pl.program_id(1)
    cols = NSA_GROUP * Q_BLOCK
    lane = lax.broadcasted_iota(jnp.int32, (1, Q_BLOCK), 1)
    pos_q = i * Q_BLOCK + lane
    tile4 = lambda a: jnp.concatenate([a] * NSA_GROUP, axis=1)
    heads = range(P_HEADS)
    q_ts = []
    for h in heads:
        q_blk = qt_ref[h * NSA_GROUP * HEAD_DIM:(h + 1) * NSA_GROUP * HEAD_DIM, :]
        q_ts.append(jnp.concatenate([q_blk[g * HEAD_DIM:(g + 1) * HEAD_DIM, :] for g in range(NSA_GROUP)], axis=1))

    nc = kc_ref.shape[1]
    s_cmp = [_dot(kc_ref[h], q_ts[h]) for h in heads]
    s_win = [_dot(kw_ref[h, pl.ds(pl.multiple_of(i * Q_BLOCK, Q_BLOCK), WIN_KEYS), :], q_ts[h]) for h in heads]

    mask_c = tile4(_cmp_mask(nc, pos_q))
    o_c, imps = [], []
    for h in heads:
        p_c = _softmax0(s_cmp[h], mask_c)
        o_c.append(_dot(vct_ref[h], p_c.astype(BF16)))
        pg = p_c[:, 0:Q_BLOCK]
        for g in range(1, NSA_GROUP):
            pg = pg + p_c[:, g * Q_BLOCK:(g + 1) * Q_BLOCK]
        imps.append(_split_dot(mt_ref[...], pg))
    for h, sel in enumerate(_top_blocks(imps, pos_q // SEL_BLOCK)):
        sel_ref[h] = sel

    zeros_q = jnp.zeros((LANE - HEAD_DIM - 16, cols), BF16)

    def q_aug(h, j):
        grp = sel_ref[h, pl.ds(pl.multiple_of(j * BLOCKS_PER_TILE, BLOCKS_PER_TILE), BLOCKS_PER_TILE), :]
        bias = jnp.concatenate([jnp.where(grp > 0.5, 0.0, NEG), jnp.zeros_like(grp)], axis=0)
        return jnp.concatenate([q_ts[h], tile4(bias).astype(BF16), zeros_q], axis=0)

    def scores(h, j):
        return _dot(ks_ref[h, pl.ds(pl.multiple_of(j * SEL_TILE, SEL_TILE), SEL_TILE), :], q_aug(h, j))

    def sel_body(j, states):
        states = list(states)
        s_cur = s_ref[...]
        for h in heads:
            s_next = scores(h + 1, j) if h + 1 < P_HEADS else scores(0, j + 1)
            states[h] = _flash_update_biased(states[h], s_cur, vst_ref[h, j])
            s_cur = s_next
        s_ref[...] = s_cur
        return tuple(states)

    init = tuple((jnp.full((1, cols), NEG, F32), jnp.zeros((V_AUG, cols), F32)) for _ in heads)
    j_diag = (i * Q_BLOCK) // SEL_TILE
    s_ref[...] = scores(0, 0)
    states = lax.fori_loop(0, j_diag, sel_body, init)
    key_pos = j_diag * SEL_TILE + lax.broadcasted_iota(jnp.int32, (SEL_TILE, 1), 0)
    causal = tile4(jnp.where(key_pos <= pos_q, 0.0, NEG))
    last = [s_ref[...]] + [scores(h, j_diag) for h in heads[1:]]
    o_s = []
    for h in heads:
        _, acc = _flash_update_biased(states[h], last[h] + causal, vst_ref[h, j_diag])
        o_s.append(acc[:HEAD_DIM] / jnp.maximum(acc[HEAD_DIM:HEAD_DIM + 1], TINY))

    w_pos = i * Q_BLOCK - WINDOW + lax.broadcasted_iota(jnp.int32, (WIN_KEYS, 1), 0)
    mask_w = tile4((w_pos <= pos_q) & (w_pos > pos_q - WINDOW) & (w_pos >= 0))
    o_w = []
    for h in heads:
        p_w, denom = _masked_exp0(s_win[h], mask_w)
        p_w = p_w.astype(BF16)
        acc = jnp.zeros((HEAD_DIM, cols), F32)
        for w in range(WIN_KEYS // Q_BLOCK):
            acc = acc + _dot(vwt_ref[h, i + w], p_w[w * Q_BLOCK:(w + 1) * Q_BLOCK, :])
        o_w.append(acc / denom)

    for h in heads:
        gt = jax.nn.sigmoid(gt_ref[h])
        outs = []
        for g in range(NSA_GROUP):
            sl = slice(g * Q_BLOCK, (g + 1) * Q_BLOCK)
            outs.append(gt[3 * g:3 * g + 1, :] * o_c[h][:, sl] + gt[3 * g + 1:3 * g + 2, :] * o_s[h][:, sl]
                        + gt[3 * g + 2:3 * g + 3, :] * o_w[h][:, sl])
        for pair in range(NSA_GROUP // 2):
            both = jnp.concatenate([outs[2 * pair], outs[2 * pair + 1]], axis=0)
            lo = (h * NSA_GROUP // 2 + pair) * LANE
            o_ref[:, lo:lo + LANE] = both.T


def nsa_prompt(qt, kc, vct, ks, vst, kw, vwt, gt, mt, t):
    nq = t // Q_BLOCK
    head = lambda a: pl.BlockSpec((P_HEADS,) + a.shape[1:], lambda h, i: (h,) + (0,) * (a.ndim - 1),
                                  pipeline_mode=pl.Buffered(1))
    width = P_HEADS * NSA_GROUP * HEAD_DIM
    return pl.pallas_call(
        _nsa_prompt_kernel,
        grid=(NSA_KV_HEADS // P_HEADS, nq),
        in_specs=[pl.BlockSpec((width, Q_BLOCK), lambda h, i: (h, i)),
                  head(kc), head(vct), head(ks), head(vst), head(kw), head(vwt),
                  pl.BlockSpec((P_HEADS, 16, Q_BLOCK), lambda h, i: (h, 0, i)),
                  pl.BlockSpec(mt.shape, lambda h, i: (0, 0))],
        out_specs=pl.BlockSpec((Q_BLOCK, width), lambda h, i: (i, h)),
        out_shape=jax.ShapeDtypeStruct((t, NSA_WIDTH), F32),
        scratch_shapes=[pltpu.VMEM((P_HEADS, mt.shape[0], Q_BLOCK), F32),
                        pltpu.VMEM((SEL_TILE, NSA_GROUP * Q_BLOCK), F32)],
        compiler_params=_cparams(("parallel", "arbitrary")),
        name="nsa_p",
    )(qt, kc, vct, ks, vst, kw, vwt, gt, mt)


def _cmp_to_sel_t(n_rows, n_blk, n_blk_pad):
    cs = (np.arange(n_rows)[None, :] - 1) * CMP_STRIDE
    js = np.arange(n_blk_pad)[:, None] * SEL_BLOCK
    m = (cs < js + SEL_BLOCK) & (cs + CMP_LEN > js) & (np.arange(n_rows)[None, :] >= 1) & (np.arange(n_blk_pad)[:, None] < n_blk)
    return jnp.asarray(m.astype(np.float32), dtype=BF16)


S_PAGES = 16
S_COLS = NSA_HEADS * 8


def _softmax_rows(s, mask):
    s = jnp.where(mask, s, NEG)
    m = jnp.max(s, axis=1, keepdims=True)
    p = jnp.where(mask, jnp.exp2(s - m), 0.0)
    return p / jnp.maximum(jnp.sum(p, axis=1, keepdims=True), TINY)


def _flash_rows(state, s, v_t):
    m, l, acc = state
    m_new = jnp.maximum(m, jnp.max(s, axis=1, keepdims=True))
    p = jnp.exp2(s - m_new)
    alpha = jnp.exp2(m - m_new)
    return m_new, alpha * l + jnp.sum(p, axis=1, keepdims=True), alpha * acc + _dot_nt(p.astype(BF16), v_t)


def _nsa_sample_kernel(n_steps, pt_ref, *refs):
    pages = refs[:S_PAGES]
    (qb_ref, kct_ref, vct_ref, cw_ref, nw_ref, ns_ref, g_ref, mt_ref, gsum_ref, emat_ref, o_ref,
     sel_ref, m_ref, l_ref, acc_ref, oc_ref, ow_ref) = refs[S_PAGES:]
    s_id = pl.program_id(1)
    past = n_steps * S_PAGES * PAGE
    qb = qb_ref[0]
    pos_q = past + lax.broadcasted_iota(jnp.int32, (S_COLS, 1), 0) % 8
    blocks_per_step = S_PAGES * PAGE // SEL_BLOCK

    def block_bias(grp, n_keys):
        flags = jnp.concatenate([grp, jnp.zeros((LANE - grp.shape[0], S_COLS), F32)], axis=0).T
        return _dot(jnp.where(flags > 0.5, 0.0, NEG).astype(BF16), emat_ref[:, :n_keys])

    @pl.when(s_id == 0)
    def _():
        n_ent = kct_ref.shape[2]
        wb = cw_ref.shape[2]
        k_win = jnp.concatenate([cw_ref[0, :HALF_ROW, :], nw_ref[0, :HALF_ROW, :]], axis=1).astype(BF16)
        s = _dot(qb, kct_ref[0].astype(BF16))
        s_win = _dot(qb, k_win)
        ent = lax.broadcasted_iota(jnp.int32, (1, n_ent), 1)
        p_c = _softmax_rows(s, (ent >= 1) & (ent * CMP_STRIDE + (CMP_LEN - CMP_STRIDE - 1) <= pos_q))
        oc_ref[...] = _dot_nt(p_c.astype(BF16), vct_ref[0].astype(BF16))
        imp = _split_dot(mt_ref[...], p_c.T)
        imp = _split_dot_r(imp, gsum_ref[...])
        pos_row = past + lax.broadcasted_iota(jnp.int32, (1, S_COLS), 1) % 8
        sel_ref[...] = _top_blocks([imp], pos_row // SEL_BLOCK)[0]

        v_t = jnp.concatenate([cw_ref[0, HALF_ROW:, :], nw_ref[0, HALF_ROW:, :]], axis=1).astype(BF16)
        w_pos = past - wb + lax.broadcasted_iota(jnp.int32, (1, wb + PAGE), 1)
        p_w = _softmax_rows(s_win, (w_pos <= pos_q) & (w_pos > pos_q - WINDOW) & (w_pos >= 0))
        ow_ref[...] = _dot_nt(p_w.astype(BF16), v_t)

        nblk0 = past // SEL_BLOCK
        key_pos = past + lax.broadcasted_iota(jnp.int32, (1, PAGE), 1)
        s = (_dot(qb, ns_ref[0, :HALF_ROW, :].astype(BF16)) + block_bias(sel_ref[nblk0:nblk0 + 8, :], PAGE)
             + jnp.where(key_pos <= pos_q, 0.0, NEG))
        init = (jnp.full((S_COLS, 1), NEG, F32), jnp.zeros((S_COLS, 1), F32), jnp.zeros((S_COLS, HALF_ROW), F32))
        m_ref[...], l_ref[...], acc_ref[...] = _flash_rows(init, s, ns_ref[0, HALF_ROW:, :].astype(BF16))

    grp = sel_ref[pl.ds(pl.multiple_of(s_id * blocks_per_step, blocks_per_step), blocks_per_step), :]
    k_t = jnp.concatenate([pg[0, :HALF_ROW, :] for pg in pages], axis=1).astype(BF16)
    v_t = jnp.concatenate([pg[0, HALF_ROW:, :] for pg in pages], axis=1).astype(BF16)
    s = _dot(qb, k_t) + block_bias(grp, S_PAGES * PAGE)
    st = _flash_rows((m_ref[...], l_ref[...], acc_ref[...]), s, v_t)
    m_ref[...], l_ref[...], acc_ref[...] = st

    @pl.when(s_id == n_steps - 1)
    def _():
        g = jax.nn.sigmoid(g_ref[0])
        o_s = st[2] / jnp.maximum(st[1], TINY)
        o_ref[0] = g[:, 0:1] * oc_ref[...] + g[:, 1:2] * o_s + g[:, 2:3] * ow_ref[...]


def nsa_sample(pool_t, page_table, qb, kct, vct, cache_wt, new_w, new_s, g, mt, gsum, emat):
    b, n_pages = page_table.shape
    steps = n_pages // S_PAGES
    page_spec = lambda k: pl.BlockSpec((1, KV_ROW, PAGE), lambda bi, si, pt, k=k: (pt[bi, si * S_PAGES + k], 0, 0))
    per_b = lambda a: pl.BlockSpec((1,) + a.shape[1:], lambda bi, si, pt: (bi,) + (0,) * (a.ndim - 1))
    const = lambda a: pl.BlockSpec(a.shape, lambda bi, si, pt: (0,) * a.ndim)
    grid_spec = pltpu.PrefetchScalarGridSpec(
        num_scalar_prefetch=1,
        grid=(b, steps),
        in_specs=[page_spec(k) for k in range(S_PAGES)] + [
            per_b(qb), per_b(kct), per_b(vct), per_b(cache_wt), per_b(new_w), per_b(new_s), per_b(g),
            const(mt), const(gsum), const(emat)],
        out_specs=pl.BlockSpec((1, S_COLS, HALF_ROW), lambda bi, si, pt: (bi, 0, 0)),
        scratch_shapes=[pltpu.VMEM((mt.shape[0], S_COLS), F32), pltpu.VMEM((S_COLS, 1), F32), pltpu.VMEM((S_COLS, 1), F32),
                        pltpu.VMEM((S_COLS, HALF_ROW), F32), pltpu.VMEM((S_COLS, HALF_ROW), F32),
                        pltpu.VMEM((S_COLS, HALF_ROW), F32)],
    )
    return pl.pallas_call(
        functools.partial(_nsa_sample_kernel, steps),
        grid_spec=grid_spec,
        out_shape=jax.ShapeDtypeStruct((b, S_COLS, HALF_ROW), F32),
        compiler_params=_cparams(("parallel", "arbitrary")),
        name="nsa_s",
    )(page_table, *([pool_t] * S_PAGES), qb, kct, vct, cache_wt, new_w, new_s, g, mt, gsum, emat)


GLA_SUB = 16


def _gla_head(q, k, v, cum, state):
    c = q.shape[0]
    sub = min(GLA_SUB, c)
    lane = lax.broadcasted_iota(jnp.int32, (1, LANE), 1)
    t_sub = lax.broadcasted_iota(jnp.int32, (sub, 1), 0)
    row_pad = lambda a: jnp.concatenate([a, jnp.zeros((LANE - c, a.shape[1]), F32)], axis=0).astype(BF16)
    v_pad = row_pad(v)
    o = _dot((q * jnp.exp(cum)).astype(BF16), state.astype(BF16))
    blocks = []
    for r0 in range(0, c, sub):
        q_i, cum_i = q[r0:r0 + sub], cum[r0:r0 + sub]
        if r0 == 0:
            att_i = jnp.zeros((sub, LANE), F32)
        else:
            base = cum[r0 - 1:r0]
            q_dec = (q_i * jnp.exp(cum_i - base)).astype(BF16)
            k_dec = row_pad(k * jnp.exp(jnp.minimum(base - cum, 0.0)))
            att_i = jnp.where(lane < r0, _dot_nt(q_dec, k_dec), 0.0)
        for s in range(r0, r0 + sub):
            decay = jnp.exp(jnp.where(t_sub >= s - r0, cum_i - cum[s:s + 1], NEG))
            column = jnp.sum(q_i * k[s:s + 1] * decay, axis=-1, keepdims=True)
            att_i = jnp.where(lane == s, column, att_i)
        blocks.append(att_i)
    att = jnp.concatenate(blocks, axis=0)
    o = o + _dot(att.astype(BF16), v_pad)

    c_last = cum[c - 1:c]
    k_end = jnp.concatenate([k * jnp.exp(c_last - cum), jnp.zeros((LANE - c, GLA_DK), F32)], axis=0)
    eye = lax.broadcasted_iota(jnp.int32, (GLA_DK, GLA_DK), 0) == lax.broadcasted_iota(jnp.int32, (GLA_DK, GLA_DK), 1)
    decay_col = jnp.sum(jnp.where(eye, jnp.exp(c_last), 0.0), axis=1, keepdims=True)
    return o, decay_col * state + _dot(k_end.T.astype(BF16), v_pad)


def _gla_kernel(q_ref, k_ref, v_ref, a_ref, z_ref, wa_ref, ba_ref, nw_ref, s0_ref, o_ref, so_ref, s_ref):
    c = q_ref.shape[0]
    ci = pl.program_id(1)

    @pl.when(ci == 0)
    def _():
        s_ref[...] = s0_ref[0]

    pre = _dot(a_ref[...].astype(BF16), wa_ref[...].astype(BF16)) + ba_ref[...]
    log_a = (jnp.minimum(pre, 0.0) - jnp.log1p(jnp.exp(-jnp.abs(pre)))) / GLA_TAU
    t_idx = lax.broadcasted_iota(jnp.int32, (c, 1), 0)
    cum = log_a
    sh = 1
    while sh < c:
        cum = cum + jnp.where(t_idx >= sh, pltpu.roll(cum, sh, 0), 0.0)
        sh *= 2
    for h in range(GLA_HEADS):
        ks = slice(h * GLA_DK, (h + 1) * GLA_DK)
        vs = slice(h * GLA_DV, (h + 1) * GLA_DV)
        o, new_state = _gla_head(q_ref[:, ks] * (GLA_DK ** -0.5), k_ref[:, ks], v_ref[:, vs], cum[:, ks], s_ref[h])
        s_ref[h] = new_state
        y = o * lax.rsqrt(jnp.mean(o * o, axis=-1, keepdims=True) + EPS) * nw_ref[...]
        o_ref[:, vs] = y * _silu(z_ref[:, vs])

    @pl.when(ci == pl.num_programs(1) - 1)
    def _():
        so_ref[0] = s_ref[...]


def gla(proj, w_a2p, b_a, gla_norm_w, s0, n_seq, chunk):
    rows = proj.shape[0]
    n_chunk = rows // (n_seq * chunk)
    kw, vw = GLA_HEADS * GLA_DK, GLA_HEADS * GLA_DV
    rows_at = lambda width, col: pl.BlockSpec((chunk, width), lambda b, c: (b * n_chunk + c, col // width))
    state_spec = pl.BlockSpec((1, GLA_HEADS, GLA_DK, GLA_DV), lambda b, c: (b, 0, 0, 0))
    return pl.pallas_call(
        _gla_kernel,
        grid=(n_seq, n_chunk),
        in_specs=[rows_at(kw, C_QG), rows_at(kw, C_KG), rows_at(vw, C_VG), rows_at(LANE, C_AG), rows_at(vw, C_ZG),
                  pl.BlockSpec((LANE, kw), lambda b, c: (0, 0)),
                  pl.BlockSpec((1, kw), lambda b, c: (0, 0)),
                  pl.BlockSpec((1, GLA_DV), lambda b, c: (0, 0)),
                  state_spec],
        out_specs=[pl.BlockSpec((chunk, vw), lambda b, c: (b * n_chunk + c, 0)), state_spec],
        out_shape=[jax.ShapeDtypeStruct((rows, vw), F32),
                   jax.ShapeDtypeStruct((n_seq, GLA_HEADS, GLA_DK, GLA_DV), F32)],
        scratch_shapes=[pltpu.VMEM((GLA_HEADS, GLA_DK, GLA_DV), F32)],
        compiler_params=_cparams(("parallel", "arbitrary")),
        name="gla",
    )(proj, proj, proj, proj, proj, w_a2p, b_a.reshape(1, -1), gla_norm_w.reshape(1, -1), s0)


def _out_kernel(on_ref, zn_ref, og_ref, mn_ref, mg_ref, x_ref, gate_ref, wn_ref, wg_ref, wo_ref, fw_ref, y_ref):
    o_nsa = (on_ref[...] * _silu(zn_ref[...])).astype(BF16)
    merged = (jax.nn.sigmoid(mn_ref[...]) * _dot(o_nsa, wn_ref[...])
              + jax.nn.sigmoid(mg_ref[...]) * _dot(og_ref[...].astype(BF16), wg_ref[...]))
    y = x_ref[...] + gate_ref[...] * _dot(merged.astype(BF16), wo_ref[...])
    y_ref[...] = y * lax.rsqrt(jnp.mean(y * y, axis=-1, keepdims=True) + EPS) * fw_ref[...]


def out_proj(o_nsa, o_gla, proj, x, gate, w_o_nsa, w_o_gla, w_out, final_norm_w, tm):
    rows = x.shape[0]
    per_row = gate.shape[0] != 1
    gate_spec = pl.BlockSpec((tm, D_MODEL), lambda i: (i, 0)) if per_row else pl.BlockSpec((1, D_MODEL), lambda i: (0, 0))
    resident = lambda a: pl.BlockSpec(a.shape, lambda i: (0, 0), pipeline_mode=pl.Buffered(1))
    return pl.pallas_call(
        _out_kernel,
        grid=(rows // tm,),
        in_specs=[pl.BlockSpec((tm, NSA_WIDTH), lambda i: (i, 0)),
                  pl.BlockSpec((tm, NSA_WIDTH), lambda i: (i, C_ZN // NSA_WIDTH)),
                  pl.BlockSpec((tm, NSA_WIDTH), lambda i: (i, 0)),
                  pl.BlockSpec((tm, D_MODEL), lambda i: (i, C_MN // D_MODEL)),
                  pl.BlockSpec((tm, D_MODEL), lambda i: (i, C_MG // D_MODEL)),
                  pl.BlockSpec((tm, D_MODEL), lambda i: (i, 0)),
                  gate_spec, resident(w_o_nsa), resident(w_o_gla), resident(w_out),
                  pl.BlockSpec((1, D_MODEL), lambda i: (0, 0))],
        out_specs=pl.BlockSpec((tm, D_MODEL), lambda i: (i, 0)),
        out_shape=jax.ShapeDtypeStruct((rows, D_MODEL), F32),
        compiler_params=_cparams(("parallel",)),
        name="outproj",
    )(o_nsa, proj, o_gla, proj, proj, x, gate, w_o_nsa, w_o_gla, w_out, final_norm_w.reshape(1, D_MODEL))


def _feature_major(a):
    lead = a.shape[:-4]
    n = len(lead)
    a = jnp.transpose(a, tuple(range(n)) + (n + 1, n + 2, n + 3, n))
    return a.reshape(lead + (KV_ROW, a.shape[-1]))


def _token_major(a_t, lead):
    rows = a_t.shape[-1]
    a = a_t.reshape(a_t.shape[:-2] + (2, NSA_KV_HEADS, HEAD_DIM, rows))
    n = a.ndim - 4
    a = jnp.transpose(a, tuple(range(n)) + (n + 3, n, n + 1, n + 2))
    return a.reshape(lead + (rows, 2, NSA_KV_HEADS, HEAD_DIM))


def kernel(x_prompt, x_sample, cache_kv_cmp, cache_kv_sel, cache_kv_win, state_gla, page_table, c_prompt, c_sample, norm_w, w_ada, b_ada, w_in, cmp_pos, cmp_w1, cmp_b1, cmp_w2, cmp_b2, w_a2, b_a, gla_norm_w, w_o_nsa, w_o_gla, w_out, final_norm_w):
    assert x_prompt.shape[0] == 1 and norm_w.shape[0] == 1, "one prompt sequence, one layer"
    t_p = x_prompt.shape[1]
    b_s, t_s = x_sample.shape[:2]
    past = page_table.shape[1] * PAGE
    wb = cache_kv_win.shape[2]
    assert t_s == 8 and wb == WINDOW and past % (S_PAGES * PAGE) == 0 and t_p % SEL_TILE == 0
    rows_s = b_s * t_s

    c_rows = jnp.zeros((40, D_MODEL), F32).at[0:1].set(c_prompt).at[1:1 + b_s].set(c_sample)
    mod = ada_mod(c_rows, w_ada[0], b_ada[0])
    shift, scale, gate = mod[:, :D_MODEL], mod[:, D_MODEL:2 * D_MODEL], mod[:, 2 * D_MODEL:]
    per_row = lambda a: jnp.repeat(a[1:1 + b_s], t_s, axis=0)

    w_t = jnp.transpose(w_in[0])
    xp = x_prompt.reshape(t_p, D_MODEL)
    xs = x_sample.reshape(rows_s, D_MODEL)
    tp, ts = _row_tiles(t_p), _row_tiles(rows_s)
    h_p = modulated_norm(xp, scale[0:1], shift[0:1], norm_w[0], tp['norm'])
    h_s = modulated_norm(xs, per_row(scale), per_row(shift), norm_w[0], ts['norm'])
    proj_p = in_proj(h_p, w_t, RM_OFFSETS, tp['proj_rm'], RM_TILE, False)
    projt_p = in_proj(h_p, w_t, FM_OFFSETS, tp['proj_fm'], FM_TILE, True)
    proj_s = in_proj(h_s, w_t, RM_OFFSETS, ts['proj_rm'], RM_TILE, False)
    projt_s = in_proj(h_s, w_t, FM_OFFSETS, ts['proj_fm'], FM_TILE, True)

    cos_p, sin_p = _rope_tables(jnp.arange(t_p, dtype=jnp.int32))
    cos_s, sin_s = _rope_tables(jnp.tile(past + jnp.arange(t_s, dtype=jnp.int32), b_s))
    qt_p, kvc_p, kvs_p, kvw_p, ks_p, kw_p, vst_p, vwt_p = rope_stage(projt_p, cos_p, sin_p, tp['rope'], True)
    qt_s, kvc_s, kvs_s, kvw_s = rope_stage(projt_s, cos_s, sin_s, ts['rope'], False)

    pb = pos_bias(cmp_pos[0], cmp_w1[0], cmp_b1[0])
    cmp_consts = _compress_weights(cmp_w1[0], cmp_w2[0], cmp_b2[0])
    perm, w1t, w2k, w2t, b2k, b2c = cmp_consts
    ident = jnp.arange(t_p // PAGE, dtype=jnp.int32)[None, :]
    kc_p, _, vct_p = compress(kvc_p, ident, perm, w1t, pb, w2k, w2t, b2k, b2c)
    pool_c = _feature_major(cache_kv_cmp[0])
    _, kct_s, vct_s = compress(pool_c, page_table, perm, w1t, pb, w2k, w2t, b2k, b2c)

    n_ent = kc_p.shape[1]
    kc_h = jnp.transpose(kc_p[0].reshape(n_ent, NSA_KV_HEADS, HEAD_DIM), (1, 0, 2)).astype(BF16)
    vct_h = vct_p[0].reshape(NSA_KV_HEADS, HEAD_DIM, n_ent).astype(BF16)
    kw_h = jnp.pad(kw_p, ((0, 0), (WINDOW, 0), (0, 0)))
    vwt_h = jnp.pad(vwt_p, ((0, 0), (WINDOW // Q_BLOCK, 0), (0, 0), (0, 0)))
    g_p = projt_p[R_GN:R_GN + 48].reshape(NSA_KV_HEADS, 12, t_p)
    g_p = jnp.pad(g_p, ((0, 0), (0, 4), (0, 0)))
    mt_p = _cmp_to_sel_t(n_ent, t_p // SEL_BLOCK, t_p // SEL_BLOCK)
    o_nsa_p = nsa_prompt(qt_p, kc_h, vct_h, ks_p, vst_p, kw_h, vwt_h, g_p, mt_p, t_p)

    q5 = qt_s.reshape(NSA_KV_HEADS, NSA_GROUP, HEAD_DIM, b_s, t_s)
    q_c = jnp.transpose(q5, (3, 0, 1, 4, 2)).reshape(b_s, NSA_KV_HEADS, NSA_GROUP * t_s, HEAD_DIM)
    eye = jnp.eye(NSA_KV_HEADS, dtype=BF16)
    qb = jnp.einsum('bhcd,hk->bhckd', q_c, eye).reshape(b_s, S_COLS, HALF_ROW)
    new_keys = lambda a_t: jnp.pad(jnp.transpose(a_t.reshape(KV_ROW, b_s, t_s), (1, 0, 2)), ((0, 0), (0, 0), (0, PAGE - t_s)))
    g_s = projt_s[R_GN:R_GN + 48].reshape(NSA_KV_HEADS, NSA_GROUP, 3, b_s, t_s)
    g_s = jnp.transpose(g_s, (3, 0, 1, 4, 2)).reshape(b_s, S_COLS, 3)
    g_s = jnp.pad(g_s, ((0, 0), (0, 0), (0, 5)))
    n_blk_s = -(-(past + t_s) // SEL_BLOCK)
    mt_s = _cmp_to_sel_t(kct_s.shape[2], n_blk_s, -(-n_blk_s // 8) * 8)
    col = np.arange(S_COLS)
    gsum = jnp.asarray(((col[:, None] // 32 == col[None, :] // 32) & (col[:, None] % 8 == col[None, :] % 8)).astype(np.float32), dtype=BF16)
    emat = jnp.asarray((np.arange(S_PAGES * PAGE)[None, :] // SEL_BLOCK == np.arange(LANE)[:, None]).astype(np.float32), dtype=BF16)
    cache_wt = _feature_major(cache_kv_win[0])
    o_all = nsa_sample(_feature_major(cache_kv_sel[0]), page_table, qb, kct_s, vct_s, cache_wt,
                       new_keys(kvw_s), new_keys(kvs_s), g_s, mt_s, gsum, emat)
    o6 = o_all.reshape(b_s, NSA_KV_HEADS, NSA_GROUP, t_s, NSA_KV_HEADS, HEAD_DIM)
    o_nsa_s = jnp.stack([o6[:, h, :, :, h, :] for h in range(NSA_KV_HEADS)], axis=1)
    o_nsa_s = jnp.transpose(o_nsa_s, (0, 3, 1, 2, 4)).reshape(rows_s, NSA_WIDTH)

    w_a2p = jnp.zeros((LANE, GLA_HEADS * GLA_DK), F32).at[:GLA_RANK].set(w_a2[0])
    s0_p = jnp.zeros((1, GLA_HEADS, GLA_DK, GLA_DV), F32)
    o_gla_p, st_p = gla(proj_p, w_a2p, b_a[0], gla_norm_w[0], s0_p, 1, GLA_CHUNK)
    o_gla_s, st_s = gla(proj_s, w_a2p, b_a[0], gla_norm_w[0], state_gla[0], b_s, t_s)

    wn, wg, wo = w_o_nsa[0].astype(BF16), w_o_gla[0].astype(BF16), w_out[0].astype(BF16)
    y_p = out_proj(o_nsa_p, o_gla_p, proj_p, xp, gate[0:1], wn, wg, wo, final_norm_w, tp['out'])
    y_s = out_proj(o_nsa_s, o_gla_s, proj_s, xs, per_row(gate), wn, wg, wo, final_norm_w, ts['out'])

    sample_rows = lambda a_t: _token_major(jnp.transpose(a_t.reshape(KV_ROW, b_s, t_s), (1, 0, 2)), (1, b_s))
    win_t = jnp.concatenate([cache_wt, jnp.transpose(kvw_s.reshape(KV_ROW, b_s, t_s), (1, 0, 2))], axis=2)[:, :, t_s:]
    n_win = min(WINDOW, t_p)
    return (y_p.reshape(x_prompt.shape), y_s.reshape(x_sample.shape),
            _token_major(kvc_p, (1, 1)), sample_rows(kvc_s), _token_major(kvs_p, (1, 1)), sample_rows(kvs_s),
            _token_major(kvw_p[:, t_p - n_win:], (1, 1)), _token_major(win_t, (1, b_s)),
            st_p[None], st_s[None])
```

```python
import functools

import jax
import jax.numpy as jnp
import numpy as np
from jax import lax
from jax.experimental import pallas as pl
from jax.experimental.pallas import tpu as pltpu

F32 = jnp.float32
BF16 = jnp.bfloat16

D_MODEL = 2048
HEAD_DIM = 64
NSA_HEADS = 16
NSA_KV_HEADS = 4
NSA_GROUP = 4
NSA_WIDTH = 1024
HALF_ROW = NSA_KV_HEADS * HEAD_DIM
KV_ROW = 2 * HALF_ROW
CMP_LEN = 32
CMP_STRIDE = 16
CMP_HIDDEN = 128
SEL_BLOCK = 64
N_SELECT = 16
N_LOCAL = 2
WINDOW = 512
Q_BLOCK = 128
PAGE = 128
GLA_HEADS = 4
GLA_DK = 128
GLA_DV = 256
GLA_RANK = 16
GLA_TAU = 16.0
GLA_CHUNK = 64
ROPE_THETA = 10000.0
EPS = 1e-6
NEG = -1e30
BIG = 1e30
TINY = 1e-30
REMOVED = -3e38
LOG2E = 1.4426950408889634
SEL_TILE = 512
V_AUG = HEAD_DIM + 16

LANE = 128
VMEM_LIMIT = 48 * 1024 * 1024

(W_Q, W_KV, W_GN, W_ZN, W_QG, W_KG, W_VG, W_AG, W_ZG, W_MN, W_MG) = (
    0, 1024, 2560, 2608, 3632, 4144, 4656, 5680, 5696, 6720, 8768)
RM_TILE = 1024
W_ALIGN = 16
C_MN, C_MG, C_ZN, C_VG, C_ZG, C_QG, C_KG, C_AG = 0, 2048, 4096, 5120, 6144, 7168, 7680, 8192
assert W_KG == W_QG + RM_TILE // 2
RM_SOURCES = ((W_MN, 2), (W_MG, 2), (W_ZN, 1), (W_VG, 1), (W_ZG, 1), (W_QG, 1), (W_AG, 1))
RM_OFFSETS = tuple(start + RM_TILE * k for start, tiles in RM_SOURCES for k in range(tiles))
RM_COLS = RM_TILE * len(RM_OFFSETS)
R_Q, R_KV, R_GN = W_Q, W_KV, W_GN
FM_TILE = 1344
FM_ROWS = 2 * FM_TILE
FM_OFFSETS = (0, FM_TILE)


def _row_tiles(rows):
    big = rows >= 1024
    return dict(norm=512 if big else rows, proj_rm=1024 if big else rows, proj_fm=512 if big else rows,
                rope=SEL_TILE if big else rows, out=256 if big else rows)


def _cparams(sem):
    return pltpu.CompilerParams(dimension_semantics=sem, vmem_limit_bytes=VMEM_LIMIT)


def _dot(a, b):
    return jnp.dot(a, b, preferred_element_type=F32)


def _dot_nt(a, b):
    return lax.dot_general(a, b, (((1,), (1,)), ((), ())), preferred_element_type=F32)


def _silu(x):
    return x * jax.nn.sigmoid(x)


def _ada_kernel(c_ref, w_ref, b_ref, o_ref):
    o_ref[...] = _dot(c_ref[...].astype(BF16), w_ref[...].astype(BF16)) + b_ref[...]


def ada_mod(c_rows, w_ada, b_ada):
    rows, tn = c_rows.shape[0], 512
    n = w_ada.shape[1]
    return pl.pallas_call(
        _ada_kernel,
        grid=(n // tn,),
        in_specs=[pl.BlockSpec((rows, D_MODEL), lambda j: (0, 0)),
                  pl.BlockSpec((D_MODEL, tn), lambda j: (0, j)),
                  pl.BlockSpec((1, tn), lambda j: (0, j))],
        out_specs=pl.BlockSpec((rows, tn), lambda j: (0, j)),
        out_shape=jax.ShapeDtypeStruct((rows, n), F32),
        compiler_params=_cparams(("parallel",)),
        name="ada",
    )(c_rows, w_ada, b_ada.reshape(1, n))


def _norm_kernel(x_ref, sc_ref, sh_ref, nw_ref, h_ref):
    x = x_ref[...]
    y = x * lax.rsqrt(jnp.mean(x * x, axis=-1, keepdims=True) + EPS) * nw_ref[...]
    h_ref[...] = (y * (1.0 + sc_ref[...]) + sh_ref[...]).astype(BF16)


def modulated_norm(x, scale, shift, norm_w, tm):
    rows = x.shape[0]
    per_row = scale.shape[0] != 1
    mod_spec = pl.BlockSpec((tm, D_MODEL), lambda i: (i, 0)) if per_row else pl.BlockSpec((1, D_MODEL), lambda i: (0, 0))
    return pl.pallas_call(
        _norm_kernel,
        grid=(rows // tm,),
        in_specs=[pl.BlockSpec((tm, D_MODEL), lambda i: (i, 0)), mod_spec, mod_spec,
                  pl.BlockSpec((1, D_MODEL), lambda i: (0, 0))],
        out_specs=pl.BlockSpec((tm, D_MODEL), lambda i: (i, 0)),
        out_shape=jax.ShapeDtypeStruct((rows, D_MODEL), BF16),
        compiler_params=_cparams(("parallel",)),
        name="norm",
    )(x, scale, shift, norm_w.reshape(1, D_MODEL))


def _inproj_rm_kernel(off_ref, h_ref, w_ref, o_ref, wb_ref):
    @pl.when(pl.program_id(1) == 0)
    def _():
        wb_ref[...] = w_ref[...].astype(BF16)

    o_ref[...] = _dot_nt(h_ref[...], wb_ref[...])


def _inproj_fm_kernel(off_ref, h_ref, w_ref, o_ref, wb_ref):
    @pl.when(pl.program_id(1) == 0)
    def _():
        wb_ref[...] = w_ref[...].astype(BF16)

    o_ref[...] = _dot_nt(wb_ref[...], h_ref[...])


def in_proj(h, w_t, row_offsets, tm, tn, feature_major):
    rows, n = h.shape[0], len(row_offsets) * tn
    if feature_major:
        body, out_spec, out_shape = _inproj_fm_kernel, pl.BlockSpec((tn, tm), lambda j, i, off: (j, i)), (n, rows)
    else:
        body, out_spec, out_shape = _inproj_rm_kernel, pl.BlockSpec((tm, tn), lambda j, i, off: (i, j)), (rows, n)
    grid_spec = pltpu.PrefetchScalarGridSpec(
        num_scalar_prefetch=1,
        grid=(len(row_offsets), rows // tm),
        in_specs=[pl.BlockSpec((tm, D_MODEL), lambda j, i, off: (i, 0)),
                  pl.BlockSpec((pl.Element(tn), pl.Element(D_MODEL)), lambda j, i, off: (off[j] * W_ALIGN, 0))],
        out_specs=out_spec,
        scratch_shapes=[pltpu.VMEM((tn, D_MODEL), BF16)],
    )
    return pl.pallas_call(
        body,
        grid_spec=grid_spec,
        out_shape=jax.ShapeDtypeStruct(out_shape, F32),
        compiler_params=_cparams(("parallel", "arbitrary")),
        name="inproj_fm" if feature_major else "inproj_rm",
    )(jnp.asarray([o // W_ALIGN for o in row_offsets], jnp.int32), h, w_t)


def _rope_kernel(q_ref, c_ref, s_ref, w_ref, cos_ref, sin_ref, qo_ref, co_ref, so_ref, wo_ref, *tile_refs):
    cos, sin = cos_ref[...], sin_ref[...]
    hh = HEAD_DIM // 2
    tr = cos.shape[1]

    def rot(src, head):
        x1 = src[head * HEAD_DIM:head * HEAD_DIM + hh, :]
        x2 = src[head * HEAD_DIM + hh:(head + 1) * HEAD_DIM, :]
        return x1 * cos - x2 * sin, x2 * cos + x1 * sin

    q_scale = HEAD_DIM ** -0.5 * LOG2E
    for head in range(NSA_HEADS):
        o1, o2 = rot(q_ref, head)
        qo_ref[head * HEAD_DIM:head * HEAD_DIM + hh, :] = (o1 * q_scale).astype(BF16)
        qo_ref[head * HEAD_DIM + hh:(head + 1) * HEAD_DIM, :] = (o2 * q_scale).astype(BF16)
    for src, dst in ((c_ref, co_ref), (s_ref, so_ref), (w_ref, wo_ref)):
        for head in range(NSA_KV_HEADS):
            o1, o2 = rot(src, head)
            dst[head * HEAD_DIM:head * HEAD_DIM + hh, :] = o1
            dst[head * HEAD_DIM + hh:(head + 1) * HEAD_DIM, :] = o2
        dst[HALF_ROW:, :] = src[HALF_ROW:, :]
    if tile_refs:
        ks_ref, kw_ref, vs_ref, vw_ref = tile_refs
        lane = lax.broadcasted_iota(jnp.int32, (1, LANE), 1)
        r = lax.broadcasted_iota(jnp.int32, (tr, 1), 0)
        onehot = jnp.where(lane - HEAD_DIM == (r // SEL_BLOCK) % (SEL_TILE // SEL_BLOCK), 1.0, 0.0)
        ones_row = jnp.where(lax.broadcasted_iota(jnp.int32, (V_AUG - HEAD_DIM, SEL_TILE), 0) == 0, 1.0, 0.0)
        for pair in range(NSA_KV_HEADS // 2):
            k_pair = so_ref[pair * LANE:(pair + 1) * LANE, :].T
            ks_ref[2 * pair] = jnp.where(lane < HEAD_DIM, k_pair, onehot).astype(BF16)
            ks_ref[2 * pair + 1] = jnp.where(lane < HEAD_DIM, pltpu.roll(k_pair, HEAD_DIM, 1), onehot).astype(BF16)
            kw_pair = wo_ref[pair * LANE:(pair + 1) * LANE, :].T.astype(BF16)
            kw_ref[2 * pair] = kw_pair[:, :HEAD_DIM]
            kw_ref[2 * pair + 1] = kw_pair[:, HEAD_DIM:]
        for head in range(NSA_KV_HEADS):
            rows = slice(HALF_ROW + head * HEAD_DIM, HALF_ROW + (head + 1) * HEAD_DIM)
            v = so_ref[rows, :]
            for w in range(tr // SEL_TILE):
                vs_ref[head, w] = jnp.concatenate([v[:, w * SEL_TILE:(w + 1) * SEL_TILE], ones_row], axis=0).astype(BF16)
            v = wo_ref[rows, :].astype(BF16)
            for w in range(tr // Q_BLOCK):
                vw_ref[head, w] = v[:, w * Q_BLOCK:(w + 1) * Q_BLOCK]


def rope_stage(proj_t, cos_t, sin_t, tr, with_tiles):
    tok = proj_t.shape[1]
    kv_spec = lambda k: pl.BlockSpec((KV_ROW, tr), lambda i, k=k: (R_KV // KV_ROW + k, i))
    out_kv = jax.ShapeDtypeStruct((KV_ROW, tok), F32)
    tab = pl.BlockSpec((HEAD_DIM // 2, tr), lambda i: (0, i))
    out_specs = [pl.BlockSpec((NSA_WIDTH, tr), lambda i: (0, i))] + [pl.BlockSpec((KV_ROW, tr), lambda i: (0, i))] * 3
    out_shape = [jax.ShapeDtypeStruct((NSA_WIDTH, tok), BF16), out_kv, out_kv, out_kv]
    if with_tiles:
        k_rows = lambda width: jax.ShapeDtypeStruct((NSA_KV_HEADS, tok, width), BF16)
        k_spec = lambda width: pl.BlockSpec((NSA_KV_HEADS, tr, width), lambda i: (0, i, 0))
        v_tiles = lambda rows, tile: jax.ShapeDtypeStruct((NSA_KV_HEADS, tok // tile, rows, tile), BF16)
        v_spec = lambda rows, tile: pl.BlockSpec((NSA_KV_HEADS, tr // tile, rows, tile), lambda i: (0, i, 0, 0))
        out_specs += [k_spec(LANE), k_spec(HEAD_DIM), v_spec(V_AUG, SEL_TILE), v_spec(HEAD_DIM, Q_BLOCK)]
        out_shape += [k_rows(LANE), k_rows(HEAD_DIM), v_tiles(V_AUG, SEL_TILE), v_tiles(HEAD_DIM, Q_BLOCK)]
    return pl.pallas_call(
        _rope_kernel,
        grid=(tok // tr,),
        in_specs=[pl.BlockSpec((NSA_WIDTH, tr), lambda i: (R_Q // NSA_WIDTH, i)), kv_spec(0), kv_spec(1), kv_spec(2), tab, tab],
        out_specs=out_specs,
        out_shape=out_shape,
        compiler_params=_cparams(("parallel",)),
        name="rope",
    )(proj_t, proj_t, proj_t, proj_t, cos_t, sin_t)


def _rope_tables(pos):
    half = HEAD_DIM // 2
    inv = ROPE_THETA ** (-jnp.arange(half, dtype=F32) / half)
    ang = inv[:, None] * pos.astype(F32)[None, :]
    return jnp.cos(ang), jnp.sin(ang)


def _posbias_kernel(p_ref, w_ref, b_ref, o_ref):
    for x in range(2):
        o_ref[x] = _dot(p_ref[x], w_ref[x]) + b_ref[x]


def pos_bias(cmp_pos, cmp_w1, cmp_b1):
    k = CMP_LEN * HEAD_DIM
    pos = jnp.zeros((2, 8, k), F32).at[:, 0].set(cmp_pos.reshape(2, k))
    out = pl.pallas_call(
        _posbias_kernel,
        out_shape=jax.ShapeDtypeStruct((2, 8, CMP_HIDDEN), F32),
        compiler_params=pltpu.CompilerParams(vmem_limit_bytes=VMEM_LIMIT),
        name="posbias",
    )(pos, cmp_w1.reshape(2, k, CMP_HIDDEN), cmp_b1.reshape(2, 1, CMP_HIDDEN))
    return out[:, 0]


CMP_PAGES = 16
CMP_CHUNKS = CMP_PAGES * PAGE // CMP_STRIDE
CHUNKS_PER_PAGE = PAGE // CMP_STRIDE


def _compress_kernel(pt_ref, *refs):
    pages = refs[:CMP_PAGES]
    perm_ref, w1_ref, pb_ref, w2_ref, w2t_ref, b2_ref, b2c_ref, k_ref, kt_ref, vt_ref, carry_ref = refs[CMP_PAGES:]
    s = pl.program_id(1)

    @pl.when(s == 0)
    def _():
        carry_ref[...] = jnp.zeros_like(carry_ref)

    n = CMP_CHUNKS
    perm = perm_ref[...]
    page = lambda pg: pg[...].reshape(KV_ROW, PAGE)
    rows_by_p = [_dot_nt(perm, page(pg).astype(BF16)) for pg in pages]
    row0 = lax.broadcasted_iota(jnp.int32, (n, 1), 0) == 0
    accs = []
    for t in range(KV_ROW // LANE):
        x = t // 2
        sl = slice(t * LANE, (t + 1) * LANE)
        acc = jnp.zeros((n, 4 * CMP_HIDDEN), F32)
        for pp in range(CMP_STRIDE // 2):
            parts = []
            for p in (2 * pp, 2 * pp + 1):
                parts.append(jnp.concatenate(
                    [r[p * CHUNKS_PER_PAGE:(p + 1) * CHUNKS_PER_PAGE, sl] for r in rows_by_p], axis=0))
            lhs = jnp.concatenate(parts, axis=1).astype(BF16)
            acc = acc + _dot(lhs, w1_ref[x, pp])
        accs.append(acc)
    for t, acc in enumerate(accs):
        x = t // 2
        sl = slice(t * LANE, (t + 1) * LANE)
        hid = []
        for hh in range(2):
            part0 = acc[:, hh * 256:hh * 256 + CMP_HIDDEN]
            part1 = acc[:, hh * 256 + CMP_HIDDEN:(hh + 1) * 256]
            csl = slice((t * 2 + hh) * CMP_HIDDEN, (t * 2 + hh + 1) * CMP_HIDDEN)
            prev = jnp.where(row0, carry_ref[0:1, csl], pltpu.roll(part0, 1, 0))
            carry_ref[0:1, csl] = part0[n - 1:n, :]
            hid.append(_silu(prev + part1 + pb_ref[x:x + 1, :]))
        hid = jnp.concatenate(hid, axis=1).astype(BF16)
        out_t = _dot_nt(w2t_ref[x], hid) + b2c_ref[x]
        rows = slice((t % 2) * LANE, (t % 2 + 1) * LANE)
        if x == 0:
            k_ref[0, :, sl] = _dot(hid, w2_ref[...]) + b2_ref[...]
            kt_ref[0, rows, :] = out_t
        else:
            vt_ref[0, rows, :] = out_t


def compress(pool_t, page_table, perm, w1t, pb, w2k, w2t, b2k, b2c):
    b, n_pages = page_table.shape
    steps = n_pages // CMP_PAGES
    n_blk = n_pages * CHUNKS_PER_PAGE
    if pool_t.ndim == 3:
        page_spec = lambda k: pl.BlockSpec((1, KV_ROW, PAGE), lambda bi, si, pt, k=k: (pt[bi, si * CMP_PAGES + k], 0, 0))
    else:
        page_spec = lambda k: pl.BlockSpec((KV_ROW, PAGE), lambda bi, si, pt, k=k: (0, pt[bi, si * CMP_PAGES + k]))
    const = lambda a: pl.BlockSpec(a.shape, lambda bi, si, pt: (0,) * a.ndim)
    consts = (perm, w1t, pb, w2k, w2t, b2k, b2c)
    fm_spec = pl.BlockSpec((1, HALF_ROW, CMP_CHUNKS), lambda bi, si, pt: (bi, 0, si))
    fm_shape = jax.ShapeDtypeStruct((b, HALF_ROW, n_blk), F32)
    grid_spec = pltpu.PrefetchScalarGridSpec(
        num_scalar_prefetch=1,
        grid=(b, steps),
        in_specs=[page_spec(k) for k in range(CMP_PAGES)] + [const(a) for a in consts],
        out_specs=[pl.BlockSpec((1, CMP_CHUNKS, HALF_ROW), lambda bi, si, pt: (bi, si, 0)), fm_spec, fm_spec],
        scratch_shapes=[pltpu.VMEM((8, 8 * CMP_HIDDEN), F32)],
    )
    return pl.pallas_call(
        _compress_kernel,
        grid_spec=grid_spec,
        out_shape=[jax.ShapeDtypeStruct((b, n_blk, HALF_ROW), F32), fm_shape, fm_shape],
        compiler_params=_cparams(("parallel", "arbitrary")),
        name="compress",
    )(page_table, *([pool_t] * CMP_PAGES), *consts)


def _compress_weights(cmp_w1, cmp_w2, cmp_b2):
    w1 = cmp_w1.reshape(2, 2, CMP_STRIDE // 2, 2, HEAD_DIM, CMP_HIDDEN)
    w1 = jnp.transpose(w1, (0, 2, 3, 4, 1, 5))
    eye = jnp.eye(2, dtype=F32)
    w1t = jnp.einsum('xqpdje,hk->xqphdkje', w1, eye).reshape(2, CMP_STRIDE // 2, 256, 512).astype(BF16)
    w2bd = jnp.einsum('xed,hk->xhekd', cmp_w2, eye).reshape(2, 256, LANE).astype(BF16)
    b2t = jnp.concatenate([cmp_b2, cmp_b2], axis=1)
    r = np.arange(PAGE)
    perm = np.zeros((PAGE, PAGE), np.float32)
    perm[(r % CMP_STRIDE) * CHUNKS_PER_PAGE + r // CMP_STRIDE, r] = 1.0
    return (jnp.asarray(perm, dtype=BF16), w1t, w2bd[0], jnp.transpose(w2bd, (0, 2, 1)), b2t[0:1], b2t.reshape(2, LANE, 1))


def _masked_exp0(s, mask):
    s = jnp.where(mask, s, NEG)
    m = jnp.max(s, axis=0, keepdims=True)
    p = jnp.where(mask, jnp.exp2(s - m), 0.0)
    return p, jnp.maximum(jnp.sum(p, axis=0, keepdims=True), TINY)


def _softmax0(s, mask):
    p, denom = _masked_exp0(s, mask)
    return p / denom


def _split_dot(a, x):
    hi = x.astype(BF16)
    lo = (x - hi.astype(F32)).astype(BF16)
    return _dot(a, hi) + _dot(a, lo)


def _split_dot_r(x, a):
    hi = x.astype(BF16)
    lo = (x - hi.astype(F32)).astype(BF16)
    return _dot(hi, a) + _dot(lo, a)


def _top_blocks(imps, cur):
    blk = lax.broadcasted_iota(jnp.int32, (imps[0].shape[0], 1), 0)
    forced = (blk == 0) | ((blk <= cur) & (blk > cur - N_LOCAL))
    imps = tuple(jnp.where(forced, REMOVED, jnp.where(blk > cur, -BIG, imp)) for imp in imps)
    picked = jnp.where(forced, 1.0, 0.0)
    blk_f = blk.astype(F32)

    def pick(_, carry):
        out = []
        for imp, sel in carry:
            mx = jnp.max(imp, axis=0, keepdims=True)
            first = jnp.min(jnp.where(imp == mx, blk_f, 1e9), axis=0, keepdims=True)
            hit = blk_f == first
            out.append((jnp.where(hit, REMOVED, imp), jnp.where(hit, 1.0, sel)))
        return tuple(out)

    final = lax.fori_loop(0, N_SELECT - 1 - N_LOCAL, pick, tuple((imp, picked) for imp in imps), unroll=True)
    return [sel for _, sel in final]


def _flash_update_biased(state, s, v_aug):
    m, acc = state
    m_new = jnp.maximum(m, jnp.max(s, axis=0, keepdims=True))
    p = jnp.exp2(s - m_new)
    acc = jnp.exp2(m - m_new) * acc + _dot(v_aug, p.astype(BF16))
    return m_new, acc


def _cmp_mask(n_rows, pos_q):
    r = lax.broadcasted_iota(jnp.int32, (n_rows, 1), 0)
    return (r >= 1) & (r * CMP_STRIDE + (CMP_LEN - CMP_STRIDE - 1) <= pos_q)


WIN_KEYS = WINDOW + Q_BLOCK
P_HEADS = 4
BLOCKS_PER_TILE = SEL_TILE // SEL_BLOCK


def _nsa_prompt_kernel(qt_ref, kc_ref, vct_ref, ks_ref, vst_ref, kw_ref, vwt_ref, gt_ref, mt_ref, o_ref, sel_ref, s_ref):
    i = pl.program_id(1)
    cols = NSA_GROUP * Q_BLOCK
    lane = lax.broadcasted_iota(jnp.int32, (1, Q_BLOCK), 1)
    pos_q = i * Q_BLOCK + lane
    tile4 = lambda a: jnp.concatenate([a] * NSA_GROUP, axis=1)
    heads = range(P_HEADS)
    q_ts = []
    for h in heads:
        q_blk = qt_ref[h * NSA_GROUP * HEAD_DIM:(h + 1) * NSA_GROUP * HEAD_DIM, :]
        q_ts.append(jnp.concatenate([q_blk[g * HEAD_DIM:(g + 1) * HEAD_DIM, :] for g in range(NSA_GROUP)], axis=1))

    nc = kc_ref.shape[1]
    s_win = [_dot(kw_ref[h, pl.ds(pl.multiple_of(i * Q_BLOCK, Q_BLOCK), WIN_KEYS), :], q_ts[h]) for h in heads]

    def compressed_prefix(n):
        def run():
            mask_c = tile4(_cmp_mask(n, pos_q))
            outs = []
            for h in heads:
                p_c = _softmax0(_dot(kc_ref[h, :n, :], q_ts[h]), mask_c)
                o = _dot(vct_ref[h, :, :n], p_c.astype(BF16))
                pg = p_c[:, 0:Q_BLOCK]
                for g in range(1, NSA_GROUP):
                    pg = pg + p_c[:, g * Q_BLOCK:(g + 1) * Q_BLOCK]
                outs += [o, _split_dot(mt_ref[:, :n], pg)]
            return tuple(outs)
        return run

    last_entry = (i * Q_BLOCK + Q_BLOCK - 1) // CMP_STRIDE
    res = lax.switch(last_entry // LANE, [compressed_prefix(n) for n in range(LANE, nc + 1, LANE)])
    o_c, imps = list(res[0::2]), list(res[1::2])
    for h, sel in enumerate(_top_blocks(imps, pos_q // SEL_BLOCK)):
        sel_ref[h] = sel

    zeros_q = jnp.zeros((LANE - HEAD_DIM - 16, cols), BF16)

    def q_aug(h, j):
        grp = sel_ref[h, pl.ds(pl.multiple_of(j * BLOCKS_PER_TILE, BLOCKS_PER_TILE), BLOCKS_PER_TILE), :]
        bias = jnp.concatenate([jnp.where(grp > 0.5, 0.0, NEG), jnp.zeros_like(grp)], axis=0)
        return jnp.concatenate([q_ts[h], tile4(bias).astype(BF16), zeros_q], axis=0)

    def scores(h, j):
        return _dot(ks_ref[h, pl.ds(pl.multiple_of(j * SEL_TILE, SEL_TILE), SEL_TILE), :], q_aug(h, j))

    def sel_body(j, states):
        states = list(states)
        s_cur = s_ref[...]
        for h in heads:
            s_next = scores(h + 1, j) if h + 1 < P_HEADS else scores(0, j + 1)
            states[h] = _flash_update_biased(states[h], s_cur, vst_ref[h, j])
            s_cur = s_next
        s_ref[...] = s_cur
        return tuple(states)

    init = tuple((jnp.full((1, cols), NEG, F32), jnp.zeros((V_AUG, cols), F32)) for _ in heads)
    j_diag = (i * Q_BLOCK) // SEL_TILE
    s_ref[...] = scores(0, 0)
    states = lax.fori_loop(0, j_diag, sel_body, init)
    key_pos = j_diag * SEL_TILE + lax.broadcasted_iota(jnp.int32, (SEL_TILE, 1), 0)
    causal = tile4(jnp.where(key_pos <= pos_q, 0.0, NEG))
    last = [s_ref[...]] + [scores(h, j_diag) for h in heads[1:]]
    o_s = []
    for h in heads:
        _, acc = _flash_update_biased(states[h], last[h] + causal, vst_ref[h, j_diag])
        o_s.append(acc[:HEAD_DIM] / jnp.maximum(acc[HEAD_DIM:HEAD_DIM + 1], TINY))

    w_pos = i * Q_BLOCK - WINDOW + lax.broadcasted_iota(jnp.int32, (WIN_KEYS, 1), 0)
    mask_w = tile4((w_pos <= pos_q) & (w_pos > pos_q - WINDOW) & (w_pos >= 0))
    o_w = []
    for h in heads:
        p_w, denom = _masked_exp0(s_win[h], mask_w)
        p_w = p_w.astype(BF16)
        acc = jnp.zeros((HEAD_DIM, cols), F32)
        for w in range(WIN_KEYS // Q_BLOCK):
            acc = acc + _dot(vwt_ref[h, i + w], p_w[w * Q_BLOCK:(w + 1) * Q_BLOCK, :])
        o_w.append(acc / denom)

    for h in heads:
        gt = jax.nn.sigmoid(gt_ref[h])
        outs = []
        for g in range(NSA_GROUP):
            sl = slice(g * Q_BLOCK, (g + 1) * Q_BLOCK)
            outs.append(gt[3 * g:3 * g + 1, :] * o_c[h][:, sl] + gt[3 * g + 1:3 * g + 2, :] * o_s[h][:, sl]
                        + gt[3 * g + 2:3 * g + 3, :] * o_w[h][:, sl])
        for pair in range(NSA_GROUP // 2):
            both = jnp.concatenate([outs[2 * pair], outs[2 * pair + 1]], axis=0)
            lo = (h * NSA_GROUP // 2 + pair) * LANE
            o_ref[:, lo:lo + LANE] = both.T


def nsa_prompt(qt, kc, vct, ks, vst, kw, vwt, gt, mt, t):
    nq = t // Q_BLOCK
    head = lambda a: pl.BlockSpec((P_HEADS,) + a.shape[1:], lambda h, i: (h,) + (0,) * (a.ndim - 1),
                                  pipeline_mode=pl.Buffered(1))
    width = P_HEADS * NSA_GROUP * HEAD_DIM
    return pl.pallas_call(
        _nsa_prompt_kernel,
        grid=(NSA_KV_HEADS // P_HEADS, nq),
        in_specs=[pl.BlockSpec((width, Q_BLOCK), lambda h, i: (h, i)),
                  head(kc), head(vct), head(ks), head(vst), head(kw), head(vwt),
                  pl.BlockSpec((P_HEADS, 16, Q_BLOCK), lambda h, i: (h, 0, i)),
                  pl.BlockSpec(mt.shape, lambda h, i: (0, 0))],
        out_specs=pl.BlockSpec((Q_BLOCK, width), lambda h, i: (i, h)),
        out_shape=jax.ShapeDtypeStruct((t, NSA_WIDTH), F32),
        scratch_shapes=[pltpu.VMEM((P_HEADS, mt.shape[0], Q_BLOCK), F32),
                        pltpu.VMEM((SEL_TILE, NSA_GROUP * Q_BLOCK), F32)],
        compiler_params=_cparams(("parallel", "arbitrary")),
        name="nsa_p",
    )(qt, kc, vct, ks, vst, kw, vwt, gt, mt)


def _cmp_to_sel_t(n_rows, n_blk, n_blk_pad):
    cs = (np.arange(n_rows)[None, :] - 1) * CMP_STRIDE
    js = np.arange(n_blk_pad)[:, None] * SEL_BLOCK
    m = (cs < js + SEL_BLOCK) & (cs + CMP_LEN > js) & (np.arange(n_rows)[None, :] >= 1) & (np.arange(n_blk_pad)[:, None] < n_blk)
    return jnp.asarray(m.astype(np.float32), dtype=BF16)


S_PAGES = 16
S_COLS = NSA_HEADS * 8


def _softmax_rows(s, mask):
    s = jnp.where(mask, s, NEG)
    m = jnp.max(s, axis=1, keepdims=True)
    p = jnp.where(mask, jnp.exp2(s - m), 0.0)
    return p / jnp.maximum(jnp.sum(p, axis=1, keepdims=True), TINY)


def _flash_rows(state, s, v_t):
    m, l, acc = state
    m_new = jnp.maximum(m, jnp.max(s, axis=1, keepdims=True))
    p = jnp.exp2(s - m_new)
    alpha = jnp.exp2(m - m_new)
    return m_new, alpha * l + jnp.sum(p, axis=1, keepdims=True), alpha * acc + _dot_nt(p.astype(BF16), v_t)


def _nsa_sample_kernel(n_steps, pt_ref, *refs):
    pages = refs[:S_PAGES]
    (qb_ref, kct_ref, vct_ref, cw_ref, nw_ref, ns_ref, g_ref, mt_ref, gsum_ref, emat_ref, o_ref,
     sel_ref, m_ref, l_ref, acc_ref, oc_ref, ow_ref) = refs[S_PAGES:]
    s_id = pl.program_id(1)
    past = n_steps * S_PAGES * PAGE
    qb = qb_ref[0]
    pos_q = past + lax.broadcasted_iota(jnp.int32, (S_COLS, 1), 0) % 8
    blocks_per_step = S_PAGES * PAGE // SEL_BLOCK

    def block_bias(grp, n_keys):
        flags = jnp.concatenate([grp, jnp.zeros((LANE - grp.shape[0], S_COLS), F32)], axis=0).T
        return _dot(jnp.where(flags > 0.5, 0.0, NEG).astype(BF16), emat_ref[:, :n_keys])

    @pl.when(s_id == 0)
    def _():
        n_ent = kct_ref.shape[2]
        wb = cw_ref.shape[2]
        k_win = jnp.concatenate([cw_ref[0, :HALF_ROW, :], nw_ref[0, :HALF_ROW, :]], axis=1).astype(BF16)
        s = _dot(qb, kct_ref[0].astype(BF16))
        s_win = _dot(qb, k_win)
        ent = lax.broadcasted_iota(jnp.int32, (1, n_ent), 1)
        p_c = _softmax_rows(s, (ent >= 1) & (ent * CMP_STRIDE + (CMP_LEN - CMP_STRIDE - 1) <= pos_q))
        oc_ref[...] = _dot_nt(p_c.astype(BF16), vct_ref[0].astype(BF16))
        imp = _split_dot(mt_ref[...], p_c.T)
        imp = _split_dot_r(imp, gsum_ref[...])
        pos_row = past + lax.broadcasted_iota(jnp.int32, (1, S_COLS), 1) % 8
        sel_ref[...] = _top_blocks([imp], pos_row // SEL_BLOCK)[0]

        v_t = jnp.concatenate([cw_ref[0, HALF_ROW:, :], nw_ref[0, HALF_ROW:, :]], axis=1).astype(BF16)
        w_pos = past - wb + lax.broadcasted_iota(jnp.int32, (1, wb + PAGE), 1)
        p_w = _softmax_rows(s_win, (w_pos <= pos_q) & (w_pos > pos_q - WINDOW) & (w_pos >= 0))
        ow_ref[...] = _dot_nt(p_w.astype(BF16), v_t)

        nblk0 = past // SEL_BLOCK
        key_pos = past + lax.broadcasted_iota(jnp.int32, (1, PAGE), 1)
        s = (_dot(qb, ns_ref[0, :HALF_ROW, :].astype(BF16)) + block_bias(sel_ref[nblk0:nblk0 + 8, :], PAGE)
             + jnp.where(key_pos <= pos_q, 0.0, NEG))
        init = (jnp.full((S_COLS, 1), NEG, F32), jnp.zeros((S_COLS, 1), F32), jnp.zeros((S_COLS, HALF_ROW), F32))
        m_ref[...], l_ref[...], acc_ref[...] = _flash_rows(init, s, ns_ref[0, HALF_ROW:, :].astype(BF16))

    grp = sel_ref[pl.ds(pl.multiple_of(s_id * blocks_per_step, blocks_per_step), blocks_per_step), :]
    k_t = jnp.concatenate([pg[0, :HALF_ROW, :] for pg in pages], axis=1).astype(BF16)
    v_t = jnp.concatenate([pg[0, HALF_ROW:, :] for pg in pages], axis=1).astype(BF16)
    s = _dot(qb, k_t) + block_bias(grp, S_PAGES * PAGE)
    st = _flash_rows((m_ref[...], l_ref[...], acc_ref[...]), s, v_t)
    m_ref[...], l_ref[...], acc_ref[...] = st

    @pl.when(s_id == n_steps - 1)
    def _():
        g = jax.nn.sigmoid(g_ref[0])
        o_s = st[2] / jnp.maximum(st[1], TINY)
        o_ref[0] = g[:, 0:1] * oc_ref[...] + g[:, 1:2] * o_s + g[:, 2:3] * ow_ref[...]


def nsa_sample(pool_t, page_table, qb, kct, vct, cache_wt, new_w, new_s, g, mt, gsum, emat):
    b, n_pages = page_table.shape
    steps = n_pages // S_PAGES
    page_spec = lambda k: pl.BlockSpec((1, KV_ROW, PAGE), lambda bi, si, pt, k=k: (pt[bi, si * S_PAGES + k], 0, 0))
    per_b = lambda a: pl.BlockSpec((1,) + a.shape[1:], lambda bi, si, pt: (bi,) + (0,) * (a.ndim - 1))
    const = lambda a: pl.BlockSpec(a.shape, lambda bi, si, pt: (0,) * a.ndim)
    grid_spec = pltpu.PrefetchScalarGridSpec(
        num_scalar_prefetch=1,
        grid=(b, steps),
        in_specs=[page_spec(k) for k in range(S_PAGES)] + [
            per_b(qb), per_b(kct), per_b(vct), per_b(cache_wt), per_b(new_w), per_b(new_s), per_b(g),
            const(mt), const(gsum), const(emat)],
        out_specs=pl.BlockSpec((1, S_COLS, HALF_ROW), lambda bi, si, pt: (bi, 0, 0)),
        scratch_shapes=[pltpu.VMEM((mt.shape[0], S_COLS), F32), pltpu.VMEM((S_COLS, 1), F32), pltpu.VMEM((S_COLS, 1), F32),
                        pltpu.VMEM((S_COLS, HALF_ROW), F32), pltpu.VMEM((S_COLS, HALF_ROW), F32),
                        pltpu.VMEM((S_COLS, HALF_ROW), F32)],
    )
    return pl.pallas_call(
        functools.partial(_nsa_sample_kernel, steps),
        grid_spec=grid_spec,
        out_shape=jax.ShapeDtypeStruct((b, S_COLS, HALF_ROW), F32),
        compiler_params=_cparams(("parallel", "arbitrary")),
        name="nsa_s",
    )(page_table, *([pool_t] * S_PAGES), qb, kct, vct, cache_wt, new_w, new_s, g, mt, gsum, emat)


GLA_SUB = 16


def _gla_head(q, k, v, cum, state):
    c = q.shape[0]
    sub = min(GLA_SUB, c)
    lane = lax.broadcasted_iota(jnp.int32, (1, LANE), 1)
    t_sub = lax.broadcasted_iota(jnp.int32, (sub, 1), 0)
    row_pad = lambda a: jnp.concatenate([a, jnp.zeros((LANE - c, a.shape[1]), F32)], axis=0).astype(BF16)
    v_pad = row_pad(v)
    o = _dot((q * jnp.exp(cum)).astype(BF16), state.astype(BF16))
    blocks = []
    for r0 in range(0, c, sub):
        q_i, cum_i = q[r0:r0 + sub], cum[r0:r0 + sub]
        if r0 == 0:
            att_i = jnp.zeros((sub, LANE), F32)
        else:
            base = cum[r0 - 1:r0]
            q_dec = (q_i * jnp.exp(cum_i - base)).astype(BF16)
            k_dec = row_pad(k * jnp.exp(jnp.minimum(base - cum, 0.0)))
            att_i = jnp.where(lane < r0, _dot_nt(q_dec, k_dec), 0.0)
        for s in range(r0, r0 + sub):
            decay = jnp.exp(jnp.where(t_sub >= s - r0, cum_i - cum[s:s + 1], NEG))
            column = jnp.sum(q_i * k[s:s + 1] * decay, axis=-1, keepdims=True)
            att_i = jnp.where(lane == s, column, att_i)
        blocks.append(att_i)
    att = jnp.concatenate(blocks, axis=0)
    o = o + _dot(att.astype(BF16), v_pad)

    c_last = cum[c - 1:c]
    k_end = jnp.concatenate([k * jnp.exp(c_last - cum), jnp.zeros((LANE - c, GLA_DK), F32)], axis=0)
    eye = lax.broadcasted_iota(jnp.int32, (GLA_DK, GLA_DK), 0) == lax.broadcasted_iota(jnp.int32, (GLA_DK, GLA_DK), 1)
    decay_col = jnp.sum(jnp.where(eye, jnp.exp(c_last), 0.0), axis=1, keepdims=True)
    return o, decay_col * state + _dot(k_end.T.astype(BF16), v_pad)


def _gla_kernel(q_ref, k_ref, v_ref, a_ref, z_ref, wa_ref, ba_ref, nw_ref, s0_ref, o_ref, so_ref, s_ref):
    c = q_ref.shape[0]
    ci = pl.program_id(1)

    @pl.when(ci == 0)
    def _():
        s_ref[...] = s0_ref[0]

    pre = _dot(a_ref[...].astype(BF16), wa_ref[...].astype(BF16)) + ba_ref[...]
    log_a = (jnp.minimum(pre, 0.0) - jnp.log1p(jnp.exp(-jnp.abs(pre)))) / GLA_TAU
    t_idx = lax.broadcasted_iota(jnp.int32, (c, 1), 0)
    cum = log_a
    sh = 1
    while sh < c:
        cum = cum + jnp.where(t_idx >= sh, pltpu.roll(cum, sh, 0), 0.0)
        sh *= 2
    for h in range(GLA_HEADS):
        ks = slice(h * GLA_DK, (h + 1) * GLA_DK)
        vs = slice(h * GLA_DV, (h + 1) * GLA_DV)
        o, new_state = _gla_head(q_ref[:, ks] * (GLA_DK ** -0.5), k_ref[:, ks], v_ref[:, vs], cum[:, ks], s_ref[h])
        s_ref[h] = new_state
        y = o * lax.rsqrt(jnp.mean(o * o, axis=-1, keepdims=True) + EPS) * nw_ref[...]
        o_ref[:, vs] = y * _silu(z_ref[:, vs])

    @pl.when(ci == pl.num_programs(1) - 1)
    def _():
        so_ref[0] = s_ref[...]


def gla(proj, w_a2p, b_a, gla_norm_w, s0, n_seq, chunk):
    rows = proj.shape[0]
    n_chunk = rows // (n_seq * chunk)
    kw, vw = GLA_HEADS * GLA_DK, GLA_HEADS * GLA_DV
    rows_at = lambda width, col: pl.BlockSpec((chunk, width), lambda b, c: (b * n_chunk + c, col // width))
    state_spec = pl.BlockSpec((1, GLA_HEADS, GLA_DK, GLA_DV), lambda b, c: (b, 0, 0, 0))
    return pl.pallas_call(
        _gla_kernel,
        grid=(n_seq, n_chunk),
        in_specs=[rows_at(kw, C_QG), rows_at(kw, C_KG), rows_at(vw, C_VG), rows_at(LANE, C_AG), rows_at(vw, C_ZG),
                  pl.BlockSpec((LANE, kw), lambda b, c: (0, 0)),
                  pl.BlockSpec((1, kw), lambda b, c: (0, 0)),
                  pl.BlockSpec((1, GLA_DV), lambda b, c: (0, 0)),
                  state_spec],
        out_specs=[pl.BlockSpec((chunk, vw), lambda b, c: (b * n_chunk + c, 0)), state_spec],
        out_shape=[jax.ShapeDtypeStruct((rows, vw), F32),
                   jax.ShapeDtypeStruct((n_seq, GLA_HEADS, GLA_DK, GLA_DV), F32)],
        scratch_shapes=[pltpu.VMEM((GLA_HEADS, GLA_DK, GLA_DV), F32)],
        compiler_params=_cparams(("parallel", "arbitrary")),
        name="gla",
    )(proj, proj, proj, proj, proj, w_a2p, b_a.reshape(1, -1), gla_norm_w.reshape(1, -1), s0)


def _out_kernel(on_ref, zn_ref, og_ref, mn_ref, mg_ref, x_ref, gate_ref, wn_ref, wg_ref, wo_ref, fw_ref, y_ref):
    o_nsa = (on_ref[...] * _silu(zn_ref[...])).astype(BF16)
    merged = (jax.nn.sigmoid(mn_ref[...]) * _dot(o_nsa, wn_ref[...])
              + jax.nn.sigmoid(mg_ref[...]) * _dot(og_ref[...].astype(BF16), wg_ref[...]))
    y = x_ref[...] + gate_ref[...] * _dot(merged.astype(BF16), wo_ref[...])
    y_ref[...] = y * lax.rsqrt(jnp.mean(y * y, axis=-1, keepdims=True) + EPS) * fw_ref[...]


def out_proj(o_nsa, o_gla, proj, x, gate, w_o_nsa, w_o_gla, w_out, final_norm_w, tm):
    rows = x.shape[0]
    per_row = gate.shape[0] != 1
    gate_spec = pl.BlockSpec((tm, D_MODEL), lambda i: (i, 0)) if per_row else pl.BlockSpec((1, D_MODEL), lambda i: (0, 0))
    resident = lambda a: pl.BlockSpec(a.shape, lambda i: (0, 0), pipeline_mode=pl.Buffered(1))
    return pl.pallas_call(
        _out_kernel,
        grid=(rows // tm,),
        in_specs=[pl.BlockSpec((tm, NSA_WIDTH), lambda i: (i, 0)),
                  pl.BlockSpec((tm, NSA_WIDTH), lambda i: (i, C_ZN // NSA_WIDTH)),
                  pl.BlockSpec((tm, NSA_WIDTH), lambda i: (i, 0)),
                  pl.BlockSpec((tm, D_MODEL), lambda i: (i, C_MN // D_MODEL)),
                  pl.BlockSpec((tm, D_MODEL), lambda i: (i, C_MG // D_MODEL)),
                  pl.BlockSpec((tm, D_MODEL), lambda i: (i, 0)),
                  gate_spec, resident(w_o_nsa), resident(w_o_gla), resident(w_out),
                  pl.BlockSpec((1, D_MODEL), lambda i: (0, 0))],
        out_specs=pl.BlockSpec((tm, D_MODEL), lambda i: (i, 0)),
        out_shape=jax.ShapeDtypeStruct((rows, D_MODEL), F32),
        compiler_params=_cparams(("parallel",)),
        name="outproj",
    )(o_nsa, proj, o_gla, proj, proj, x, gate, w_o_nsa, w_o_gla, w_out, final_norm_w.reshape(1, D_MODEL))


def _feature_major(a):
    lead = a.shape[:-4]
    n = len(lead)
    a = jnp.transpose(a, tuple(range(n)) + (n + 1, n + 2, n + 3, n))
    return a.reshape(lead + (KV_ROW, a.shape[-1]))


def _token_major(a_t, lead):
    rows = a_t.shape[-1]
    a = a_t.reshape(a_t.shape[:-2] + (2, NSA_KV_HEADS, HEAD_DIM, rows))
    n = a.ndim - 4
    a = jnp.transpose(a, tuple(range(n)) + (n + 3, n, n + 1, n + 2))
    return a.reshape(lead + (rows, 2, NSA_KV_HEADS, HEAD_DIM))


def kernel(x_prompt, x_sample, cache_kv_cmp, cache_kv_sel, cache_kv_win, state_gla, page_table, c_prompt, c_sample, norm_w, w_ada, b_ada, w_in, cmp_pos, cmp_w1, cmp_b1, cmp_w2, cmp_b2, w_a2, b_a, gla_norm_w, w_o_nsa, w_o_gla, w_out, final_norm_w):
    assert x_prompt.shape[0] == 1 and norm_w.shape[0] == 1, "one prompt sequence, one layer"
    t_p = x_prompt.shape[1]
    b_s, t_s = x_sample.shape[:2]
    past = page_table.shape[1] * PAGE
    wb = cache_kv_win.shape[2]
    assert t_s == 8 and wb == WINDOW and past % (S_PAGES * PAGE) == 0 and t_p % SEL_TILE == 0
    rows_s = b_s * t_s

    c_rows = jnp.zeros((40, D_MODEL), F32).at[0:1].set(c_prompt).at[1:1 + b_s].set(c_sample)
    mod = ada_mod(c_rows, w_ada[0], b_ada[0])
    shift, scale, gate = mod[:, :D_MODEL], mod[:, D_MODEL:2 * D_MODEL], mod[:, 2 * D_MODEL:]
    per_row = lambda a: jnp.repeat(a[1:1 + b_s], t_s, axis=0)

    w_t = jnp.transpose(w_in[0])
    xp = x_prompt.reshape(t_p, D_MODEL)
    xs = x_sample.reshape(rows_s, D_MODEL)
    tp, ts = _row_tiles(t_p), _row_tiles(rows_s)
    h_p = modulated_norm(xp, scale[0:1], shift[0:1], norm_w[0], tp['norm'])
    h_s = modulated_norm(xs, per_row(scale), per_row(shift), norm_w[0], ts['norm'])
    proj_p = in_proj(h_p, w_t, RM_OFFSETS, tp['proj_rm'], RM_TILE, False)
    projt_p = in_proj(h_p, w_t, FM_OFFSETS, tp['proj_fm'], FM_TILE, True)
    proj_s = in_proj(h_s, w_t, RM_OFFSETS, ts['proj_rm'], RM_TILE, False)
    projt_s = in_proj(h_s, w_t, FM_OFFSETS, ts['proj_fm'], FM_TILE, True)

    cos_p, sin_p = _rope_tables(jnp.arange(t_p, dtype=jnp.int32))
    cos_s, sin_s = _rope_tables(jnp.tile(past + jnp.arange(t_s, dtype=jnp.int32), b_s))
    qt_p, kvc_p, kvs_p, kvw_p, ks_p, kw_p, vst_p, vwt_p = rope_stage(projt_p, cos_p, sin_p, tp['rope'], True)
    qt_s, kvc_s, kvs_s, kvw_s = rope_stage(projt_s, cos_s, sin_s, ts['rope'], False)

    pb = pos_bias(cmp_pos[0], cmp_w1[0], cmp_b1[0])
    cmp_consts = _compress_weights(cmp_w1[0], cmp_w2[0], cmp_b2[0])
    perm, w1t, w2k, w2t, b2k, b2c = cmp_consts
    ident = jnp.arange(t_p // PAGE, dtype=jnp.int32)[None, :]
    kc_p, _, vct_p = compress(kvc_p, ident, perm, w1t, pb, w2k, w2t, b2k, b2c)
    pool_c = _feature_major(cache_kv_cmp[0])
    _, kct_s, vct_s = compress(pool_c, page_table, perm, w1t, pb, w2k, w2t, b2k, b2c)

    n_ent = kc_p.shape[1]
    kc_h = jnp.transpose(kc_p[0].reshape(n_ent, NSA_KV_HEADS, HEAD_DIM), (1, 0, 2)).astype(BF16)
    vct_h = vct_p[0].reshape(NSA_KV_HEADS, HEAD_DIM, n_ent).astype(BF16)
    kw_h = jnp.pad(kw_p, ((0, 0), (WINDOW, 0), (0, 0)))
    vwt_h = jnp.pad(vwt_p, ((0, 0), (WINDOW // Q_BLOCK, 0), (0, 0), (0, 0)))
    g_p = projt_p[R_GN:R_GN + 48].reshape(NSA_KV_HEADS, 12, t_p)
    g_p = jnp.pad(g_p, ((0, 0), (0, 4), (0, 0)))
    mt_p = _cmp_to_sel_t(n_ent, t_p // SEL_BLOCK, t_p // SEL_BLOCK)
    o_nsa_p = nsa_prompt(qt_p, kc_h, vct_h, ks_p, vst_p, kw_h, vwt_h, g_p, mt_p, t_p)

    q5 = qt_s.reshape(NSA_KV_HEADS, NSA_GROUP, HEAD_DIM, b_s, t_s)
    q_c = jnp.transpose(q5, (3, 0, 1, 4, 2)).reshape(b_s, NSA_KV_HEADS, NSA_GROUP * t_s, HEAD_DIM)
    eye = jnp.eye(NSA_KV_HEADS, dtype=BF16)
    qb = jnp.einsum('bhcd,hk->bhckd', q_c, eye).reshape(b_s, S_COLS, HALF_ROW)
    new_keys = lambda a_t: jnp.pad(jnp.transpose(a_t.reshape(KV_ROW, b_s, t_s), (1, 0, 2)), ((0, 0), (0, 0), (0, PAGE - t_s)))
    g_s = projt_s[R_GN:R_GN + 48].reshape(NSA_KV_HEADS, NSA_GROUP, 3, b_s, t_s)
    g_s = jnp.transpose(g_s, (3, 0, 1, 4, 2)).reshape(b_s, S_COLS, 3)
    g_s = jnp.pad(g_s, ((0, 0), (0, 0), (0, 5)))
    n_blk_s = -(-(past + t_s) // SEL_BLOCK)
    mt_s = _cmp_to_sel_t(kct_s.shape[2], n_blk_s, -(-n_blk_s // 8) * 8)
    col = np.arange(S_COLS)
    gsum = jnp.asarray(((col[:, None] // 32 == col[None, :] // 32) & (col[:, None] % 8 == col[None, :] % 8)).astype(np.float32), dtype=BF16)
    emat = jnp.asarray((np.arange(S_PAGES * PAGE)[None, :] // SEL_BLOCK == np.arange(LANE)[:, None]).astype(np.float32), dtype=BF16)
    cache_wt = _feature_major(cache_kv_win[0])
    o_all = nsa_sample(_feature_major(cache_kv_sel[0]), page_table, qb, kct_s, vct_s, cache_wt,
                       new_keys(kvw_s), new_keys(kvs_s), g_s, mt_s, gsum, emat)
    o6 = o_all.reshape(b_s, NSA_KV_HEADS, NSA_GROUP, t_s, NSA_KV_HEADS, HEAD_DIM)
    o_nsa_s = jnp.stack([o6[:, h, :, :, h, :] for h in range(NSA_KV_HEADS)], axis=1)
    o_nsa_s = jnp.transpose(o_nsa_s, (0, 3, 1, 2, 4)).reshape(rows_s, NSA_WIDTH)

    w_a2p = jnp.zeros((LANE, GLA_HEADS * GLA_DK), F32).at[:GLA_RANK].set(w_a2[0])
    s0_p = jnp.zeros((1, GLA_HEADS, GLA_DK, GLA_DV), F32)
    o_gla_p, st_p = gla(proj_p, w_a2p, b_a[0], gla_norm_w[0], s0_p, 1, GLA_CHUNK)
    o_gla_s, st_s = gla(proj_s, w_a2p, b_a[0], gla_norm_w[0], state_gla[0], b_s, t_s)

    wn, wg, wo = w_o_nsa[0].astype(BF16), w_o_gla[0].astype(BF16), w_out[0].astype(BF16)
    y_p = out_proj(o_nsa_p, o_gla_p, proj_p, xp, gate[0:1], wn, wg, wo, final_norm_w, tp['out'])
    y_s = out_proj(o_nsa_s, o_gla_s, proj_s, xs, per_row(gate), wn, wg, wo, final_norm_w, ts['out'])

    sample_rows = lambda a_t: _token_major(jnp.transpose(a_t.reshape(KV_ROW, b_s, t_s), (1, 0, 2)), (1, b_s))
    win_t = jnp.concatenate([cache_wt, jnp.transpose(kvw_s.reshape(KV_ROW, b_s, t_s), (1, 0, 2))], axis=2)[:, :, t_s:]
    n_win = min(WINDOW, t_p)
    return (y_p.reshape(x_prompt.shape), y_s.reshape(x_sample.shape),
            _token_major(kvc_p, (1, 1)), sample_rows(kvc_s), _token_major(kvs_p, (1, 1)), sample_rows(kvs_s),
            _token_major(kvw_p[:, t_p - n_win:], (1, 1)), _token_major(win_t, (1, b_s)),
            st_p[None], st_s[None])
```
